```python
import jax
import jax.numpy as jnp
from jax import lax
import numpy as np

D_MODEL = 1024
BATCH = 8
SEQ = 4096
DEPTH = 2

CTX_LEN = 256
GRID_W = 64
NORM_EPS = 1e-6

HEAD_DIM = 64
MIX_WIDTH = D_MODEL
NA_WIDTH = MIX_WIDTH // 2
NA_HEADS = NA_WIDTH // HEAD_DIM
NA_WIN_H_MAX = 8
NA_WIN_W = 16
GLA_WIDTH = MIX_WIDTH // 4
GLA_HEADS = GLA_WIDTH // HEAD_DIM
GLA_HEAD_V = HEAD_DIM
GLA_HEAD_K = HEAD_DIM // 2
GLA_KEY = GLA_HEADS * GLA_HEAD_K
GLA_GK_RANK = 16
GLA_GATE_NORM = 16.0
HGRN_WIDTH = MIX_WIDTH - NA_WIDTH - GLA_WIDTH
HGRN_HEADS = HGRN_WIDTH // HEAD_DIM
HGRN_HEAD_DIM = HEAD_DIM

CHUNK = 16
ROPE_BASE = 10000.0

IN_SPLITS = (NA_WIDTH, NA_WIDTH, NA_WIDTH, NA_WIDTH,
             GLA_KEY, GLA_KEY, GLA_WIDTH, 2 * GLA_GK_RANK, GLA_WIDTH,
             HGRN_WIDTH, HGRN_WIDTH, HGRN_WIDTH, HGRN_WIDTH, HGRN_WIDTH)
IN_WIDTH = sum(IN_SPLITS)

kernel_name = 'hybrid_na_gla_hgrn2_prefix_dit'


def rmsnorm(x, w):
    xf = x.astype(jnp.float32)
    y = xf * lax.rsqrt(jnp.mean(xf * xf, axis=-1, keepdims=True) + NORM_EPS)
    return (y * w.astype(jnp.float32)).astype(x.dtype)


def split_columns(p):
    parts, start = [], 0
    for width in IN_SPLITS:
        parts.append(p[..., start:start + width])
        start += width
    return parts


def axial_rope(x):
    n_tok, dk = x.shape[1], x.shape[-1]
    half = dk // 2
    t = jnp.arange(n_tok)
    inv_freq = 1.0 / (ROPE_BASE ** (jnp.arange(0, half, 2, dtype=jnp.float32) / half))

    def rot(xa, pos):
        ang = pos.astype(jnp.float32)[:, None] * inv_freq[None, :]
        cos = jnp.cos(ang)[None, :, None, :]
        sin = jnp.sin(ang)[None, :, None, :]
        x1, x2 = jnp.split(xa.astype(jnp.float32), 2, axis=-1)
        return jnp.concatenate([x1 * cos - x2 * sin, x1 * sin + x2 * cos], axis=-1)

    x_row, x_col = jnp.split(x, 2, axis=-1)
    out = jnp.concatenate([rot(x_row, t // GRID_W), rot(x_col, t % GRID_W)], axis=-1)
    return out.astype(x.dtype)


def chunk_gated_scan(q, k, v, log_f, s0):
    b, h, t, dk = q.shape
    dv = v.shape[-1]
    n = t // CHUNK
    f32 = jnp.float32
    qc = q.reshape(b, h, n, CHUNK, dk).astype(f32)
    kc = k.reshape(b, h, n, CHUNK, dk).astype(f32)
    vc = v.reshape(b, h, n, CHUNK, dv).astype(f32)
    G = jnp.cumsum(log_f.reshape(b, h, n, CHUNK, dk).astype(f32), axis=3)
    g_ref = G[:, :, :, CHUNK // 2 - 1:CHUNK // 2, :]
    g_last = G[:, :, :, -1:, :]
    scores = jnp.einsum('bhnid,bhnjd->bhnij', qc * jnp.exp(G - g_ref), kc * jnp.exp(g_ref - G))
    tri = jnp.tril(jnp.ones((CHUNK, CHUNK), dtype=bool))
    scores = jnp.where(tri, scores, 0.0)
    o_intra = jnp.einsum('bhnij,bhnjv->bhniv', scores, vc)
    q_in = jnp.moveaxis(qc * jnp.exp(G), 2, 0)
    k_out = jnp.moveaxis(kc * jnp.exp(g_last - G), 2, 0)
    v_s = jnp.moveaxis(vc, 2, 0)
    decay = jnp.moveaxis(jnp.exp(g_last[:, :, :, 0, :]), 2, 0)

    def step(state, xs):
        q_i, k_i, v_i, d_i = xs
        o_i = jnp.einsum('bhcd,bhdv->bhcv', q_i, state)
        state = d_i[..., None] * state + jnp.einsum('bhcd,bhcv->bhdv', k_i, v_i)
        return state, o_i

    s_final, o_inter = lax.scan(step, s0.astype(f32), (q_in, k_out, v_s, decay))
    o = o_intra + jnp.moveaxis(o_inter, 0, 2)
    return o.reshape(b, h, t, dv).astype(v.dtype), s_final


def bidirectional_prefix_scan(q_c, k_c, v_c, lf_c, q_x, k_x, v_x, lf_x):
    b, h, _, dk = q_c.shape
    dv = v_c.shape[-1]
    s0 = jnp.zeros((b, h, dk, dv), jnp.float32)
    fl = lambda a: jnp.flip(a, axis=2)
    oc_f, sc_f = chunk_gated_scan(q_c, k_c[0], v_c, lf_c[0], s0)
    ox_f, _ = chunk_gated_scan(q_x, k_x[0], v_x, lf_x[0], sc_f)
    oc_b, sc_b = chunk_gated_scan(fl(q_c), fl(k_c[1]), fl(v_c), fl(lf_c[1]), s0)
    ox_b, _ = chunk_gated_scan(fl(q_x), fl(k_x[1]), fl(v_x), fl(lf_x[1]), sc_b)
    return oc_f + fl(oc_b), ox_f + fl(ox_b)


def head_norm_gate(o, w, g):
    o = rmsnorm(jnp.swapaxes(o, 1, 2), w)
    b, t, h, dv = o.shape
    return o.reshape(b, t, h * dv) * jax.nn.silu(g)


def neighborhood_attention(q, k, v, k_ctx, v_ctx, rpb):
    b, t, h, dh = q.shape
    rows = t // GRID_W
    win_h = min(NA_WIN_H_MAX, rows)
    scale = dh ** -0.5
    grid = lambda a: a.reshape(b, rows, GRID_W, h, dh)
    qg, kg, vg = grid(q), grid(k), grid(v)
    col = jnp.arange(GRID_W)
    col_start = jnp.clip(col - NA_WIN_W // 2, 0, GRID_W - NA_WIN_W)
    col_in = (col[None, :] >= col_start[:, None]) & (col[None, :] < col_start[:, None] + NA_WIN_W)
    col_idx = jnp.clip(col[None, :] - col[:, None], 1 - NA_WIN_W, NA_WIN_W - 1) + NA_WIN_W - 1
    rpb_cols = rpb[:, :, col_idx].astype(jnp.float32)
    n_nb = win_h * GRID_W

    def row_block(r):
        row_start = jnp.clip(r - win_h // 2, 0, rows - win_h)
        q_r = lax.dynamic_index_in_dim(qg, r, axis=1, keepdims=False)
        k_r = lax.dynamic_slice_in_dim(kg, row_start, win_h, axis=1)
        v_r = lax.dynamic_slice_in_dim(vg, row_start, win_h, axis=1)
        row_idx = row_start + jnp.arange(win_h) - r + NA_WIN_H_MAX - 1
        bias = jnp.transpose(rpb_cols[:, row_idx], (0, 2, 1, 3))
        s_nb = jnp.einsum('bqhd,brkhd->bhqrk', q_r, k_r).astype(jnp.float32) * scale + bias[None]
        s_nb = jnp.where(col_in[None, None, :, None, :], s_nb, -jnp.inf)
        s_cx = jnp.einsum('bqhd,blhd->bhql', q_r, k_ctx).astype(jnp.float32) * scale
        s = jnp.concatenate([s_nb.reshape(b, h, GRID_W, n_nb), s_cx], axis=-1)
        p = jax.nn.softmax(s, axis=-1).astype(v.dtype)
        p_nb = p[..., :n_nb].reshape(b, h, GRID_W, win_h, GRID_W)
        return (jnp.einsum('bhqrk,brkhd->bqhd', p_nb, v_r)
                + jnp.einsum('bhql,blhd->bqhd', p[..., n_nb:], v_ctx))

    out = lax.map(row_block, jnp.arange(rows))
    return jnp.moveaxis(out, 0, 1).reshape(b, t, h * dh)


def context_attention(q, k, v):
    b, l, h, dh = q.shape
    s = jnp.einsum('bqhd,bkhd->bhqk', q, k).astype(jnp.float32) * dh ** -0.5
    p = jax.nn.softmax(s, axis=-1).astype(v.dtype)
    return jnp.einsum('bhqk,bkhd->bqhd', p, v).reshape(b, l, h * dh)


def gla_inputs(q, k, v, gk, w_gk, b_gk, use_rope):
    b, t, _ = q.shape
    q = q.reshape(b, t, GLA_HEADS, GLA_HEAD_K)
    k = k.reshape(b, t, GLA_HEADS, GLA_HEAD_K)
    if use_rope:
        q = axial_rope(q)
        k = axial_rope(k)
    q = q * GLA_HEAD_K ** -0.5
    z = jnp.einsum('btzr,zrk->zbtk', gk.reshape(b, t, 2, GLA_GK_RANK), w_gk) + b_gk[:, None, None, :]
    lf = jax.nn.log_sigmoid(z.astype(jnp.float32)) / GLA_GATE_NORM
    lf = lf.reshape(2, b, t, GLA_HEADS, GLA_HEAD_K).transpose(0, 1, 3, 2, 4)
    kh = jnp.swapaxes(k, 1, 2)
    vh = jnp.swapaxes(v.reshape(b, t, GLA_HEADS, GLA_HEAD_V), 1, 2)
    return jnp.swapaxes(q, 1, 2), (kh, kh), vh, (lf[0], lf[1])


def hgrn_inputs(q, f_fwd, f_bwd, i, lb):
    b, t, _ = q.shape
    heads = lambda a: a.reshape(b, t, HGRN_HEADS, HGRN_HEAD_DIM).transpose(0, 2, 1, 3)
    f_f = lb[0] + (1.0 - lb[0]) * jax.nn.sigmoid(f_fwd.astype(jnp.float32))
    f_b = lb[1] + (1.0 - lb[1]) * jax.nn.sigmoid(f_bwd.astype(jnp.float32))
    k_pair = (heads((1.0 - f_f).astype(q.dtype)), heads((1.0 - f_b).astype(q.dtype)))
    lf_pair = (heads(jnp.log(f_f)), heads(jnp.log(f_b)))
    return heads(q) * HGRN_HEAD_DIM ** -0.5, k_pair, heads(i), lf_pair


def hybrid_mixer(h_x, h_c, w_in, rpb, w_gk, b_gk, gla_norm, lb, hgrn_norm, w_out, need_ctx):
    b, t, _ = h_x.shape
    px = split_columns(h_x @ w_in)
    pc = split_columns(h_c @ w_in)
    na_heads = lambda a: a.reshape(a.shape[0], a.shape[1], NA_HEADS, HEAD_DIM)
    qa_x, ka_x, va_x = na_heads(px[0]), na_heads(px[1]), na_heads(px[2])
    qa_c, ka_c, va_c = na_heads(pc[0]), na_heads(pc[1]), na_heads(pc[2])
    o_na_x = neighborhood_attention(qa_x, ka_x, va_x, ka_c, va_c, rpb) * jax.nn.silu(px[3])
    gla_c = gla_inputs(pc[4], pc[5], pc[6], pc[7], w_gk, b_gk, False)
    gla_x = gla_inputs(px[4], px[5], px[6], px[7], w_gk, b_gk, True)
    oc_gla, ox_gla = bidirectional_prefix_scan(*gla_c, *gla_x)
    o_gla_x = head_norm_gate(ox_gla, gla_norm, px[8])
    hg_c = hgrn_inputs(pc[9], pc[10], pc[11], pc[12], lb)
    hg_x = hgrn_inputs(px[9], px[10], px[11], px[12], lb)
    oc_hg, ox_hg = bidirectional_prefix_scan(*hg_c, *hg_x)
    o_hg_x = head_norm_gate(ox_hg, hgrn_norm, px[13])
    y_x = jnp.concatenate([o_na_x, o_gla_x, o_hg_x], axis=-1) @ w_out
    if not need_ctx:
        return y_x, None
    o_na_c = context_attention(qa_c, ka_c, va_c) * jax.nn.silu(pc[3])
    o_gla_c = head_norm_gate(oc_gla, gla_norm, pc[8])
    o_hg_c = head_norm_gate(oc_hg, hgrn_norm, pc[13])
    y_c = jnp.concatenate([o_na_c, o_gla_c, o_hg_c], axis=-1) @ w_out
    return y_x, y_c


def setup_inputs(seed: int = 0) -> dict:
    key = jax.random.key(seed)
    ks = jax.random.split(key, 16)
    nrm = lambda k, shape, s: jax.random.normal(k, shape, jnp.float32) * s
    return {
        'x': nrm(ks[0], (BATCH, SEQ, D_MODEL), 1.0),
        'c': nrm(ks[1], (BATCH, D_MODEL), 1.0),
        'ctx': nrm(ks[2], (BATCH, CTX_LEN, D_MODEL), 1.0),
        'c_ctx': nrm(ks[3], (D_MODEL,), 1.0),
        'ada_w': nrm(ks[4], (DEPTH, D_MODEL, 3 * D_MODEL), 0.5 * D_MODEL ** -0.5),
        'ada_b': nrm(ks[5], (DEPTH, 3 * D_MODEL), 0.02),
        'norm_w': 1.0 + nrm(ks[6], (DEPTH, D_MODEL), 0.02),
        'w_in': nrm(ks[7], (DEPTH, D_MODEL, IN_WIDTH), D_MODEL ** -0.5),
        'na_rpb': nrm(ks[8], (DEPTH, NA_HEADS, 2 * NA_WIN_H_MAX - 1, 2 * NA_WIN_W - 1), 0.1),
        'gla_w_gk': nrm(ks[9], (DEPTH, 2, GLA_GK_RANK, GLA_KEY), GLA_GK_RANK ** -0.5),
        'gla_b_gk': nrm(ks[10], (DEPTH, 2, GLA_KEY), 0.1),
        'gla_norm_w': 1.0 + nrm(ks[11], (DEPTH, GLA_HEAD_V), 0.02),
        'hgrn_lower_bounds': nrm(ks[12], (DEPTH, 2, HGRN_WIDTH), 0.1),
        'hgrn_norm_w': 1.0 + nrm(ks[13], (DEPTH, HGRN_HEAD_DIM), 0.02),
        'w_out': nrm(ks[14], (DEPTH, MIX_WIDTH, D_MODEL), MIX_WIDTH ** -0.5),
        'final_norm_w': 1.0 + nrm(ks[15], (D_MODEL,), 0.02),
    }


def reference(x, c, ctx, c_ctx, ada_w, ada_b, norm_w, w_in, na_rpb, gla_w_gk, gla_b_gk,
              gla_norm_w, hgrn_lower_bounds, hgrn_norm_w, w_out, final_norm_w):
    lbs = jnp.cumsum(jax.nn.softmax(hgrn_lower_bounds.astype(jnp.float32), axis=0), axis=0)
    lbs = lbs - lbs[0:1]
    xc = ctx
    for layer in range(DEPTH):
        need_ctx = layer < DEPTH - 1
        mod_x = jax.nn.silu(c) @ ada_w[layer] + ada_b[layer]
        mod_c = jax.nn.silu(c_ctx) @ ada_w[layer] + ada_b[layer]
        sh_x, sc_x, gt_x = jnp.split(mod_x[:, None, :], 3, axis=-1)
        sh_c, sc_c, gt_c = jnp.split(mod_c, 3, axis=-1)
        h_x = rmsnorm(x, norm_w[layer]) * (1.0 + sc_x) + sh_x
        h_c = rmsnorm(xc, norm_w[layer]) * (1.0 + sc_c) + sh_c
        y_x, y_c = hybrid_mixer(h_x, h_c, w_in[layer], na_rpb[layer], gla_w_gk[layer], gla_b_gk[layer],
                                gla_norm_w[layer], lbs[layer], hgrn_norm_w[layer], w_out[layer], need_ctx)
        x = x + gt_x * y_x
        if need_ctx:
            xc = xc + gt_c * y_c
    return rmsnorm(x, final_norm_w)
```

```python
import functools

import numpy as np
import jax
import jax.numpy as jnp
from jax import lax
from jax.experimental import pallas as pl
from jax.experimental.pallas import tpu as pltpu

F32 = jnp.float32
BF16 = jnp.bfloat16

D_MODEL = 1024
CTX_LEN = 256
GRID_W = 64
NORM_EPS = 1e-6
HEAD_DIM = 64
NA_WIDTH = 512
NA_HEADS = 8
NA_WIN_H = 8
NA_WIN_W = 16
GLA_WIDTH = 256
GLA_KEY = 128
GLA_GK_RANK = 16
GLA_GATE_NORM = 16.0
HGRN_WIDTH = 256
SCAN_HEADS = 4
CHUNK = 16
ROPE_BASE = 10000.0

LANES = 128
TOKEN_TILE = 256
SCAN_BLOCK = 128
CHUNKS_PER_BLOCK = SCAN_BLOCK // CHUNK
NA_Q_ROWS = TOKEN_TILE // GRID_W
NA_KEY_ROWS = NA_Q_ROWS + NA_WIN_H - 1
NA_KEYS = NA_KEY_ROWS * GRID_W
VMEM_LIMIT = 56 * 1024 * 1024

IN_SPLITS = (512, 512, 512, 512, 128, 128, 256, 32, 256, 256, 256, 256, 256, 256)
IN_OFFS = tuple(int(v) for v in np.cumsum((0,) + IN_SPLITS))
GK_PAD = LANES - 2 * GLA_GK_RANK
W_NA = 3 * NA_WIDTH
W_GATE = NA_WIDTH + GLA_WIDTH + HGRN_WIDTH
W_GLA = GLA_WIDTH + 2 * GLA_KEY + LANES
W_HG = 4 * HGRN_WIDTH
W_IN_PAD = W_NA + W_GATE + W_GLA + W_HG


def _dot(a, b):
    return jnp.dot(a, b, preferred_element_type=F32)


def _dot_nt(a, b):
    return lax.dot_general(a, b, (((1,), (1,)), ((), ())), preferred_element_type=F32)


def _dot_tn(a, b):
    return lax.dot_general(a, b, (((0,), (0,)), ((), ())), preferred_element_type=F32)


def _sigmoid(x):
    return 1.0 / (1.0 + jnp.exp(-x))


def _silu(x):
    return x * _sigmoid(x)


def _split3(x):
    hi = x.astype(BF16)
    r1 = x - hi.astype(F32)
    mid = r1.astype(BF16)
    lo = (r1 - mid.astype(F32)).astype(BF16)
    return hi, mid, lo


def _mod_kernel(c_ref, w_ref, b_ref, o_ref):
    s = _silu(c_ref[...]).astype(BF16)
    o_ref[0] = _dot(s, w_ref[0].astype(BF16)) + b_ref[0]


def _modulation(cc, ada_w, ada_b):
    depth = ada_w.shape[0]
    rows = cc.shape[0]
    return pl.pallas_call(
        _mod_kernel,
        grid=(depth, 3),
        in_specs=[
            pl.BlockSpec((rows, D_MODEL), lambda l, j: (0, 0)),
            pl.BlockSpec((1, D_MODEL, D_MODEL), lambda l, j: (l, 0, j)),
            pl.BlockSpec((1, 1, D_MODEL), lambda l, j: (l, 0, j)),
        ],
        out_specs=pl.BlockSpec((1, rows, D_MODEL), lambda l, j: (l, 0, j)),
        out_shape=jax.ShapeDtypeStruct((depth, rows, 3 * D_MODEL), F32),
        compiler_params=pltpu.CompilerParams(
            dimension_semantics=("parallel", "parallel"), vmem_limit_bytes=VMEM_LIMIT),
        name="adaln_mod",
    )(cc, ada_w, ada_b.reshape(depth, 1, 3 * D_MODEL))


def _in_proj_kernel(n_streams, n_batch, *refs):
    x_refs = refs[:n_streams]
    mod_ref, nw_ref, w_ref = refs[n_streams:n_streams + 3]
    q_ref, k_ref, v_ref, gate_ref, gla_ref, hg_ref = refs[n_streams + 3:]
    b = pl.program_id(0)
    t = pl.program_id(1)
    is_ctx = t < CTX_LEN // TOKEN_TILE
    if n_streams == 2:
        x = jnp.where(is_ctx, x_refs[0][0], x_refs[1][0])
    else:
        x = x_refs[0][0]
    row = jnp.where(is_ctx, n_batch, b)
    m = mod_ref[pl.ds(row, 1), :]
    sh = m[:, :D_MODEL]
    sc = m[:, D_MODEL:2 * D_MODEL]
    ms = jnp.mean(x * x, axis=-1, keepdims=True)
    y = x * lax.rsqrt(ms + NORM_EPS) * nw_ref[...]
    h = y * (1.0 + sc) + sh
    p = _dot(h.astype(BF16), w_ref[...])
    for pr in range(NA_HEADS // 2):
        lo = pr * LANES
        q_ref[0, pr] = p[:, lo:lo + LANES].astype(BF16)
        k_ref[0, pr] = p[:, NA_WIDTH + lo:NA_WIDTH + lo + LANES].astype(BF16)
        v_ref[0, pr] = p[:, 2 * NA_WIDTH + lo:2 * NA_WIDTH + lo + LANES].astype(BF16)
    gate_ref[0] = p[:, W_NA:W_NA + W_GATE]
    gla_ref[0] = p[:, W_NA + W_GATE:W_NA + W_GATE + W_GLA]
    hg_ref[0] = p[:, W_NA + W_GATE + W_GLA:]


def _in_proj(streams, mod_l, norm_w_l, w_in_p, n_batch, t_all):
    n_tiles = t_all // TOKEN_TILE
    n_ctx_tiles = CTX_LEN // TOKEN_TILE
    if len(streams) == 2:
        x_specs = [
            pl.BlockSpec((1, TOKEN_TILE, D_MODEL),
                         lambda b, t: (b, jnp.minimum(t, n_ctx_tiles - 1), 0)),
            pl.BlockSpec((1, TOKEN_TILE, D_MODEL),
                         lambda b, t: (b, jnp.maximum(t - n_ctx_tiles, 0), 0)),
        ]
    else:
        x_specs = [pl.BlockSpec((1, TOKEN_TILE, D_MODEL), lambda b, t: (b, t, 0))]
    pair_spec = pl.BlockSpec((1, NA_HEADS // 2, TOKEN_TILE, LANES), lambda b, t: (b, 0, t, 0))
    pair_shape = jax.ShapeDtypeStruct((n_batch, NA_HEADS // 2, t_all, LANES), BF16)

    def slab(width):
        return (pl.BlockSpec((1, TOKEN_TILE, width), lambda b, t: (b, t, 0)),
                jax.ShapeDtypeStruct((n_batch, t_all, width), F32))

    gate_spec, gate_shape = slab(W_GATE)
    gla_spec, gla_shape = slab(W_GLA)
    hg_spec, hg_shape = slab(W_HG)
    return pl.pallas_call(
        functools.partial(_in_proj_kernel, len(streams), n_batch),
        grid=(n_batch, n_tiles),
        in_specs=x_specs + [
            pl.BlockSpec(mod_l.shape, lambda b, t: (0, 0)),
            pl.BlockSpec((1, D_MODEL), lambda b, t: (0, 0)),
            pl.BlockSpec((D_MODEL, W_IN_PAD), lambda b, t: (0, 0)),
        ],
        out_specs=[pair_spec, pair_spec, pair_spec, gate_spec, gla_spec, hg_spec],
        out_shape=[pair_shape, pair_shape, pair_shape, gate_shape, gla_shape, hg_shape],
        compiler_params=pltpu.CompilerParams(
            dimension_semantics=("parallel", "parallel"), vmem_limit_bytes=VMEM_LIMIT),
        name="in_proj",
    )(*streams, mod_l, norm_w_l, w_in_p)


def _na_kernel(tile_off, q_ref, k_ref, v_ref, bias_ref, o_ref):
    tb = pl.program_id(2) + tile_off
    lane = lax.broadcasted_iota(jnp.int32, (1, LANES), 1)
    q = q_ref[0, 0] * jnp.asarray(HEAD_DIM ** -0.5, BF16)
    k_ctx = k_ref[0, 0, 0:CTX_LEN, :]
    v_ctx = v_ref[0, 0, 0:CTX_LEN, :]

    def head_q(hh):
        return jnp.where((lane // HEAD_DIM) == hh, q, jnp.zeros_like(q))

    def merge(o0, o1):
        return jnp.where(lane < HEAD_DIM, o0, o1)

    @pl.when(tb == 0)
    def _ctx_queries():
        outs = []
        for hh in range(2):
            s = _dot_nt(head_q(hh), k_ctx)
            m = jnp.max(s, axis=-1, keepdims=True)
            e = jnp.exp(s - m)
            l = jnp.sum(e, axis=-1, keepdims=True)
            outs.append(_dot(e.astype(BF16), v_ctx) / l)
        o_ref[0] = merge(*outs)

    @pl.when(tb > 0)
    def _latent_queries():
        g = tb - 1
        n_groups = pl.num_programs(2) - 1 + tile_off
        ws = jnp.clip(NA_Q_ROWS * g - NA_WIN_H // 2, 0, GRID_W - NA_KEY_ROWS)
        case = (g > 0).astype(jnp.int32) + (g == n_groups - 1).astype(jnp.int32)
        start = pl.multiple_of(CTX_LEN + ws * GRID_W, GRID_W)
        k_win = k_ref[0, 0, pl.ds(start, NA_KEYS), :]
        v_win = v_ref[0, 0, pl.ds(start, NA_KEYS), :]
        outs = []
        for hh in range(2):
            qh = head_q(hh)
            s_nb = _dot_nt(qh, k_win) + bias_ref[case, hh]
            s_cx = _dot_nt(qh, k_ctx)
            m = jnp.maximum(jnp.max(s_nb, axis=-1, keepdims=True),
                            jnp.max(s_cx, axis=-1, keepdims=True))
            e_nb = jnp.exp(s_nb - m)
            e_cx = jnp.exp(s_cx - m)
            l = jnp.sum(e_nb, axis=-1, keepdims=True) + jnp.sum(e_cx, axis=-1, keepdims=True)
            o = _dot(e_nb.astype(BF16), v_win) + _dot(e_cx.astype(BF16), v_ctx)
            outs.append(o / l)
        o_ref[0] = merge(*outs)


def _neighborhood_attention(q, k, v, bias, with_ctx_queries):
    n_batch, n_pairs, t_all, _ = q.shape
    n_tiles = t_all // TOKEN_TILE
    tile_off = 0 if with_ctx_queries else CTX_LEN // TOKEN_TILE
    n_steps = n_tiles - tile_off
    t_out = n_steps * TOKEN_TILE
    return pl.pallas_call(
        functools.partial(_na_kernel, tile_off),
        grid=(n_pairs, n_batch, n_steps),
        in_specs=[
            pl.BlockSpec((1, 1, TOKEN_TILE, LANES), lambda p, b, g: (b, p, g + tile_off, 0)),
            pl.BlockSpec((1, 1, t_all, LANES), lambda p, b, g: (b, p, 0, 0)),
            pl.BlockSpec((1, 1, t_all, LANES), lambda p, b, g: (b, p, 0, 0)),
            pl.BlockSpec((3, 2, TOKEN_TILE, NA_KEYS), lambda p, b, g: (0, p, 0, 0)),
        ],
        out_specs=pl.BlockSpec((1, TOKEN_TILE, LANES), lambda p, b, g: (b, g, p)),
        out_shape=jax.ShapeDtypeStruct((n_batch, t_out, NA_WIDTH), F32),
        compiler_params=pltpu.CompilerParams(
            dimension_semantics=("parallel", "parallel", "arbitrary"),
            vmem_limit_bytes=VMEM_LIMIT),
        name="neighborhood_attention",
    )(q, k, v, bias)


def _na_bias_tables(rpb):
    col = np.arange(GRID_W)
    col_start = np.clip(col - NA_WIN_W // 2, 0, GRID_W - NA_WIN_W)
    col_in = (col[None, :] >= col_start[:, None]) & (col[None, :] < col_start[:, None] + NA_WIN_W)
    col_idx = np.clip(col[None, :] - col[:, None], 1 - NA_WIN_W, NA_WIN_W - 1) + NA_WIN_W - 1
    rpb_cols = jnp.where(col_in[None, None], rpb[:, :, col_idx].astype(F32), -jnp.inf)
    lr = np.arange(NA_Q_ROWS)[:, None]
    lk = np.arange(NA_KEY_ROWS)[None, :]
    cases = [
        (lk + 0 * lr < NA_WIN_H, lk - lr + NA_WIN_H - 1),
        ((lk >= lr) & (lk < lr + NA_WIN_H), lk - lr + NA_WIN_H // 2 - 1),
        ((lk >= NA_KEY_ROWS - NA_WIN_H) & (lr >= 0), lk - lr),
    ]
    valid = np.stack([c[0] for c in cases])
    ridx = np.clip(np.stack([c[1] for c in cases]), 0, 2 * NA_WIN_H - 2)
    t = rpb_cols[:, ridx]
    t = jnp.where(valid[None, :, :, :, None, None], t, -jnp.inf)
    t = jnp.transpose(t, (1, 0, 2, 4, 3, 5))
    return t.reshape(3, NA_HEADS, TOKEN_TILE, NA_KEYS)


def _scan_constants():
    idx = np.arange(SCAN_BLOCK)
    c = idx // CHUNK
    pos = idx % CHUNK
    same = c[:, None] == c[None, :]
    out = {}
    for name, cum, ref in (
        ("f", same & (pos[None, :] <= pos[:, None]), same & (pos[None, :] <= CHUNK // 2 - 1)),
        ("b", same & (pos[None, :] >= pos[:, None]), same & (pos[None, :] >= CHUNK // 2)),
    ):
        cum_f = cum.astype(np.float32)
        ref_f = ref.astype(np.float32)
        last_f = same.astype(np.float32)
        out["mats_" + name] = np.concatenate([cum_f - ref_f, cum_f, last_f - cum_f, last_f], axis=0)
        out["tri_" + name] = np.tile(cum_f, (SCAN_HEADS, 1))
    out["chunk_cols"] = (c[:, None] == (np.arange(CHUNKS_PER_BLOCK * HEAD_DIM) // HEAD_DIM)[None, :]
                         ).astype(np.float32)
    return out


def _scan_direction(q, k, v, lf, mats, tri, chunk_cols, st_ref, reverse):
    n = SCAN_BLOCK
    hd = q.shape[1]
    dk = hd // SCAN_HEADS
    lane_k = lax.broadcasted_iota(jnp.int32, (1, hd), 1) // dk
    lane_v = lax.broadcasted_iota(jnp.int32, (1, SCAN_HEADS * HEAD_DIM), 1) // HEAD_DIM
    lane = lax.broadcasted_iota(jnp.int32, (1, LANES), 1)

    hi, mid, lo = _split3(lf)
    sums = _dot(mats, hi) + _dot(mats, mid) + _dot(mats, lo)
    x_ref = sums[0:n]
    a = jnp.exp(x_ref)
    a_inv = jnp.exp(-x_ref)
    e_in = jnp.exp(sums[n:2 * n])
    e_out = jnp.exp(sums[2 * n:3 * n])
    d_last = jnp.exp(sums[3 * n:4 * n])

    q_t = (q * a).astype(BF16)
    k_t = (k * a_inv).astype(BF16)
    q_in = (q * e_in).astype(BF16)
    k_out = (k * e_out).astype(BF16)
    v_b = v.astype(BF16)

    def stack_heads(x):
        zero = jnp.zeros_like(x)
        return jnp.concatenate([jnp.where(lane_k == h, x, zero) for h in range(SCAN_HEADS)], axis=0)

    scores = _dot_nt(stack_heads(q_t), k_t)
    scores = jnp.where(tri > 0.5, scores, 0.0)
    pv = _dot(scores.astype(BF16), v_b)
    o = jnp.zeros((n, SCAN_HEADS * HEAD_DIM), F32)
    for h in range(SCAN_HEADS):
        o = o + jnp.where(lane_v == h, pv[h * n:(h + 1) * n], 0.0)

    chunk_mask = chunk_cols > 0.5
    v_blocks = []
    for h in range(SCAN_HEADS):
        col = v[:, (h // 2) * LANES:(h // 2 + 1) * LANES]
        rolled = pltpu.roll(col, HEAD_DIM, 1)
        own_low = (lane < HEAD_DIM) == (h % 2 == 0)
        dup = jnp.where(own_low, col, rolled).astype(BF16)
        tiled = jnp.concatenate([dup] * (CHUNKS_PER_BLOCK * HEAD_DIM // LANES), axis=1)
        v_blocks.append(jnp.where(chunk_mask, tiled, jnp.zeros_like(tiled)))
    inc_all = _dot_tn(jnp.concatenate(v_blocks, axis=1), k_out)
    rows = CHUNKS_PER_BLOCK * HEAD_DIM
    inc = jnp.zeros((rows, hd), F32)
    for h in range(SCAN_HEADS):
        inc = inc + jnp.where(lane_k == h, inc_all[h * rows:(h + 1) * rows], 0.0)

    st = st_ref[...]
    states = [None] * CHUNKS_PER_BLOCK
    order = range(CHUNKS_PER_BLOCK - 1, -1, -1) if reverse else range(CHUNKS_PER_BLOCK)
    for c in order:
        states[c] = st
        st = d_last[c * CHUNK:c * CHUNK + 1, :] * st + inc[c * HEAD_DIM:(c + 1) * HEAD_DIM]
    st_ref[...] = st
    s_stack = jnp.concatenate(states, axis=0).astype(BF16)

    r = _dot_nt(stack_heads(q_in), s_stack)
    halves = []
    for h in range(SCAN_HEADS):
        rh = r[h * n:(h + 1) * n]
        acc = jnp.zeros((n, LANES), F32)
        for m in range(rows // LANES):
            acc = acc + jnp.where(chunk_mask[:, m * LANES:(m + 1) * LANES],
                                  rh[:, m * LANES:(m + 1) * LANES], 0.0)
        halves.append(acc + pltpu.roll(acc, HEAD_DIM, 1))
    cols = [jnp.where(lane < HEAD_DIM, halves[2 * p], halves[2 * p + 1]) for p in range(SCAN_HEADS // 2)]
    return o + jnp.concatenate(cols, axis=1)


def _rope(x, cos, sin_signed):
    lane = lax.broadcasted_iota(jnp.int32, (1, x.shape[1]), 1)
    first = (lane % 16) < 8
    partner = jnp.where(first, pltpu.roll(x, x.shape[1] - 8, 1), pltpu.roll(x, 8, 1))
    return x * cos + partner * sin_signed


def _scan_kernel(n_ctx_blocks,
                 gla_f_ref, gla_b_ref, hg_f_ref, hg_b_ref,
                 cos_f_ref, sin_f_ref, cos_b_ref, sin_b_ref,
                 wgk_ref, bgk_ref, lb_ref,
                 mats_f_ref, mats_b_ref, tri_f_ref, tri_b_ref, cc_ref,
                 ogf_ref, ogb_ref, ohf_ref, ohb_ref,
                 sgf_ref, sgb_ref, shf_ref, shb_ref):
    del n_ctx_blocks

    @pl.when(pl.program_id(1) == 0)
    def _reset():
        for ref in (sgf_ref, sgb_ref, shf_ref, shb_ref):
            ref[...] = jnp.zeros_like(ref)

    chunk_cols = cc_ref[...]

    def gla(in_ref, cos_ref, sin_ref, direction, mats_ref, tri_ref, st_ref, out_ref):
        blk = in_ref[0]
        v = blk[:, 0:GLA_WIDTH]
        q = blk[:, GLA_WIDTH:GLA_WIDTH + GLA_KEY]
        k = blk[:, GLA_WIDTH + GLA_KEY:GLA_WIDTH + 2 * GLA_KEY]
        gk = blk[:, GLA_WIDTH + 2 * GLA_KEY:]
        cos = cos_ref[...]
        sin = sin_ref[...]
        q = _rope(q, cos, sin) * (GLA_KEY // SCAN_HEADS) ** -0.5
        k = _rope(k, cos, sin)
        lo = direction * GLA_KEY
        z = _dot(gk.astype(BF16), wgk_ref[:, lo:lo + GLA_KEY]) + bgk_ref[:, lo:lo + GLA_KEY]
        log_sig = jnp.minimum(z, 0.0) - jnp.log(1.0 + jnp.exp(-jnp.abs(z)))
        lf = log_sig / GLA_GATE_NORM
        out_ref[0] = _scan_direction(q, k, v, lf, mats_ref[...], tri_ref[...], chunk_cols,
                                     st_ref, reverse=direction == 1)

    def hgrn(in_ref, direction, mats_ref, tri_ref, st_ref, out_ref):
        blk = in_ref[0]
        q = blk[:, 0:HGRN_WIDTH] * HEAD_DIM ** -0.5
        f_raw = blk[:, (1 + direction) * HGRN_WIDTH:(2 + direction) * HGRN_WIDTH]
        v = blk[:, 3 * HGRN_WIDTH:]
        lb = lb_ref[direction:direction + 1, :]
        f = lb + (1.0 - lb) * _sigmoid(f_raw)
        out_ref[0] = _scan_direction(q, 1.0 - f, v, jnp.log(f), mats_ref[...], tri_ref[...], chunk_cols,
                                     st_ref, reverse=direction == 1)

    gla(gla_f_ref, cos_f_ref, sin_f_ref, 0, mats_f_ref, tri_f_ref, sgf_ref, ogf_ref)
    gla(gla_b_ref, cos_b_ref, sin_b_ref, 1, mats_b_ref, tri_b_ref, sgb_ref, ogb_ref)
    hgrn(hg_f_ref, 0, mats_f_ref, tri_f_ref, shf_ref, ohf_ref)
    hgrn(hg_b_ref, 1, mats_b_ref, tri_b_ref, shb_ref, ohb_ref)


def _scans(gla_in, hg_in, cos_t, sin_t, wgk, bgk, lb, consts):
    n_batch, t_all, _ = gla_in.shape
    n_blocks = t_all // SCAN_BLOCK
    n_ctx_blocks = CTX_LEN // SCAN_BLOCK

    def fwd(i):
        return i

    def bwd(i):
        return jnp.where(i < n_ctx_blocks, n_ctx_blocks - 1 - i, n_blocks - 1 + n_ctx_blocks - i)

    def tok_spec(width, order):
        return pl.BlockSpec((1, SCAN_BLOCK, width), lambda b, i: (b, order(i), 0))

    def tab_spec(order):
        return pl.BlockSpec((SCAN_BLOCK, LANES), lambda b, i: (order(i), 0))

    def const_spec(arr):
        return pl.BlockSpec(arr.shape, lambda b, i: (0,) * arr.ndim)

    const_arrays = [wgk, bgk, lb, consts["mats_f"], consts["mats_b"], consts["tri_f"], consts["tri_b"],
                    consts["chunk_cols"]]
    out_shape = jax.ShapeDtypeStruct((n_batch, t_all, GLA_WIDTH), F32)
    return pl.pallas_call(
        functools.partial(_scan_kernel, n_ctx_blocks),
        grid=(n_batch, n_blocks),
        in_specs=[tok_spec(W_GLA, fwd), tok_spec(W_GLA, bwd), tok_spec(W_HG, fwd), tok_spec(W_HG, bwd),
                  tab_spec(fwd), tab_spec(fwd), tab_spec(bwd), tab_spec(bwd)]
                 + [const_spec(a) for a in const_arrays],
        out_specs=[tok_spec(GLA_WIDTH, fwd), tok_spec(GLA_WIDTH, bwd),
                   tok_spec(HGRN_WIDTH, fwd), tok_spec(HGRN_WIDTH, bwd)],
        out_shape=[out_shape] * 4,
        scratch_shapes=[pltpu.VMEM((HEAD_DIM, GLA_KEY), F32), pltpu.VMEM((HEAD_DIM, GLA_KEY), F32),
                        pltpu.VMEM((HEAD_DIM, HGRN_WIDTH), F32), pltpu.VMEM((HEAD_DIM, HGRN_WIDTH), F32)],
        compiler_params=pltpu.CompilerParams(
            dimension_semantics=("parallel", "arbitrary"), vmem_limit_bytes=VMEM_LIMIT),
        name="bidirectional_scans",
    )(gla_in, gla_in, hg_in, hg_in, cos_t, sin_t, cos_t, sin_t, *const_arrays)


def _rope_tables(t_lat):
    half = GLA_KEY // SCAN_HEADS // 2
    inv_freq = 1.0 / (ROPE_BASE ** (np.arange(0, half, 2, dtype=np.float32) / half))
    t = np.arange(t_lat)
    lane = np.arange(LANES) % (2 * half)
    pos = np.where(lane[None, :] < half, (t // GRID_W)[:, None], (t % GRID_W)[:, None]).astype(np.float32)
    ang = jnp.asarray(pos) * jnp.asarray(inv_freq[(lane % half) % (half // 2)])[None, :]
    first = jnp.asarray(((lane % half) < half // 2)[None, :])
    cos = jnp.cos(ang)
    sin = jnp.where(first, -jnp.sin(ang), jnp.sin(ang))
    cos = jnp.concatenate([jnp.ones((CTX_LEN, LANES), F32), cos], axis=0)
    sin = jnp.concatenate([jnp.zeros((CTX_LEN, LANES), F32), sin], axis=0)
    return cos, sin


def _out_proj_kernel(n_streams, n_batch, tile_off, final,
                     *refs):
    x_refs = refs[:n_streams]
    (ona_ref, ogf_ref, ogb_ref, ohf_ref, ohb_ref, gate_ref, mod_ref,
     gnw_ref, hnw_ref, bd_ref, w_ref) = refs[n_streams:n_streams + 11]
    rest = refs[n_streams + 11:]
    b = pl.program_id(0)
    t = pl.program_id(1) + tile_off
    is_ctx = t < CTX_LEN // TOKEN_TILE
    if n_streams == 2:
        x = jnp.where(is_ctx, x_refs[0][0], x_refs[1][0])
    else:
        x = x_refs[0][0]
    row = jnp.where(is_ctx, n_batch, b)
    gt = mod_ref[pl.ds(row, 1), :][:, 2 * D_MODEL:]
    bd = bd_ref[...]

    def head_norm(o, w):
        sq = o * o
        hi = sq.astype(BF16)
        lo = (sq - hi.astype(F32)).astype(BF16)
        ms = _dot(hi, bd) + _dot(lo, bd)
        return o * lax.rsqrt(ms + NORM_EPS) * w

    o_gla = head_norm(ogf_ref[0] + ogb_ref[0], gnw_ref[...])
    o_hg = head_norm(ohf_ref[0] + ohb_ref[0], hnw_ref[...])
    o_cat = jnp.concatenate([ona_ref[0], o_gla, o_hg], axis=1)
    act = o_cat * _silu(gate_ref[0])
    y = _dot(act.astype(BF16), w_ref[...])
    xn = x + gt * y
    if final:
        fw_ref, out_ref = rest
        ms = jnp.mean(xn * xn, axis=-1, keepdims=True)
        out_ref[0] = xn * lax.rsqrt(ms + NORM_EPS) * fw_ref[...]
    else:
        (out_ref,) = rest
        out_ref[0] = xn


def _out_proj(streams, o_na, scans, gates, mod_l, gla_nw, hg_nw, bd, w_out_b, n_batch, t_all,
              final_w=None):
    final = final_w is not None
    n_ctx_tiles = CTX_LEN // TOKEN_TILE
    tile_off = n_ctx_tiles if final else 0
    n_steps = t_all // TOKEN_TILE - tile_off
    if len(streams) == 2:
        x_specs = [
            pl.BlockSpec((1, TOKEN_TILE, D_MODEL),
                         lambda b, t: (b, jnp.minimum(t, n_ctx_tiles - 1), 0)),
            pl.BlockSpec((1, TOKEN_TILE, D_MODEL),
                         lambda b, t: (b, jnp.maximum(t - n_ctx_tiles, 0), 0)),
        ]
    else:
        x_specs = [pl.BlockSpec((1, TOKEN_TILE, D_MODEL), lambda b, t: (b, t + tile_off, 0))]

    def tok(width):
        return pl.BlockSpec((1, TOKEN_TILE, width), lambda b, t: (b, t + tile_off, 0))

    def const_spec(arr):
        return pl.BlockSpec(arr.shape, lambda b, t: (0,) * arr.ndim)

    ona_spec = pl.BlockSpec((1, TOKEN_TILE, NA_WIDTH), lambda b, t: (b, t, 0))
    consts = [mod_l, gla_nw, hg_nw, bd, w_out_b] + ([final_w] if final else [])
    return pl.pallas_call(
        functools.partial(_out_proj_kernel, len(streams), n_batch, tile_off, final),
        grid=(n_batch, n_steps),
        in_specs=x_specs + [ona_spec] + [tok(GLA_WIDTH)] * 4 + [tok(W_GATE)]
                 + [const_spec(a) for a in consts],
        out_specs=pl.BlockSpec((1, TOKEN_TILE, D_MODEL), lambda b, t: (b, t, 0)),
        out_shape=jax.ShapeDtypeStruct((n_batch, n_steps * TOKEN_TILE, D_MODEL), F32),
        compiler_params=pltpu.CompilerParams(
            dimension_semantics=("parallel", "parallel"), vmem_limit_bytes=VMEM_LIMIT),
        name="out_proj",
    )(*streams, o_na, *scans, gates, *consts)


def _permute_w_in(w):
    seg = [w[:, IN_OFFS[i]:IN_OFFS[i + 1]] for i in range(len(IN_SPLITS))]
    pad = jnp.zeros((w.shape[0], GK_PAD), w.dtype)
    cols = [seg[0], seg[1], seg[2],
            seg[3], seg[8], seg[13],
            seg[6], seg[4], seg[5], seg[7], pad,
            seg[9], seg[10], seg[11], seg[12]]
    return jnp.concatenate(cols, axis=1).astype(BF16)


def kernel(x, c, ctx, c_ctx, ada_w, ada_b, norm_w, w_in, na_rpb, gla_w_gk, gla_b_gk, gla_norm_w,
           hgrn_lower_bounds, hgrn_norm_w, w_out, final_norm_w):
    n_batch, t_lat, _ = x.shape
    depth = ada_w.shape[0]
    t_all = CTX_LEN + t_lat
    assert ctx.shape[1] == CTX_LEN and t_lat % TOKEN_TILE == 0 and t_lat // GRID_W >= NA_KEY_ROWS

    lbs = jnp.cumsum(jax.nn.softmax(hgrn_lower_bounds.astype(F32), axis=0), axis=0)
    lbs = lbs - lbs[0:1]
    mod_rows = -(-(n_batch + 1) // 8) * 8
    cc = jnp.zeros((mod_rows, D_MODEL), F32).at[:n_batch].set(c).at[n_batch].set(c_ctx)
    mod = _modulation(cc, ada_w, ada_b)

    consts = {k_: jnp.asarray(v_, BF16 if k_.startswith("mats") else F32)
              for k_, v_ in _scan_constants().items()}
    cos_t, sin_t = _rope_tables(t_lat)
    head_of = np.arange(GLA_WIDTH) // HEAD_DIM
    bd = jnp.asarray((head_of[:, None] == head_of[None, :]).astype(np.float32) / HEAD_DIM, BF16)

    streams = [ctx, x]
    out = None
    for layer in range(depth):
        last = layer == depth - 1
        w_in_p = _permute_w_in(w_in[layer])
        wgk = jnp.zeros((LANES, 2 * GLA_KEY), F32)
        wgk = wgk.at[0:GLA_GK_RANK, 0:GLA_KEY].set(gla_w_gk[layer, 0])
        wgk = wgk.at[GLA_GK_RANK:2 * GLA_GK_RANK, GLA_KEY:].set(gla_w_gk[layer, 1]).astype(BF16)
        bgk = gla_b_gk[layer].reshape(1, 2 * GLA_KEY)
        bias = _na_bias_tables(na_rpb[layer])
        gla_nw = jnp.tile(gla_norm_w[layer], SCAN_HEADS).reshape(1, GLA_WIDTH)
        hg_nw = jnp.tile(hgrn_norm_w[layer], SCAN_HEADS).reshape(1, HGRN_WIDTH)

        q, k, v, gates, gla_in, hg_in = _in_proj(
            streams, mod[layer], norm_w[layer].reshape(1, D_MODEL), w_in_p, n_batch, t_all)
        o_na = _neighborhood_attention(q, k, v, bias, with_ctx_queries=not last)
        scans = _scans(gla_in, hg_in, cos_t, sin_t, wgk, bgk, lbs[layer], consts)
        out = _out_proj(streams, o_na, scans, gates, mod[layer], gla_nw, hg_nw, bd,
                        w_out[layer].astype(BF16), n_batch, t_all,
                        final_w=final_norm_w.reshape(1, D_MODEL) if last else None)
        streams = [out]
    return out
```

```python
import functools

import numpy as np
import jax
import jax.numpy as jnp
from jax import lax
from jax.experimental import pallas as pl
from jax.experimental.pallas import tpu as pltpu

F32 = jnp.float32
BF16 = jnp.bfloat16

D_MODEL = 1024
CTX_LEN = 256
GRID_W = 64
NORM_EPS = 1e-6
HEAD_DIM = 64
NA_WIDTH = 512
NA_HEADS = 8
NA_WIN_H = 8
NA_WIN_W = 16
GLA_WIDTH = 256
GLA_KEY = 128
GLA_GK_RANK = 16
GLA_GATE_NORM = 16.0
HGRN_WIDTH = 256
SCAN_HEADS = 4
CHUNK = 16
ROPE_BASE = 10000.0

LANES = 128
TOKEN_TILE = 256
SCAN_BLOCK = 128
SCAN_SUB = 2
CHUNKS_PER_BLOCK = SCAN_BLOCK // CHUNK
NA_Q_ROWS = TOKEN_TILE // GRID_W
NA_KEY_ROWS = NA_Q_ROWS + NA_WIN_H - 1
NA_KEYS = NA_KEY_ROWS * GRID_W
VMEM_LIMIT = 56 * 1024 * 1024

IN_SPLITS = (512, 512, 512, 512, 128, 128, 256, 32, 256, 256, 256, 256, 256, 256)
IN_OFFS = tuple(int(v) for v in np.cumsum((0,) + IN_SPLITS))
GK_PAD = LANES - 2 * GLA_GK_RANK
W_NA = 3 * NA_WIDTH
W_GATE = NA_WIDTH + GLA_WIDTH + HGRN_WIDTH
W_GLA = GLA_WIDTH + 2 * GLA_KEY + LANES
W_HG = 4 * HGRN_WIDTH
W_IN_PAD = W_NA + W_GATE + W_GLA + W_HG


def _dot(a, b):
    return jnp.dot(a, b, preferred_element_type=F32)


def _dot_nt(a, b):
    return lax.dot_general(a, b, (((1,), (1,)), ((), ())), preferred_element_type=F32)


def _dot_tn(a, b):
    return lax.dot_general(a, b, (((0,), (0,)), ((), ())), preferred_element_type=F32)


def _sigmoid(x):
    return 1.0 / (1.0 + jnp.exp(-x))


def _silu(x):
    return x * _sigmoid(x)


def _split3(x):
    hi = x.astype(BF16)
    r1 = x - hi.astype(F32)
    mid = r1.astype(BF16)
    lo = (r1 - mid.astype(F32)).astype(BF16)
    return hi, mid, lo


def _mod_kernel(c_ref, w_ref, b_ref, o_ref):
    s = _silu(c_ref[...]).astype(BF16)
    o_ref[0] = _dot(s, w_ref[0].astype(BF16)) + b_ref[0]


def _modulation(cc, ada_w, ada_b):
    depth = ada_w.shape[0]
    rows = cc.shape[0]
    return pl.pallas_call(
        _mod_kernel,
        grid=(depth, 3),
        in_specs=[
            pl.BlockSpec((rows, D_MODEL), lambda l, j: (0, 0)),
            pl.BlockSpec((1, D_MODEL, D_MODEL), lambda l, j: (l, 0, j)),
            pl.BlockSpec((1, 1, D_MODEL), lambda l, j: (l, 0, j)),
        ],
        out_specs=pl.BlockSpec((1, rows, D_MODEL), lambda l, j: (l, 0, j)),
        out_shape=jax.ShapeDtypeStruct((depth, rows, 3 * D_MODEL), F32),
        compiler_params=pltpu.CompilerParams(
            dimension_semantics=("parallel", "parallel"), vmem_limit_bytes=VMEM_LIMIT),
        name="adaln_mod",
    )(cc, ada_w, ada_b.reshape(depth, 1, 3 * D_MODEL))


def _in_proj_kernel(n_streams, n_batch, *refs):
    x_refs = refs[:n_streams]
    mod_ref, nw_ref, w_ref = refs[n_streams:n_streams + 3]
    q_ref, k_ref, v_ref, gate_ref, gla_ref, hg_ref = refs[n_streams + 3:]
    b = pl.program_id(0)
    t = pl.program_id(1)
    is_ctx = t < CTX_LEN // TOKEN_TILE
    if n_streams == 2:
        x = jnp.where(is_ctx, x_refs[0][0], x_refs[1][0])
    else:
        x = x_refs[0][0]
    row = jnp.where(is_ctx, n_batch, b)
    m = mod_ref[pl.ds(row, 1), :]
    sh = m[:, :D_MODEL]
    sc = m[:, D_MODEL:2 * D_MODEL]
    ms = jnp.mean(x * x, axis=-1, keepdims=True)
    y = x * lax.rsqrt(ms + NORM_EPS) * nw_ref[...]
    h = y * (1.0 + sc) + sh
    p = _dot(h.astype(BF16), w_ref[...])
    for pr in range(NA_HEADS // 2):
        lo = pr * LANES
        q_ref[0, pr] = p[:, lo:lo + LANES].astype(BF16)
        k_ref[0, pr] = p[:, NA_WIDTH + lo:NA_WIDTH + lo + LANES].astype(BF16)
        v_ref[0, pr] = p[:, 2 * NA_WIDTH + lo:2 * NA_WIDTH + lo + LANES].astype(BF16)
    gate_ref[0] = p[:, W_NA:W_NA + W_GATE]
    gla_ref[0] = p[:, W_NA + W_GATE:W_NA + W_GATE + W_GLA]
    hg_ref[0] = p[:, W_NA + W_GATE + W_GLA:]


def _in_proj(streams, mod_l, norm_w_l, w_in_p, n_batch, t_all):
    n_tiles = t_all // TOKEN_TILE
    n_ctx_tiles = CTX_LEN // TOKEN_TILE
    if len(streams) == 2:
        x_specs = [
            pl.BlockSpec((1, TOKEN_TILE, D_MODEL),
                         lambda b, t: (b, jnp.minimum(t, n_ctx_tiles - 1), 0)),
            pl.BlockSpec((1, TOKEN_TILE, D_MODEL),
                         lambda b, t: (b, jnp.maximum(t - n_ctx_tiles, 0), 0)),
        ]
    else:
        x_specs = [pl.BlockSpec((1, TOKEN_TILE, D_MODEL), lambda b, t: (b, t, 0))]
    pair_spec = pl.BlockSpec((1, NA_HEADS // 2, TOKEN_TILE, LANES), lambda b, t: (b, 0, t, 0))
    pair_shape = jax.ShapeDtypeStruct((n_batch, NA_HEADS // 2, t_all, LANES), BF16)

    def slab(width):
        return (pl.BlockSpec((1, TOKEN_TILE, width), lambda b, t: (b, t, 0)),
                jax.ShapeDtypeStruct((n_batch, t_all, width), F32))

    gate_spec, gate_shape = slab(W_GATE)
    gla_spec, gla_shape = slab(W_GLA)
    hg_spec, hg_shape = slab(W_HG)
    return pl.pallas_call(
        functools.partial(_in_proj_kernel, len(streams), n_batch),
        grid=(n_batch, n_tiles),
        in_specs=x_specs + [
            pl.BlockSpec(mod_l.shape, lambda b, t: (0, 0)),
            pl.BlockSpec((1, D_MODEL), lambda b, t: (0, 0)),
            pl.BlockSpec((D_MODEL, W_IN_PAD), lambda b, t: (0, 0)),
        ],
        out_specs=[pair_spec, pair_spec, pair_spec, gate_spec, gla_spec, hg_spec],
        out_shape=[pair_shape, pair_shape, pair_shape, gate_shape, gla_shape, hg_shape],
        compiler_params=pltpu.CompilerParams(
            dimension_semantics=("parallel", "parallel"), vmem_limit_bytes=VMEM_LIMIT),
        name="in_proj",
    )(*streams, mod_l, norm_w_l, w_in_p)


def _na_kernel(tile_off, q_ref, k_ref, v_ref, bias_ref, o_ref):
    tb = pl.program_id(2) + tile_off
    lane = lax.broadcasted_iota(jnp.int32, (1, LANES), 1)
    q = q_ref[0, 0] * jnp.asarray(HEAD_DIM ** -0.5, BF16)
    k_ctx = k_ref[0, 0, 0:CTX_LEN, :]
    v_ctx = v_ref[0, 0, 0:CTX_LEN, :]

    def head_q(hh):
        return jnp.where((lane // HEAD_DIM) == hh, q, jnp.zeros_like(q))

    def merge(o0, o1):
        return jnp.where(lane < HEAD_DIM, o0, o1)

    @pl.when(tb == 0)
    def _ctx_queries():
        outs = []
        for hh in range(2):
            s = _dot_nt(head_q(hh), k_ctx)
            m = jnp.max(s, axis=-1, keepdims=True)
            e = jnp.exp(s - m)
            l = jnp.sum(e, axis=-1, keepdims=True)
            outs.append(_dot(e.astype(BF16), v_ctx) / l)
        o_ref[0] = merge(*outs)

    @pl.when(tb > 0)
    def _latent_queries():
        g = tb - 1
        n_groups = pl.num_programs(2) - 1 + tile_off
        ws = jnp.clip(NA_Q_ROWS * g - NA_WIN_H // 2, 0, GRID_W - NA_KEY_ROWS)
        case = (g > 0).astype(jnp.int32) + (g == n_groups - 1).astype(jnp.int32)
        start = pl.multiple_of(CTX_LEN + ws * GRID_W, GRID_W)
        k_win = k_ref[0, 0, pl.ds(start, NA_KEYS), :]
        v_win = v_ref[0, 0, pl.ds(start, NA_KEYS), :]
        outs = []
        for hh in range(2):
            qh = head_q(hh)
            s_nb = _dot_nt(qh, k_win) + bias_ref[case, hh]
            s_cx = _dot_nt(qh, k_ctx)
            m = jnp.maximum(jnp.max(s_nb, axis=-1, keepdims=True),
                            jnp.max(s_cx, axis=-1, keepdims=True))
            e_nb = jnp.exp(s_nb - m)
            e_cx = jnp.exp(s_cx - m)
            l = jnp.sum(e_nb, axis=-1, keepdims=True) + jnp.sum(e_cx, axis=-1, keepdims=True)
            o = _dot(e_nb.astype(BF16), v_win) + _dot(e_cx.astype(BF16), v_ctx)
            outs.append(o / l)
        o_ref[0] = merge(*outs)


def _neighborhood_attention(q, k, v, bias, with_ctx_queries):
    n_batch, n_pairs, t_all, _ = q.shape
    n_tiles = t_all // TOKEN_TILE
    tile_off = 0 if with_ctx_queries else CTX_LEN // TOKEN_TILE
    n_steps = n_tiles - tile_off
    t_out = n_steps * TOKEN_TILE
    return pl.pallas_call(
        functools.partial(_na_kernel, tile_off),
        grid=(n_pairs, n_batch, n_steps),
        in_specs=[
            pl.BlockSpec((1, 1, TOKEN_TILE, LANES), lambda p, b, g: (b, p, g + tile_off, 0)),
            pl.BlockSpec((1, 1, t_all, LANES), lambda p, b, g: (b, p, 0, 0)),
            pl.BlockSpec((1, 1, t_all, LANES), lambda p, b, g: (b, p, 0, 0)),
            pl.BlockSpec((3, 2, TOKEN_TILE, NA_KEYS), lambda p, b, g: (0, p, 0, 0)),
        ],
        out_specs=pl.BlockSpec((1, TOKEN_TILE, LANES), lambda p, b, g: (b, g, p)),
        out_shape=jax.ShapeDtypeStruct((n_batch, t_out, NA_WIDTH), F32),
        compiler_params=pltpu.CompilerParams(
            dimension_semantics=("parallel", "parallel", "arbitrary"),
            vmem_limit_bytes=VMEM_LIMIT),
        name="neighborhood_attention",
    )(q, k, v, bias)


def _na_bias_tables(rpb):
    col = np.arange(GRID_W)
    col_start = np.clip(col - NA_WIN_W // 2, 0, GRID_W - NA_WIN_W)
    col_in = (col[None, :] >= col_start[:, None]) & (col[None, :] < col_start[:, None] + NA_WIN_W)
    col_idx = np.clip(col[None, :] - col[:, None], 1 - NA_WIN_W, NA_WIN_W - 1) + NA_WIN_W - 1
    rpb_cols = jnp.where(col_in[None, None], rpb[:, :, col_idx].astype(F32), -jnp.inf)
    lr = np.arange(NA_Q_ROWS)[:, None]
    lk = np.arange(NA_KEY_ROWS)[None, :]
    cases = [
        (lk + 0 * lr < NA_WIN_H, lk - lr + NA_WIN_H - 1),
        ((lk >= lr) & (lk < lr + NA_WIN_H), lk - lr + NA_WIN_H // 2 - 1),
        ((lk >= NA_KEY_ROWS - NA_WIN_H) & (lr >= 0), lk - lr),
    ]
    valid = np.stack([c[0] for c in cases])
    ridx = np.clip(np.stack([c[1] for c in cases]), 0, 2 * NA_WIN_H - 2)
    t = rpb_cols[:, ridx]
    t = jnp.where(valid[None, :, :, :, None, None], t, -jnp.inf)
    t = jnp.transpose(t, (1, 0, 2, 4, 3, 5))
    return t.reshape(3, NA_HEADS, TOKEN_TILE, NA_KEYS)


def _scan_constants():
    idx = np.arange(SCAN_BLOCK)
    c = idx // CHUNK
    pos = idx % CHUNK
    same = c[:, None] == c[None, :]
    return {"cum_f": (same & (pos[None, :] <= pos[:, None])).astype(np.float32),
            "cum_b": (same & (pos[None, :] >= pos[:, None])).astype(np.float32)}


def _block_matrix(n_rows, n_cols, block_shape, dtype, block_at):
    zero = jnp.zeros(block_shape, dtype)
    rows = []
    for r in range(n_rows):
        blocks = [block_at(r, c) for c in range(n_cols)]
        rows.append(jnp.concatenate([zero if b is None else b for b in blocks], axis=1))
    return jnp.concatenate(rows, axis=0)


def _run_interleaved(stagewise):
    results = [None] * len(stagewise)
    live = list(range(len(stagewise)))
    while live:
        for i in list(live):
            try:
                next(stagewise[i])
            except StopIteration as done:
                results[i] = done.value
                live.remove(i)
    return results


def _scan_direction(load_inputs, cum, st_ref, reverse):
    q, k, v, lf = load_inputs()
    n = SCAN_BLOCK
    hd = q.shape[1]
    dk = hd // SCAN_HEADS
    pack = 2 * LANES // hd
    low_half = lax.broadcasted_iota(jnp.int32, (1, LANES), 1) < HEAD_DIM
    lane_k = lax.broadcasted_iota(jnp.int32, (1, hd), 1) // dk
    lane_v = lax.broadcasted_iota(jnp.int32, (1, SCAN_HEADS * HEAD_DIM), 1) // HEAD_DIM

    def lane_mask(cond):
        return jnp.where(cond, 1.0, 0.0).astype(BF16)

    half_masks = {True: lane_mask(low_half), False: lane_mask(jnp.logical_not(low_half))}
    k_masks = [lane_mask(lane_k == h) for h in range(SCAN_HEADS)]
    v_masks = [lane_mask(lane_v == h) for h in range(SCAN_HEADS)]

    def keep_half(x, low):
        return x * half_masks[low]

    hi, mid, lo = _split3(lf)
    g = _dot(cum, hi) + _dot(cum, mid) + _dot(cum, lo)
    yield
    ref_pos, last_pos = (CHUNK // 2, 0) if reverse else (CHUNK // 2 - 1, CHUNK - 1)
    to_ref, to_last, decay = [], [], []
    for c in range(CHUNKS_PER_BLOCK):
        gc = g[c * CHUNK:(c + 1) * CHUNK]
        g_last = gc[last_pos:last_pos + 1]
        to_ref.append(gc - gc[ref_pos:ref_pos + 1])
        to_last.append(g_last - gc)
        decay.append(jnp.exp(g_last))
    x_ref = jnp.concatenate(to_ref, axis=0)
    a = jnp.exp(x_ref)
    a_inv = jnp.exp(-x_ref)
    e_in = jnp.exp(g)
    e_out = jnp.exp(jnp.concatenate(to_last, axis=0))

    q_t = (q * a).astype(BF16)
    k_t_tr = jnp.transpose(k * a_inv).astype(BF16)
    q_in = (q * e_in).astype(BF16)
    k_out = (k * e_out).astype(BF16)
    v_b = v.astype(BF16)

    k_blk = _block_matrix(SCAN_HEADS, SCAN_HEADS, (dk, n), BF16,
                          lambda r, c: k_t_tr[r * dk:(r + 1) * dk] if r == c else None)
    scores = _dot(q_t, k_blk)

    v_tr = jnp.transpose(v).astype(BF16)
    token_group = lax.broadcasted_iota(jnp.int32, (1, n), 1) // (CHUNK * pack)
    group_masks = [lane_mask(token_group == m) for m in range(CHUNKS_PER_BLOCK // pack)]
    v_blk = jnp.concatenate(
        [jnp.concatenate([v_tr[h * HEAD_DIM:(h + 1) * HEAD_DIM] * gm for h in range(SCAN_HEADS)], axis=1)
         for gm in group_masks], axis=0)
    k_rows = []
    for h in range(SCAN_HEADS):
        k_h = k_out * k_masks[h]
        if pack > 1:
            k_h = _block_matrix(CHUNKS_PER_BLOCK, pack, (CHUNK, hd), BF16,
                                lambda c, par, k_h=k_h: k_h[c * CHUNK:(c + 1) * CHUNK] if c % pack == par else None)
        k_rows.append(k_h)
    k_stack = jnp.concatenate(k_rows, axis=0)
    inc = _dot(v_blk, k_stack)
    yield

    tri = jnp.concatenate([cum.astype(F32)] * SCAN_HEADS, axis=1) > 0.5
    p = jnp.where(tri, scores, 0.0).astype(BF16)
    v_stack = jnp.concatenate([v_b * v_masks[h] for h in range(SCAN_HEADS)], axis=0)
    o_intra = _dot(p, v_stack)
    yield

    st = st_ref[...]
    states = [None] * CHUNKS_PER_BLOCK
    order = range(CHUNKS_PER_BLOCK - 1, -1, -1) if reverse else range(CHUNKS_PER_BLOCK)
    for c in order:
        states[c] = st
        m, par = divmod(c, pack)
        st = decay[c] * st + inc[m * HEAD_DIM:(m + 1) * HEAD_DIM, par * hd:(par + 1) * hd]
    st_ref[...] = st

    q_blk = _block_matrix(CHUNKS_PER_BLOCK, CHUNKS_PER_BLOCK, (CHUNK, hd), BF16,
                          lambda r, c: q_in[r * CHUNK:(r + 1) * CHUNK] if r == c else None)
    s_parts = []
    for s in states:
        s_tr = jnp.transpose(jnp.concatenate([s, s], axis=0)).astype(BF16)

        def s_block(h, j, s_tr=s_tr):
            return keep_half(s_tr[h * dk:(h + 1) * dk], h % 2 == 0) if h // 2 == j else None

        s_parts.append(_block_matrix(SCAN_HEADS, SCAN_HEADS // 2, (dk, LANES), BF16, s_block))
    return o_intra + _dot(q_blk, jnp.concatenate(s_parts, axis=0))


def _rope(x, cos, sin_signed):
    lane = lax.broadcasted_iota(jnp.int32, (1, x.shape[1]), 1)
    first = (lane % 16) < 8
    partner = jnp.where(first, pltpu.roll(x, x.shape[1] - 8, 1), pltpu.roll(x, 8, 1))
    return x * cos + partner * sin_signed


def _scan_kernel(n_ctx_blocks,
                 gla_f_ref, gla_b_ref, hg_f_ref, hg_b_ref,
                 cos_f_ref, sin_f_ref, cos_b_ref, sin_b_ref,
                 wgk_ref, bgk_ref, lb_ref, cum_f_ref, cum_b_ref,
                 ogf_ref, ogb_ref, ohf_ref, ohb_ref,
                 sgf_ref, sgb_ref, shf_ref, shb_ref):
    del n_ctx_blocks

    @pl.when(pl.program_id(1) == 0)
    def _reset():
        for ref in (sgf_ref, sgb_ref, shf_ref, shb_ref):
            ref[...] = jnp.zeros_like(ref)

    def gla_inputs(in_ref, cos_ref, sin_ref, direction, rows):
        blk = in_ref[0, rows, :]
        v = blk[:, 0:GLA_WIDTH]
        q = blk[:, GLA_WIDTH:GLA_WIDTH + GLA_KEY]
        k = blk[:, GLA_WIDTH + GLA_KEY:GLA_WIDTH + 2 * GLA_KEY]
        gk = blk[:, GLA_WIDTH + 2 * GLA_KEY:]
        cos = cos_ref[rows, :]
        sin = sin_ref[rows, :]
        q = _rope(q, cos, sin) * (GLA_KEY // SCAN_HEADS) ** -0.5
        k = _rope(k, cos, sin)
        lo = direction * GLA_KEY
        z = _dot(gk.astype(BF16), wgk_ref[:, lo:lo + GLA_KEY]) + bgk_ref[:, lo:lo + GLA_KEY]
        log_sig = jnp.minimum(z, 0.0) - jnp.log(1.0 + jnp.exp(-jnp.abs(z)))
        return q, k, v, log_sig / GLA_GATE_NORM

    def hgrn_inputs(in_ref, direction, rows):
        blk = in_ref[0, rows, :]
        q = blk[:, 0:HGRN_WIDTH] * HEAD_DIM ** -0.5
        f_raw = blk[:, (1 + direction) * HGRN_WIDTH:(2 + direction) * HGRN_WIDTH]
        v = blk[:, 3 * HGRN_WIDTH:]
        lb = lb_ref[direction:direction + 1, :]
        f = lb + (1.0 - lb) * _sigmoid(f_raw)
        return q, 1.0 - f, v, jnp.log(f)

    jobs, dests = [], []
    for s in range(SCAN_SUB):
        rows_f = pl.ds(s * SCAN_BLOCK, SCAN_BLOCK)
        rows_b = pl.ds((SCAN_SUB - 1 - s) * SCAN_BLOCK, SCAN_BLOCK)
        jobs += [
            _scan_direction(functools.partial(gla_inputs, gla_f_ref, cos_f_ref, sin_f_ref, 0, rows_f),
                            cum_f_ref[...],sgf_ref, reverse=False),
            _scan_direction(functools.partial(hgrn_inputs, hg_f_ref, 0, rows_f),
                            cum_f_ref[...],shf_ref, reverse=False),
            _scan_direction(functools.partial(gla_inputs, gla_b_ref, cos_b_ref, sin_b_ref, 1, rows_b),
                            cum_b_ref[...],sgb_ref, reverse=True),
            _scan_direction(functools.partial(hgrn_inputs, hg_b_ref, 1, rows_b),
                            cum_b_ref[...],shb_ref, reverse=True),
        ]
        dests += [(ogf_ref, rows_f), (ohf_ref, rows_f), (ogb_ref, rows_b), (ohb_ref, rows_b)]
    for (ref, rows), val in zip(dests, _run_interleaved(jobs)):
        ref[0, rows, :] = val


def _scans(gla_in, hg_in, cos_t, sin_t, wgk, bgk, lb, consts):
    n_batch, t_all, _ = gla_in.shape
    step_rows = SCAN_SUB * SCAN_BLOCK
    assert CTX_LEN % step_rows == 0 and t_all % step_rows == 0
    n_blocks = t_all // step_rows
    n_ctx_blocks = CTX_LEN // step_rows

    def fwd(i):
        return i

    def bwd(i):
        return jnp.where(i < n_ctx_blocks, n_ctx_blocks - 1 - i, n_blocks - 1 + n_ctx_blocks - i)

    def tok_spec(width, order):
        return pl.BlockSpec((1, step_rows, width), lambda b, i: (b, order(i), 0))

    def tab_spec(order):
        return pl.BlockSpec((step_rows, LANES), lambda b, i: (order(i), 0))

    def const_spec(arr):
        return pl.BlockSpec(arr.shape, lambda b, i: (0,) * arr.ndim)

    const_arrays = [wgk, bgk, lb, consts["cum_f"], consts["cum_b"]]
    out_shape = jax.ShapeDtypeStruct((n_batch, t_all, GLA_WIDTH), F32)
    return pl.pallas_call(
        functools.partial(_scan_kernel, n_ctx_blocks),
        grid=(n_batch, n_blocks),
        in_specs=[tok_spec(W_GLA, fwd), tok_spec(W_GLA, bwd), tok_spec(W_HG, fwd), tok_spec(W_HG, bwd),
                  tab_spec(fwd), tab_spec(fwd), tab_spec(bwd), tab_spec(bwd)]
                 + [const_spec(a) for a in const_arrays],
        out_specs=[tok_spec(GLA_WIDTH, fwd), tok_spec(GLA_WIDTH, bwd),
                   tok_spec(HGRN_WIDTH, fwd), tok_spec(HGRN_WIDTH, bwd)],
        out_shape=[out_shape] * 4,
        scratch_shapes=[pltpu.VMEM((HEAD_DIM, GLA_KEY), F32), pltpu.VMEM((HEAD_DIM, GLA_KEY), F32),
                        pltpu.VMEM((HEAD_DIM, HGRN_WIDTH), F32), pltpu.VMEM((HEAD_DIM, HGRN_WIDTH), F32)],
        compiler_params=pltpu.CompilerParams(
            dimension_semantics=("parallel", "arbitrary"), vmem_limit_bytes=VMEM_LIMIT),
        name="bidirectional_scans",
    )(gla_in, gla_in, hg_in, hg_in, cos_t, sin_t, cos_t, sin_t, *const_arrays)


def _rope_tables(t_lat):
    half = GLA_KEY // SCAN_HEADS // 2
    inv_freq = 1.0 / (ROPE_BASE ** (np.arange(0, half, 2, dtype=np.float32) / half))
    t = np.arange(t_lat)
    lane = np.arange(LANES) % (2 * half)
    pos = np.where(lane[None, :] < half, (t // GRID_W)[:, None], (t % GRID_W)[:, None]).astype(np.float32)
    ang = jnp.asarray(pos) * jnp.asarray(inv_freq[(lane % half) % (half // 2)])[None, :]
    first = jnp.asarray(((lane % half) < half // 2)[None, :])
    cos = jnp.cos(ang)
    sin = jnp.where(first, -jnp.sin(ang), jnp.sin(ang))
    cos = jnp.concatenate([jnp.ones((CTX_LEN, LANES), F32), cos], axis=0)
    sin = jnp.concatenate([jnp.zeros((CTX_LEN, LANES), F32), sin], axis=0)
    return cos, sin


def _out_proj_kernel(n_streams, n_batch, tile_off, final,
                     *refs):
    x_refs = refs[:n_streams]
    (ona_ref, ogf_ref, ogb_ref, ohf_ref, ohb_ref, gate_ref, mod_ref,
     gnw_ref, hnw_ref, bd_ref, w_ref) = refs[n_streams:n_streams + 11]
    rest = refs[n_streams + 11:]
    b = pl.program_id(0)
    t = pl.program_id(1) + tile_off
    is_ctx = t < CTX_LEN // TOKEN_TILE
    if n_streams == 2:
        x = jnp.where(is_ctx, x_refs[0][0], x_refs[1][0])
    else:
        x = x_refs[0][0]
    row = jnp.where(is_ctx, n_batch, b)
    gt = mod_ref[pl.ds(row, 1), :][:, 2 * D_MODEL:]
    bd = bd_ref[...]

    def head_norm(o, w):
        sq = o * o
        hi = sq.astype(BF16)
        lo = (sq - hi.astype(F32)).astype(BF16)
        ms = _dot(hi, bd) + _dot(lo, bd)
        return o * lax.rsqrt(ms + NORM_EPS) * w

    o_gla = head_norm(ogf_ref[0] + ogb_ref[0], gnw_ref[...])
    o_hg = head_norm(ohf_ref[0] + ohb_ref[0], hnw_ref[...])
    o_cat = jnp.concatenate([ona_ref[0], o_gla, o_hg], axis=1)
    act = o_cat * _silu(gate_ref[0])
    y = _dot(act.astype(BF16), w_ref[...])
    xn = x + gt * y
    if final:
        fw_ref, out_ref = rest
        ms = jnp.mean(xn * xn, axis=-1, keepdims=True)
        out_ref[0] = xn * lax.rsqrt(ms + NORM_EPS) * fw_ref[...]
    else:
        (out_ref,) = rest
        out_ref[0] = xn


def _out_proj(streams, o_na, scans, gates, mod_l, gla_nw, hg_nw, bd, w_out_b, n_batch, t_all,
              final_w=None):
    final = final_w is not None
    n_ctx_tiles = CTX_LEN // TOKEN_TILE
    tile_off = n_ctx_tiles if final else 0
    n_steps = t_all // TOKEN_TILE - tile_off
    if len(streams) == 2:
        x_specs = [
            pl.BlockSpec((1, TOKEN_TILE, D_MODEL),
                         lambda b, t: (b, jnp.minimum(t, n_ctx_tiles - 1), 0)),
            pl.BlockSpec((1, TOKEN_TILE, D_MODEL),
                         lambda b, t: (b, jnp.maximum(t - n_ctx_tiles, 0), 0)),
        ]
    else:
        x_specs = [pl.BlockSpec((1, TOKEN_TILE, D_MODEL), lambda b, t: (b, t + tile_off, 0))]

    def tok(width):
        return pl.BlockSpec((1, TOKEN_TILE, width), lambda b, t: (b, t + tile_off, 0))

    def const_spec(arr):
        return pl.BlockSpec(arr.shape, lambda b, t: (0,) * arr.ndim)

    ona_spec = pl.BlockSpec((1, TOKEN_TILE, NA_WIDTH), lambda b, t: (b, t, 0))
    consts = [mod_l, gla_nw, hg_nw, bd, w_out_b] + ([final_w] if final else [])
    return pl.pallas_call(
        functools.partial(_out_proj_kernel, len(streams), n_batch, tile_off, final),
        grid=(n_batch, n_steps),
        in_specs=x_specs + [ona_spec] + [tok(GLA_WIDTH)] * 4 + [tok(W_GATE)]
                 + [const_spec(a) for a in consts],
        out_specs=pl.BlockSpec((1, TOKEN_TILE, D_MODEL), lambda b, t: (b, t, 0)),
        out_shape=jax.ShapeDtypeStruct((n_batch, n_steps * TOKEN_TILE, D_MODEL), F32),
        compiler_params=pltpu.CompilerParams(
            dimension_semantics=("parallel", "parallel"), vmem_limit_bytes=VMEM_LIMIT),
        name="out_proj",
    )(*streams, o_na, *scans, gates, *consts)


def _permute_w_in(w):
    seg = [w[:, IN_OFFS[i]:IN_OFFS[i + 1]] for i in range(len(IN_SPLITS))]
    pad = jnp.zeros((w.shape[0], GK_PAD), w.dtype)
    cols = [seg[0], seg[1], seg[2],
            seg[3], seg[8], seg[13],
            seg[6], seg[4], seg[5], seg[7], pad,
            seg[9], seg[10], seg[11], seg[12]]
    return jnp.concatenate(cols, axis=1).astype(BF16)


def kernel(x, c, ctx, c_ctx, ada_w, ada_b, norm_w, w_in, na_rpb, gla_w_gk, gla_b_gk, gla_norm_w,
           hgrn_lower_bounds, hgrn_norm_w, w_out, final_norm_w):
    n_batch, t_lat, _ = x.shape
    depth = ada_w.shape[0]
    t_all = CTX_LEN + t_lat
    assert ctx.shape[1] == CTX_LEN and t_lat % TOKEN_TILE == 0 and t_lat // GRID_W >= NA_KEY_ROWS

    lbs = jnp.cumsum(jax.nn.softmax(hgrn_lower_bounds.astype(F32), axis=0), axis=0)
    lbs = lbs - lbs[0:1]
    mod_rows = -(-(n_batch + 1) // 8) * 8
    cc = jnp.zeros((mod_rows, D_MODEL), F32).at[:n_batch].set(c).at[n_batch].set(c_ctx)
    mod = _modulation(cc, ada_w, ada_b)

    consts = {k_: jnp.asarray(v_, BF16) for k_, v_ in _scan_constants().items()}
    cos_t, sin_t = _rope_tables(t_lat)
    head_of = np.arange(GLA_WIDTH) // HEAD_DIM
    bd = jnp.asarray((head_of[:, None] == head_of[None, :]).astype(np.float32) / HEAD_DIM, BF16)

    streams = [ctx, x]
    out = None
    for layer in range(depth):
        last = layer == depth - 1
        w_in_p = _permute_w_in(w_in[layer])
        wgk = jnp.zeros((LANES, 2 * GLA_KEY), F32)
        wgk = wgk.at[0:GLA_GK_RANK, 0:GLA_KEY].set(gla_w_gk[layer, 0])
        wgk = wgk.at[GLA_GK_RANK:2 * GLA_GK_RANK, GLA_KEY:].set(gla_w_gk[layer, 1]).astype(BF16)
        bgk = gla_b_gk[layer].reshape(1, 2 * GLA_KEY)
        bias = _na_bias_tables(na_rpb[layer])
        gla_nw = jnp.tile(gla_norm_w[layer], SCAN_HEADS).reshape(1, GLA_WIDTH)
        hg_nw = jnp.tile(hgrn_norm_w[layer], SCAN_HEADS).reshape(1, HGRN_WIDTH)

        q, k, v, gates, gla_in, hg_in = _in_proj(
            streams, mod[layer], norm_w[layer].reshape(1, D_MODEL), w_in_p, n_batch, t_all)
        o_na = _neighborhood_attention(q, k, v, bias, with_ctx_queries=not last)
        scans = _scans(gla_in, hg_in, cos_t, sin_t, wgk, bgk, lbs[layer], consts)
        out = _out_proj(streams, o_na, scans, gates, mod[layer], gla_nw, hg_nw, bd,
                        w_out[layer].astype(BF16), n_batch, t_all,
                        final_w=final_norm_w.reshape(1, D_MODEL) if last else None)
        streams = [out]
    return out
```

```python
import functools

import numpy as np
import jax
import jax.numpy as jnp
from jax import lax
from jax.experimental import pallas as pl
from jax.experimental.pallas import tpu as pltpu

F32 = jnp.float32
BF16 = jnp.bfloat16

D_MODEL = 1024
CTX_LEN = 256
GRID_W = 64
NORM_EPS = 1e-6
HEAD_DIM = 64
NA_WIDTH = 512
NA_HEADS = 8
NA_WIN_H = 8
NA_WIN_W = 16
GLA_WIDTH = 256
GLA_KEY = 128
GLA_GK_RANK = 16
GLA_GATE_NORM = 16.0
HGRN_WIDTH = 256
SCAN_HEADS = 4
CHUNK = 16
ROPE_BASE = 10000.0

LANES = 128
TOKEN_TILE = 256
SCAN_BLOCK = 128
SCAN_SUB = 2
CHUNKS_PER_BLOCK = SCAN_BLOCK // CHUNK
NA_Q_ROWS = TOKEN_TILE // GRID_W
NA_KEY_ROWS = NA_Q_ROWS + NA_WIN_H - 1
NA_KEYS = NA_KEY_ROWS * GRID_W
VMEM_LIMIT = 56 * 1024 * 1024

IN_SPLITS = (512, 512, 512, 512, 128, 128, 256, 32, 256, 256, 256, 256, 256, 256)
IN_OFFS = tuple(int(v) for v in np.cumsum((0,) + IN_SPLITS))
GK_PAD = LANES - 2 * GLA_GK_RANK
W_NA = 3 * NA_WIDTH
W_GATE = NA_WIDTH + GLA_WIDTH + HGRN_WIDTH
W_GLA = GLA_WIDTH + 2 * GLA_KEY + LANES
W_HG = 4 * HGRN_WIDTH
W_IN_PAD = W_NA + W_GATE + W_GLA + W_HG


def _dot(a, b):
    return jnp.dot(a, b, preferred_element_type=F32)


def _dot_nt(a, b):
    return lax.dot_general(a, b, (((1,), (1,)), ((), ())), preferred_element_type=F32)


def _dot_tn(a, b):
    return lax.dot_general(a, b, (((0,), (0,)), ((), ())), preferred_element_type=F32)


def _sigmoid(x):
    return 1.0 / (1.0 + jnp.exp(-x))


def _silu(x):
    return x * _sigmoid(x)


def _split3(x):
    hi = x.astype(BF16)
    r1 = x - hi.astype(F32)
    mid = r1.astype(BF16)
    lo = (r1 - mid.astype(F32)).astype(BF16)
    return hi, mid, lo


def _mod_kernel(c_ref, w_ref, b_ref, o_ref):
    s = _silu(c_ref[...]).astype(BF16)
    o_ref[0] = _dot(s, w_ref[0].astype(BF16)) + b_ref[0]


def _modulation(cc, ada_w, ada_b):
    depth = ada_w.shape[0]
    rows = cc.shape[0]
    return pl.pallas_call(
        _mod_kernel,
        grid=(depth, 3),
        in_specs=[
            pl.BlockSpec((rows, D_MODEL), lambda l, j: (0, 0)),
            pl.BlockSpec((1, D_MODEL, D_MODEL), lambda l, j: (l, 0, j)),
            pl.BlockSpec((1, 1, D_MODEL), lambda l, j: (l, 0, j)),
        ],
        out_specs=pl.BlockSpec((1, rows, D_MODEL), lambda l, j: (l, 0, j)),
        out_shape=jax.ShapeDtypeStruct((depth, rows, 3 * D_MODEL), F32),
        compiler_params=pltpu.CompilerParams(
            dimension_semantics=("parallel", "parallel"), vmem_limit_bytes=VMEM_LIMIT),
        name="adaln_mod",
    )(cc, ada_w, ada_b.reshape(depth, 1, 3 * D_MODEL))


def _in_proj_kernel(n_streams, n_batch, *refs):
    x_refs = refs[:n_streams]
    mod_ref, nw_ref, w_ref = refs[n_streams:n_streams + 3]
    q_ref, k_ref, v_ref, gate_ref, gla_ref, hg_ref = refs[n_streams + 3:]
    b = pl.program_id(0)
    t = pl.program_id(1)
    is_ctx = t < CTX_LEN // TOKEN_TILE
    if n_streams == 2:
        x = jnp.where(is_ctx, x_refs[0][0], x_refs[1][0])
    else:
        x = x_refs[0][0]
    row = jnp.where(is_ctx, n_batch, b)
    m = mod_ref[pl.ds(row, 1), :]
    sh = m[:, :D_MODEL]
    sc = m[:, D_MODEL:2 * D_MODEL]
    ms = jnp.mean(x * x, axis=-1, keepdims=True)
    y = x * lax.rsqrt(ms + NORM_EPS) * nw_ref[...]
    h = y * (1.0 + sc) + sh
    p = _dot(h.astype(BF16), w_ref[...])
    for pr in range(NA_HEADS // 2):
        lo = pr * LANES
        q_ref[0, pr] = p[:, lo:lo + LANES].astype(BF16)
        k_ref[0, pr] = p[:, NA_WIDTH + lo:NA_WIDTH + lo + LANES].astype(BF16)
        v_ref[0, pr] = p[:, 2 * NA_WIDTH + lo:2 * NA_WIDTH + lo + LANES].astype(BF16)
    gate_ref[0] = p[:, W_NA:W_NA + W_GATE].astype(BF16)
    gla_ref[0] = p[:, W_NA + W_GATE:W_NA + W_GATE + W_GLA]
    hg_ref[0] = p[:, W_NA + W_GATE + W_GLA:]


def _in_proj(streams, mod_l, norm_w_l, w_in_p, layer, n_batch, t_all):
    n_tiles = t_all // TOKEN_TILE
    n_ctx_tiles = CTX_LEN // TOKEN_TILE
    if len(streams) == 2:
        x_specs = [
            pl.BlockSpec((1, TOKEN_TILE, D_MODEL),
                         lambda b, t: (b, jnp.minimum(t, n_ctx_tiles - 1), 0)),
            pl.BlockSpec((1, TOKEN_TILE, D_MODEL),
                         lambda b, t: (b, jnp.maximum(t - n_ctx_tiles, 0), 0)),
        ]
    else:
        x_specs = [pl.BlockSpec((1, TOKEN_TILE, D_MODEL), lambda b, t: (b, t, 0))]
    pair_spec = pl.BlockSpec((1, NA_HEADS // 2, TOKEN_TILE, LANES), lambda b, t: (b, 0, t, 0))
    pair_shape = jax.ShapeDtypeStruct((n_batch, NA_HEADS // 2, t_all, LANES), BF16)

    def slab(width, dtype):
        return (pl.BlockSpec((1, TOKEN_TILE, width), lambda b, t: (b, t, 0)),
                jax.ShapeDtypeStruct((n_batch, t_all, width), dtype))

    gate_spec, gate_shape = slab(W_GATE, BF16)
    gla_spec, gla_shape = slab(W_GLA, F32)
    hg_spec, hg_shape = slab(W_HG, F32)
    return pl.pallas_call(
        functools.partial(_in_proj_kernel, len(streams), n_batch),
        grid=(n_batch, n_tiles),
        in_specs=x_specs + [
            pl.BlockSpec(mod_l.shape, lambda b, t: (0, 0)),
            pl.BlockSpec((1, D_MODEL), lambda b, t: (0, 0)),
            pl.BlockSpec((None, D_MODEL, W_IN_PAD), lambda b, t: (layer, 0, 0),
                         pipeline_mode=pl.Buffered(1)),
        ],
        out_specs=[pair_spec, pair_spec, pair_spec, gate_spec, gla_spec, hg_spec],
        out_shape=[pair_shape, pair_shape, pair_shape, gate_shape, gla_shape, hg_shape],
        compiler_params=pltpu.CompilerParams(
            dimension_semantics=("parallel", "parallel"), vmem_limit_bytes=VMEM_LIMIT),
        name="in_proj",
    )(*streams, mod_l, norm_w_l, w_in_p)


def _na_kernel(tile_off, q_ref, k_ref, v_ref, bias_ref, o_ref):
    tb = pl.program_id(1) + tile_off
    n_pairs = q_ref.shape[1]
    lane = lax.broadcasted_iota(jnp.int32, (1, LANES), 1)
    q_scale = [jnp.where((lane // HEAD_DIM) == hh, HEAD_DIM ** -0.5, 0.0).astype(BF16) for hh in range(2)]
    heads = [(p, hh) for p in range(n_pairs) for hh in range(2)]

    def finish(outs):
        for p in range(n_pairs):
            o_ref[0, :, p * LANES:(p + 1) * LANES] = jnp.where(
                lane < HEAD_DIM, outs[(p, 0)], outs[(p, 1)]).astype(o_ref.dtype)

    def pipelined(scores_of, attend):
        outs, pending = {}, None
        for i in range(len(heads) + 1):
            issued = scores_of(*heads[i]) if i < len(heads) else None
            if pending is not None:
                outs[heads[i - 1]] = attend(heads[i - 1][0], *pending)
            pending = issued
        finish(outs)

    @pl.when(tb == 0)
    def _ctx_queries():
        def scores_of(p, hh):
            return (_dot_nt(q_ref[0, p] * q_scale[hh], k_ref[0, p, 0:CTX_LEN, :]),)

        def attend(p, s):
            e = jnp.exp(s - jnp.max(s, axis=-1, keepdims=True))
            l = jnp.sum(e, axis=-1, keepdims=True)
            return _dot(e.astype(BF16), v_ref[0, p, 0:CTX_LEN, :]) / l

        pipelined(scores_of, attend)

    @pl.when(tb > 0)
    def _latent_queries():
        g = tb - 1
        n_groups = pl.num_programs(1) - 1 + tile_off
        ws = jnp.clip(NA_Q_ROWS * g - NA_WIN_H // 2, 0, GRID_W - NA_KEY_ROWS)
        case = (g > 0).astype(jnp.int32) + (g == n_groups - 1).astype(jnp.int32)
        start = pl.multiple_of(CTX_LEN + ws * GRID_W, GRID_W)

        def scores_of(p, hh):
            qh = q_ref[0, p] * q_scale[hh]
            s_nb = _dot_nt(qh, k_ref[0, p, pl.ds(start, NA_KEYS), :]) + bias_ref[case, 2 * p + hh]
            return s_nb, _dot_nt(qh, k_ref[0, p, 0:CTX_LEN, :])

        def attend(p, s_nb, s_cx):
            m = jnp.maximum(jnp.max(s_nb, axis=-1, keepdims=True),
                            jnp.max(s_cx, axis=-1, keepdims=True))
            e_nb = jnp.exp(s_nb - m)
            e_cx = jnp.exp(s_cx - m)
            l = jnp.sum(e_nb, axis=-1, keepdims=True) + jnp.sum(e_cx, axis=-1, keepdims=True)
            o = (_dot(e_nb.astype(BF16), v_ref[0, p, pl.ds(start, NA_KEYS), :])
                 + _dot(e_cx.astype(BF16), v_ref[0, p, 0:CTX_LEN, :]))
            return o / l

        pipelined(scores_of, attend)


def _neighborhood_attention(q, k, v, bias, layer, with_ctx_queries):
    n_batch, n_pairs, t_all, _ = q.shape
    n_tiles = t_all // TOKEN_TILE
    tile_off = 0 if with_ctx_queries else CTX_LEN // TOKEN_TILE
    n_steps = n_tiles - tile_off
    t_out = n_steps * TOKEN_TILE
    return pl.pallas_call(
        functools.partial(_na_kernel, tile_off),
        grid=(n_batch, n_steps),
        in_specs=[
            pl.BlockSpec((1, n_pairs, TOKEN_TILE, LANES), lambda b, g: (b, 0, g + tile_off, 0)),
            pl.BlockSpec((1, n_pairs, t_all, LANES), lambda b, g: (b, 0, 0, 0)),
            pl.BlockSpec((1, n_pairs, t_all, LANES), lambda b, g: (b, 0, 0, 0)),
            pl.BlockSpec((None,) + bias.shape[1:], lambda b, g: (layer, 0, 0, 0, 0),
                         pipeline_mode=pl.Buffered(1)),
        ],
        out_specs=pl.BlockSpec((1, TOKEN_TILE, NA_WIDTH), lambda b, g: (b, g, 0)),
        out_shape=jax.ShapeDtypeStruct((n_batch, t_out, NA_WIDTH), BF16),
        compiler_params=pltpu.CompilerParams(
            dimension_semantics=("parallel", "arbitrary"),
            vmem_limit_bytes=VMEM_LIMIT),
        name="neighborhood_attention",
    )(q, k, v, bias)


def _na_bias_tables(rpb):
    col = np.arange(GRID_W)
    col_start = np.clip(col - NA_WIN_W // 2, 0, GRID_W - NA_WIN_W)
    col_in = (col[None, :] >= col_start[:, None]) & (col[None, :] < col_start[:, None] + NA_WIN_W)
    col_idx = np.clip(col[None, :] - col[:, None], 1 - NA_WIN_W, NA_WIN_W - 1) + NA_WIN_W - 1
    rpb_cols = jnp.where(col_in, rpb[..., col_idx].astype(F32), -jnp.inf)
    rpb_cols = jnp.swapaxes(rpb_cols, 2, 3)
    lr = np.arange(NA_Q_ROWS)[:, None]
    lk = np.arange(NA_KEY_ROWS)[None, :]
    cases = [
        (lk + 0 * lr < NA_WIN_H, lk - lr + NA_WIN_H - 1),
        ((lk >= lr) & (lk < lr + NA_WIN_H), lk - lr + NA_WIN_H // 2 - 1),
        ((lk >= NA_KEY_ROWS - NA_WIN_H) & (lr >= 0), lk - lr),
    ]
    valid = np.stack([c[0] for c in cases])
    ridx = np.clip(np.stack([c[1] for c in cases]), 0, 2 * NA_WIN_H - 2)
    t = rpb_cols[:, :, :, ridx, :]
    t = jnp.where(valid[:, :, :, None], t, -jnp.inf)
    t = t.reshape(t.shape[:5] + (NA_KEYS,))
    t = jnp.transpose(t, (0, 3, 1, 4, 2, 5))
    return t.reshape(t.shape[0], 3, NA_HEADS, TOKEN_TILE, NA_KEYS)


def _scan_constants():
    idx = np.arange(SCAN_BLOCK)
    c = idx // CHUNK
    pos = idx % CHUNK
    same = c[:, None] == c[None, :]
    return {"cum_f": (same & (pos[None, :] <= pos[:, None])).astype(np.float32),
            "cum_b": (same & (pos[None, :] >= pos[:, None])).astype(np.float32)}


def _block_matrix(n_rows, n_cols, block_shape, dtype, block_at):
    zero = jnp.zeros(block_shape, dtype)
    rows = []
    for r in range(n_rows):
        blocks = [block_at(r, c) for c in range(n_cols)]
        rows.append(jnp.concatenate([zero if b is None else b for b in blocks], axis=1))
    return jnp.concatenate(rows, axis=0)


def _run_interleaved(stagewise):
    results = [None] * len(stagewise)
    live = list(range(len(stagewise)))
    while live:
        for i in list(live):
            try:
                next(stagewise[i])
            except StopIteration as done:
                results[i] = done.value
                live.remove(i)
    return results


def _scan_direction(load_inputs, cum, st_ref, reverse):
    q, k, v, lf = load_inputs()
    n = SCAN_BLOCK
    hd = q.shape[1]
    dk = hd // SCAN_HEADS
    pack = 2 * LANES // hd
    low_half = lax.broadcasted_iota(jnp.int32, (1, LANES), 1) < HEAD_DIM
    lane_k = lax.broadcasted_iota(jnp.int32, (1, hd), 1) // dk
    lane_v = lax.broadcasted_iota(jnp.int32, (1, SCAN_HEADS * HEAD_DIM), 1) // HEAD_DIM

    def lane_mask(cond):
        return jnp.where(cond, 1.0, 0.0).astype(BF16)

    half_masks = {True: lane_mask(low_half), False: lane_mask(jnp.logical_not(low_half))}
    k_masks = [lane_mask(lane_k == h) for h in range(SCAN_HEADS)]
    v_masks = [lane_mask(lane_v == h) for h in range(SCAN_HEADS)]

    def keep_half(x, low):
        return x * half_masks[low]

    hi, mid, lo = _split3(lf)
    g = _dot(cum, hi) + _dot(cum, mid) + _dot(cum, lo)
    yield
    ref_pos, last_pos = (CHUNK // 2, 0) if reverse else (CHUNK // 2 - 1, CHUNK - 1)
    to_ref, to_last, decay = [], [], []
    for c in range(CHUNKS_PER_BLOCK):
        gc = g[c * CHUNK:(c + 1) * CHUNK]
        g_last = gc[last_pos:last_pos + 1]
        to_ref.append(gc - gc[ref_pos:ref_pos + 1])
        to_last.append(g_last - gc)
        decay.append(jnp.exp(g_last))
    x_ref = jnp.concatenate(to_ref, axis=0)
    a = jnp.exp(x_ref)
    a_inv = jnp.exp(-x_ref)
    e_in = jnp.exp(g)
    e_out = jnp.exp(jnp.concatenate(to_last, axis=0))

    q_t = (q * a).astype(BF16)
    k_t_tr = jnp.transpose(k * a_inv).astype(BF16)
    q_in = (q * e_in).astype(BF16)
    k_out = (k * e_out).astype(BF16)
    v_b = v.astype(BF16)

    k_blk = _block_matrix(SCAN_HEADS, SCAN_HEADS, (dk, n), BF16,
                          lambda r, c: k_t_tr[r * dk:(r + 1) * dk] if r == c else None)
    scores = _dot(q_t, k_blk)

    v_tr = jnp.transpose(v).astype(BF16)
    token_group = lax.broadcasted_iota(jnp.int32, (1, n), 1) // (CHUNK * pack)
    group_masks = [lane_mask(token_group == m) for m in range(CHUNKS_PER_BLOCK // pack)]
    v_blk = jnp.concatenate(
        [jnp.concatenate([v_tr[h * HEAD_DIM:(h + 1) * HEAD_DIM] * gm for h in range(SCAN_HEADS)], axis=1)
         for gm in group_masks], axis=0)
    k_rows = []
    for h in range(SCAN_HEADS):
        k_h = k_out * k_masks[h]
        if pack > 1:
            k_h = _block_matrix(CHUNKS_PER_BLOCK, pack, (CHUNK, hd), BF16,
                                lambda c, par, k_h=k_h: k_h[c * CHUNK:(c + 1) * CHUNK] if c % pack == par else None)
        k_rows.append(k_h)
    k_stack = jnp.concatenate(k_rows, axis=0)
    inc = _dot(v_blk, k_stack)
    yield

    tri = jnp.concatenate([cum.astype(F32)] * SCAN_HEADS, axis=1) > 0.5
    p = jnp.where(tri, scores, 0.0).astype(BF16)
    v_stack = jnp.concatenate([v_b * v_masks[h] for h in range(SCAN_HEADS)], axis=0)
    o_intra = _dot(p, v_stack)
    yield

    st = st_ref[...]
    states = [None] * CHUNKS_PER_BLOCK
    order = range(CHUNKS_PER_BLOCK - 1, -1, -1) if reverse else range(CHUNKS_PER_BLOCK)
    for c in order:
        states[c] = st
        m, par = divmod(c, pack)
        st = decay[c] * st + inc[m * HEAD_DIM:(m + 1) * HEAD_DIM, par * hd:(par + 1) * hd]
    st_ref[...] = st

    q_blk = _block_matrix(CHUNKS_PER_BLOCK, CHUNKS_PER_BLOCK, (CHUNK, hd), BF16,
                          lambda r, c: q_in[r * CHUNK:(r + 1) * CHUNK] if r == c else None)
    s_parts = []
    for s in states:
        s_tr = jnp.transpose(jnp.concatenate([s, s], axis=0)).astype(BF16)

        def s_block(h, j, s_tr=s_tr):
            return keep_half(s_tr[h * dk:(h + 1) * dk], h % 2 == 0) if h // 2 == j else None

        s_parts.append(_block_matrix(SCAN_HEADS, SCAN_HEADS // 2, (dk, LANES), BF16, s_block))
    return o_intra + _dot(q_blk, jnp.concatenate(s_parts, axis=0))


def _rope(x, cos, sin_signed):
    lane = lax.broadcasted_iota(jnp.int32, (1, x.shape[1]), 1)
    first = (lane % 16) < 8
    partner = jnp.where(first, pltpu.roll(x, x.shape[1] - 8, 1), pltpu.roll(x, 8, 1))
    return x * cos + partner * sin_signed


def _scan_kernel(n_ctx_blocks,
                 gla_f_ref, gla_b_ref, hg_f_ref, hg_b_ref,
                 cos_f_ref, sin_f_ref, cos_b_ref, sin_b_ref,
                 wgk_ref, bgk_ref, lb_ref, cum_f_ref, cum_b_ref,
                 ogf_ref, ogb_ref, ohf_ref, ohb_ref,
                 sgf_ref, sgb_ref, shf_ref, shb_ref):
    del n_ctx_blocks

    @pl.when(pl.program_id(1) == 0)
    def _reset():
        for ref in (sgf_ref, sgb_ref, shf_ref, shb_ref):
            ref[...] = jnp.zeros_like(ref)

    def gla_inputs(in_ref, cos_ref, sin_ref, direction, rows):
        blk = in_ref[0, rows, :]
        v = blk[:, 0:GLA_WIDTH]
        q = blk[:, GLA_WIDTH:GLA_WIDTH + GLA_KEY]
        k = blk[:, GLA_WIDTH + GLA_KEY:GLA_WIDTH + 2 * GLA_KEY]
        gk = blk[:, GLA_WIDTH + 2 * GLA_KEY:]
        cos = cos_ref[rows, :]
        sin = sin_ref[rows, :]
        q = _rope(q, cos, sin) * (GLA_KEY // SCAN_HEADS) ** -0.5
        k = _rope(k, cos, sin)
        lo = direction * GLA_KEY
        z = _dot(gk.astype(BF16), wgk_ref[:, lo:lo + GLA_KEY]) + bgk_ref[:, lo:lo + GLA_KEY]
        log_sig = jnp.minimum(z, 0.0) - jnp.log(1.0 + jnp.exp(-jnp.abs(z)))
        return q, k, v, log_sig / GLA_GATE_NORM

    def hgrn_inputs(in_ref, direction, rows):
        blk = in_ref[0, rows, :]
        q = blk[:, 0:HGRN_WIDTH] * HEAD_DIM ** -0.5
        f_raw = blk[:, (1 + direction) * HGRN_WIDTH:(2 + direction) * HGRN_WIDTH]
        v = blk[:, 3 * HGRN_WIDTH:]
        lb = lb_ref[direction:direction + 1, :]
        f = lb + (1.0 - lb) * _sigmoid(f_raw)
        return q, 1.0 - f, v, jnp.log(f)

    jobs, dests = [], []
    for s in range(SCAN_SUB):
        rows_f = pl.ds(s * SCAN_BLOCK, SCAN_BLOCK)
        rows_b = pl.ds((SCAN_SUB - 1 - s) * SCAN_BLOCK, SCAN_BLOCK)
        jobs += [
            _scan_direction(functools.partial(gla_inputs, gla_f_ref, cos_f_ref, sin_f_ref, 0, rows_f),
                            cum_f_ref[...],sgf_ref, reverse=False),
            _scan_direction(functools.partial(hgrn_inputs, hg_f_ref, 0, rows_f),
                            cum_f_ref[...],shf_ref, reverse=False),
            _scan_direction(functools.partial(gla_inputs, gla_b_ref, cos_b_ref, sin_b_ref, 1, rows_b),
                            cum_b_ref[...],sgb_ref, reverse=True),
            _scan_direction(functools.partial(hgrn_inputs, hg_b_ref, 1, rows_b),
                            cum_b_ref[...],shb_ref, reverse=True),
        ]
        dests += [(ogf_ref, rows_f), (ohf_ref, rows_f), (ogb_ref, rows_b), (ohb_ref, rows_b)]
    for (ref, rows), val in zip(dests, _run_interleaved(jobs)):
        ref[0, rows, :] = val.astype(ref.dtype)


def _scans(gla_in, hg_in, cos_t, sin_t, wgk, bgk, lb, consts):
    n_batch, t_all, _ = gla_in.shape
    step_rows = SCAN_SUB * SCAN_BLOCK
    assert CTX_LEN % step_rows == 0 and t_all % step_rows == 0
    n_blocks = t_all // step_rows
    n_ctx_blocks = CTX_LEN // step_rows

    def fwd(i):
        return i

    def bwd(i):
        return jnp.where(i < n_ctx_blocks, n_ctx_blocks - 1 - i, n_blocks - 1 + n_ctx_blocks - i)

    def tok_spec(width, order):
        return pl.BlockSpec((1, step_rows, width), lambda b, i: (b, order(i), 0))

    def tab_spec(order):
        return pl.BlockSpec((step_rows, LANES), lambda b, i: (order(i), 0))

    def const_spec(arr):
        return pl.BlockSpec(arr.shape, lambda b, i: (0,) * arr.ndim)

    const_arrays = [wgk, bgk, lb, consts["cum_f"], consts["cum_b"]]
    out_shape = jax.ShapeDtypeStruct((n_batch, t_all, GLA_WIDTH), BF16)
    return pl.pallas_call(
        functools.partial(_scan_kernel, n_ctx_blocks),
        grid=(n_batch, n_blocks),
        in_specs=[tok_spec(W_GLA, fwd), tok_spec(W_GLA, bwd), tok_spec(W_HG, fwd), tok_spec(W_HG, bwd),
                  tab_spec(fwd), tab_spec(fwd), tab_spec(bwd), tab_spec(bwd)]
                 + [const_spec(a) for a in const_arrays],
        out_specs=[tok_spec(GLA_WIDTH, fwd), tok_spec(GLA_WIDTH, bwd),
                   tok_spec(HGRN_WIDTH, fwd), tok_spec(HGRN_WIDTH, bwd)],
        out_shape=[out_shape] * 4,
        scratch_shapes=[pltpu.VMEM((HEAD_DIM, GLA_KEY), F32), pltpu.VMEM((HEAD_DIM, GLA_KEY), F32),
                        pltpu.VMEM((HEAD_DIM, HGRN_WIDTH), F32), pltpu.VMEM((HEAD_DIM, HGRN_WIDTH), F32)],
        compiler_params=pltpu.CompilerParams(
            dimension_semantics=("parallel", "arbitrary"), vmem_limit_bytes=VMEM_LIMIT),
        name="bidirectional_scans",
    )(gla_in, gla_in, hg_in, hg_in, cos_t, sin_t, cos_t, sin_t, *const_arrays)


def _rope_tables(t_lat):
    half = GLA_KEY // SCAN_HEADS // 2
    inv_freq = 1.0 / (ROPE_BASE ** (np.arange(0, half, 2, dtype=np.float32) / half))
    t = np.arange(t_lat)
    lane = np.arange(LANES) % (2 * half)
    pos = np.where(lane[None, :] < half, (t // GRID_W)[:, None], (t % GRID_W)[:, None]).astype(np.float32)
    ang = (pos * inv_freq[(lane % half) % (half // 2)][None, :]).astype(np.float32)
    first = ((lane % half) < half // 2)[None, :]
    cos = np.cos(ang)
    sin = np.where(first, -np.sin(ang), np.sin(ang))
    cos = np.concatenate([np.ones((CTX_LEN, LANES), np.float32), cos], axis=0)
    sin = np.concatenate([np.zeros((CTX_LEN, LANES), np.float32), sin], axis=0)
    return jnp.asarray(cos, F32), jnp.asarray(sin, F32)


def _out_proj_kernel(n_streams, n_batch, tile_off, final,
                     *refs):
    x_refs = refs[:n_streams]
    (ona_ref, ogf_ref, ogb_ref, ohf_ref, ohb_ref, gate_ref, mod_ref,
     gnw_ref, hnw_ref, bd_ref, w_ref) = refs[n_streams:n_streams + 11]
    rest = refs[n_streams + 11:]
    b = pl.program_id(0)
    t = pl.program_id(1) + tile_off
    is_ctx = t < CTX_LEN // TOKEN_TILE
    if n_streams == 2:
        x = jnp.where(is_ctx, x_refs[0][0], x_refs[1][0])
    else:
        x = x_refs[0][0]
    row = jnp.where(is_ctx, n_batch, b)
    gt = mod_ref[pl.ds(row, 1), :][:, 2 * D_MODEL:]
    bd = bd_ref[...]

    def head_norm(o, w):
        sq = o * o
        hi = sq.astype(BF16)
        lo = (sq - hi.astype(F32)).astype(BF16)
        ms = _dot(hi, bd) + _dot(lo, bd)
        return o * lax.rsqrt(ms + NORM_EPS) * w

    o_gla = head_norm(ogf_ref[0].astype(F32) + ogb_ref[0].astype(F32), gnw_ref[...])
    o_hg = head_norm(ohf_ref[0].astype(F32) + ohb_ref[0].astype(F32), hnw_ref[...])
    o_cat = jnp.concatenate([ona_ref[0].astype(F32), o_gla, o_hg], axis=1)
    act = o_cat * _silu(gate_ref[0].astype(F32))
    y = _dot(act.astype(BF16), w_ref[...])
    xn = x + gt * y
    if final:
        fw_ref, out_ref = rest
        ms = jnp.mean(xn * xn, axis=-1, keepdims=True)
        out_ref[0] = xn * lax.rsqrt(ms + NORM_EPS) * fw_ref[...]
    else:
        (out_ref,) = rest
        out_ref[0] = xn


def _out_proj(streams, o_na, scans, gates, mod_l, gla_nw, hg_nw, bd, w_out_b, n_batch, t_all,
              final_w=None):
    final = final_w is not None
    n_ctx_tiles = CTX_LEN // TOKEN_TILE
    tile_off = n_ctx_tiles if final else 0
    n_steps = t_all // TOKEN_TILE - tile_off
    if len(streams) == 2:
        x_specs = [
            pl.BlockSpec((1, TOKEN_TILE, D_MODEL),
                         lambda b, t: (b, jnp.minimum(t, n_ctx_tiles - 1), 0)),
            pl.BlockSpec((1, TOKEN_TILE, D_MODEL),
                         lambda b, t: (b, jnp.maximum(t - n_ctx_tiles, 0), 0)),
        ]
    else:
        x_specs = [pl.BlockSpec((1, TOKEN_TILE, D_MODEL), lambda b, t: (b, t + tile_off, 0))]

    def tok(width):
        return pl.BlockSpec((1, TOKEN_TILE, width), lambda b, t: (b, t + tile_off, 0))

    def const_spec(arr):
        return pl.BlockSpec(arr.shape, lambda b, t: (0,) * arr.ndim)

    ona_spec = pl.BlockSpec((1, TOKEN_TILE, NA_WIDTH), lambda b, t: (b, t, 0))
    consts = [mod_l, gla_nw, hg_nw, bd, w_out_b] + ([final_w] if final else [])
    return pl.pallas_call(
        functools.partial(_out_proj_kernel, len(streams), n_batch, tile_off, final),
        grid=(n_batch, n_steps),
        in_specs=x_specs + [ona_spec] + [tok(GLA_WIDTH)] * 4 + [tok(W_GATE)]
                 + [const_spec(a) for a in consts],
        out_specs=pl.BlockSpec((1, TOKEN_TILE, D_MODEL), lambda b, t: (b, t, 0)),
        out_shape=jax.ShapeDtypeStruct((n_batch, n_steps * TOKEN_TILE, D_MODEL), F32),
        compiler_params=pltpu.CompilerParams(
            dimension_semantics=("parallel", "parallel"), vmem_limit_bytes=VMEM_LIMIT),
        name="out_proj",
    )(*streams, o_na, *scans, gates, *consts)


def _permute_w_in(w):
    w = w.astype(BF16)
    seg = [w[..., IN_OFFS[i]:IN_OFFS[i + 1]] for i in range(len(IN_SPLITS))]
    pad = jnp.zeros(w.shape[:-1] + (GK_PAD,), w.dtype)
    cols = [seg[0], seg[1], seg[2],
            seg[3], seg[8], seg[13],
            seg[6], seg[4], seg[5], seg[7], pad,
            seg[9], seg[10], seg[11], seg[12]]
    return jnp.concatenate(cols, axis=-1)


def kernel(x, c, ctx, c_ctx, ada_w, ada_b, norm_w, w_in, na_rpb, gla_w_gk, gla_b_gk, gla_norm_w,
           hgrn_lower_bounds, hgrn_norm_w, w_out, final_norm_w):
    n_batch, t_lat, _ = x.shape
    depth = ada_w.shape[0]
    t_all = CTX_LEN + t_lat
    assert ctx.shape[1] == CTX_LEN and t_lat % TOKEN_TILE == 0 and t_lat // GRID_W >= NA_KEY_ROWS

    lbs = jnp.cumsum(jax.nn.softmax(hgrn_lower_bounds.astype(F32), axis=0), axis=0)
    lbs = lbs - lbs[0:1]
    mod_rows = -(-(n_batch + 1) // 8) * 8
    cc = jnp.zeros((mod_rows, D_MODEL), F32).at[:n_batch].set(c).at[n_batch].set(c_ctx)
    mod = _modulation(cc, ada_w, ada_b)

    consts = {k_: jnp.asarray(v_, BF16) for k_, v_ in _scan_constants().items()}
    cos_t, sin_t = _rope_tables(t_lat)
    head_of = np.arange(GLA_WIDTH) // HEAD_DIM
    bd = jnp.asarray((head_of[:, None] == head_of[None, :]).astype(np.float32) / HEAD_DIM, BF16)

    w_in_p = _permute_w_in(w_in)
    w_out_b = w_out.astype(BF16)
    bias = _na_bias_tables(na_rpb)
    wgk = jnp.zeros((depth, LANES, 2 * GLA_KEY), F32)
    wgk = wgk.at[:, 0:GLA_GK_RANK, 0:GLA_KEY].set(gla_w_gk[:, 0])
    wgk = wgk.at[:, GLA_GK_RANK:2 * GLA_GK_RANK, GLA_KEY:].set(gla_w_gk[:, 1]).astype(BF16)

    streams = [ctx, x]
    out = None
    for layer in range(depth):
        last = layer == depth - 1
        bgk = gla_b_gk[layer].reshape(1, 2 * GLA_KEY)
        gla_nw = jnp.tile(gla_norm_w[layer], SCAN_HEADS).reshape(1, GLA_WIDTH)
        hg_nw = jnp.tile(hgrn_norm_w[layer], SCAN_HEADS).reshape(1, HGRN_WIDTH)

        q, k, v, gates, gla_in, hg_in = _in_proj(
            streams, mod[layer], norm_w[layer].reshape(1, D_MODEL), w_in_p, layer, n_batch, t_all)
        o_na = _neighborhood_attention(q, k, v, bias, layer, with_ctx_queries=not last)
        scans = _scans(gla_in, hg_in, cos_t, sin_t, wgk[layer], bgk, lbs[layer], consts)
        out = _out_proj(streams, o_na, scans, gates, mod[layer], gla_nw, hg_nw, bd,
                        w_out_b[layer], n_batch, t_all,
                        final_w=final_norm_w.reshape(1, D_MODEL) if last else None)
        streams = [out]
    return out
```

```python
import functools

import numpy as np
import jax
import jax.numpy as jnp
from jax import lax
from jax.experimental import pallas as pl
from jax.experimental.pallas import tpu as pltpu

F32 = jnp.float32
BF16 = jnp.bfloat16

D_MODEL = 1024
CTX_LEN = 256
GRID_W = 64
NORM_EPS = 1e-6
HEAD_DIM = 64
NA_WIDTH = 512
NA_HEADS = 8
NA_WIN_H = 8
NA_WIN_W = 16
GLA_WIDTH = 256
GLA_KEY = 128
GLA_GK_RANK = 16
GLA_GATE_NORM = 16.0
HGRN_WIDTH = 256
SCAN_HEADS = 4
CHUNK = 16
ROPE_BASE = 10000.0

LOG2_E = float(np.log2(np.e))
NA_Q_SCALE = HEAD_DIM ** -0.5 * LOG2_E

LANES = 128
TOKEN_TILE = 256
SCAN_BLOCK = 128
SCAN_SUB = 2
CHUNKS_PER_BLOCK = SCAN_BLOCK // CHUNK
NA_Q_ROWS = TOKEN_TILE // GRID_W
NA_KEY_ROWS = NA_Q_ROWS + NA_WIN_H
NA_KEYS = NA_KEY_ROWS * GRID_W
assert NA_KEYS % LANES == 0 and (NA_Q_ROWS * GRID_W) % LANES == 0 and NA_WIN_H % 4 == 0
VMEM_LIMIT = 56 * 1024 * 1024

IN_SPLITS = (512, 512, 512, 512, 128, 128, 256, 32, 256, 256, 256, 256, 256, 256)
IN_OFFS = tuple(int(v) for v in np.cumsum((0,) + IN_SPLITS))
GK_PAD = LANES - 2 * GLA_GK_RANK
W_NA = 3 * NA_WIDTH
W_GATE = NA_WIDTH + GLA_WIDTH + HGRN_WIDTH
W_GLA = GLA_WIDTH + 2 * GLA_KEY + LANES
W_HG = 4 * HGRN_WIDTH
W_IN_PAD = W_NA + W_GATE + W_GLA + W_HG


def _dot(a, b):
    return jnp.dot(a, b, preferred_element_type=F32)


def _dot_nt(a, b):
    return lax.dot_general(a, b, (((1,), (1,)), ((), ())), preferred_element_type=F32)


def _dot_tn(a, b):
    return lax.dot_general(a, b, (((0,), (0,)), ((), ())), preferred_element_type=F32)


def _sigmoid(x):
    return 1.0 / (1.0 + jnp.exp(-x))


def _silu(x):
    return x * _sigmoid(x)


def _split3(x):
    hi = x.astype(BF16)
    r1 = x - hi.astype(F32)
    mid = r1.astype(BF16)
    lo = (r1 - mid.astype(F32)).astype(BF16)
    return hi, mid, lo


def _mod_kernel(c_ref, w_ref, b_ref, o_ref):
    s = _silu(c_ref[...]).astype(BF16)
    o_ref[0] = _dot(s, w_ref[0].astype(BF16)) + b_ref[0]


def _modulation(cc, ada_w, ada_b):
    depth = ada_w.shape[0]
    rows = cc.shape[0]
    return pl.pallas_call(
        _mod_kernel,
        grid=(depth, 3),
        in_specs=[
            pl.BlockSpec((rows, D_MODEL), lambda l, j: (0, 0)),
            pl.BlockSpec((1, D_MODEL, D_MODEL), lambda l, j: (l, 0, j)),
            pl.BlockSpec((1, 1, D_MODEL), lambda l, j: (l, 0, j)),
        ],
        out_specs=pl.BlockSpec((1, rows, D_MODEL), lambda l, j: (l, 0, j)),
        out_shape=jax.ShapeDtypeStruct((depth, rows, 3 * D_MODEL), F32),
        compiler_params=pltpu.CompilerParams(
            dimension_semantics=("parallel", "parallel"), vmem_limit_bytes=VMEM_LIMIT),
        name="adaln_mod",
    )(cc, ada_w, ada_b.reshape(depth, 1, 3 * D_MODEL))


def _in_proj_kernel(n_streams, n_batch, *refs):
    x_refs = refs[:n_streams]
    mod_ref, nw_ref, w_ref = refs[n_streams:n_streams + 3]
    q_ref, k_ref, v_ref, gate_ref, gla_ref, hg_ref = refs[n_streams + 3:]
    b = pl.program_id(0)
    t = pl.program_id(1)
    is_ctx = t < CTX_LEN // TOKEN_TILE
    if n_streams == 2:
        x = jnp.where(is_ctx, x_refs[0][0], x_refs[1][0])
    else:
        x = x_refs[0][0]
    row = jnp.where(is_ctx, n_batch, b)
    m = mod_ref[pl.ds(row, 1), :]
    sh = m[:, :D_MODEL]
    sc = m[:, D_MODEL:2 * D_MODEL]
    ms = jnp.mean(x * x, axis=-1, keepdims=True)
    y = x * lax.rsqrt(ms + NORM_EPS) * nw_ref[...]
    h = y * (1.0 + sc) + sh
    p = _dot(h.astype(BF16), w_ref[...])
    for pr in range(NA_HEADS // 2):
        lo = pr * LANES
        q_ref[0, pr] = (p[:, lo:lo + LANES] * NA_Q_SCALE).astype(BF16)
        k_ref[0, pr] = p[:, NA_WIDTH + lo:NA_WIDTH + lo + LANES].astype(BF16)
        for s in range(TOKEN_TILE // LANES):
            v_tile = p[s * LANES:(s + 1) * LANES, 2 * NA_WIDTH + lo:2 * NA_WIDTH + lo + LANES]
            v_ref[0, pr, s] = jnp.transpose(v_tile).astype(BF16)
    gate_ref[0] = p[:, W_NA:W_NA + W_GATE].astype(BF16)
    gla_ref[0] = p[:, W_NA + W_GATE:W_NA + W_GATE + W_GLA]
    hg_ref[0] = p[:, W_NA + W_GATE + W_GLA:]


def _in_proj(streams, mod_l, norm_w_l, w_in_p, layer, n_batch, t_all):
    n_tiles = t_all // TOKEN_TILE
    n_ctx_tiles = CTX_LEN // TOKEN_TILE
    if len(streams) == 2:
        x_specs = [
            pl.BlockSpec((1, TOKEN_TILE, D_MODEL),
                         lambda b, t: (b, jnp.minimum(t, n_ctx_tiles - 1), 0)),
            pl.BlockSpec((1, TOKEN_TILE, D_MODEL),
                         lambda b, t: (b, jnp.maximum(t - n_ctx_tiles, 0), 0)),
        ]
    else:
        x_specs = [pl.BlockSpec((1, TOKEN_TILE, D_MODEL), lambda b, t: (b, t, 0))]
    pair_spec = pl.BlockSpec((1, NA_HEADS // 2, TOKEN_TILE, LANES), lambda b, t: (b, 0, t, 0))
    pair_shape = jax.ShapeDtypeStruct((n_batch, NA_HEADS // 2, t_all, LANES), BF16)
    vt_spec = pl.BlockSpec((1, NA_HEADS // 2, TOKEN_TILE // LANES, LANES, LANES), lambda b, t: (b, 0, t, 0, 0))
    vt_shape = jax.ShapeDtypeStruct((n_batch, NA_HEADS // 2, t_all // LANES, LANES, LANES), BF16)

    def slab(width, dtype):
        return (pl.BlockSpec((1, TOKEN_TILE, width), lambda b, t: (b, t, 0)),
                jax.ShapeDtypeStruct((n_batch, t_all, width), dtype))

    gate_spec, gate_shape = slab(W_GATE, BF16)
    gla_spec, gla_shape = slab(W_GLA, F32)
    hg_spec, hg_shape = slab(W_HG, F32)
    return pl.pallas_call(
        functools.partial(_in_proj_kernel, len(streams), n_batch),
        grid=(n_batch, n_tiles),
        in_specs=x_specs + [
            pl.BlockSpec(mod_l.shape, lambda b, t: (0, 0)),
            pl.BlockSpec((1, D_MODEL), lambda b, t: (0, 0)),
            pl.BlockSpec((None, D_MODEL, W_IN_PAD), lambda b, t: (layer, 0, 0),
                         pipeline_mode=pl.Buffered(1)),
        ],
        out_specs=[pair_spec, pair_spec, vt_spec, gate_spec, gla_spec, hg_spec],
        out_shape=[pair_shape, pair_shape, vt_shape, gate_shape, gla_shape, hg_shape],
        compiler_params=pltpu.CompilerParams(
            dimension_semantics=("parallel", "parallel"), vmem_limit_bytes=VMEM_LIMIT),
        name="in_proj",
    )(*streams, mod_l, norm_w_l, w_in_p)


def _na_bias_cases():
    cases = []
    for key_row_ok, rel in (
        (lambda lr, lk: lk < NA_WIN_H, lambda lr, lk: lk - lr + NA_WIN_H - 1),
        (lambda lr, lk: lr <= lk < lr + NA_WIN_H, lambda lr, lk: lk - lr + NA_WIN_H // 2 - 1),
        (lambda lr, lk: lk >= NA_KEY_ROWS - NA_WIN_H, lambda lr, lk: lk - lr - 1),
    ):
        cases.append([[rel(lr, lk) if key_row_ok(lr, lk) else None for lk in range(NA_KEY_ROWS)]
                      for lr in range(NA_Q_ROWS)])
    return cases


def _na_kernel(tile_off, q_ref, k_ref, vt_ref, rt_ref, o_ref, bias_ref):
    tb = pl.program_id(1) + tile_off
    n_pairs = q_ref.shape[1]
    lane = lax.broadcasted_iota(jnp.int32, (1, LANES), 1)
    q_scale = [jnp.where((lane // HEAD_DIM) == hh, 1.0, 0.0).astype(BF16) for hh in range(2)]
    heads = [(p, hh) for p in range(n_pairs) for hh in range(2)]
    ctx_tiles = CTX_LEN // LANES

    @pl.when((pl.program_id(0) == 0) & (pl.program_id(1) == 0))
    def _build_bias_tables():
        minus_inf = jnp.full((GRID_W, GRID_W), -jnp.inf, F32)

        def per_head(h, carry):
            for case, rel in enumerate(_na_bias_cases()):
                for lk in range(NA_KEY_ROWS):
                    for lr in range(0, NA_Q_ROWS, 2):
                        tiles = [minus_inf if rel[r][lk] is None else rt_ref[h, rel[r][lk]] for r in (lr, lr + 1)]
                        bias_ref[case, h, lk * GRID_W:(lk + 1) * GRID_W, lr * GRID_W:(lr + 2) * GRID_W] = (
                            jnp.concatenate(tiles, axis=1))
            return carry

        lax.fori_loop(0, NA_HEADS, per_head, 0)

    def vt_tiles(p, first, count):
        tiles = vt_ref[0, p, pl.ds(first, count)]
        return jnp.concatenate([tiles[i] for i in range(count)], axis=1)

    def finish(outs):
        row = lax.broadcasted_iota(jnp.int32, (LANES, 1), 0)
        for p in range(n_pairs):
            o_t = jnp.where(row < HEAD_DIM, outs[(p, 0)], outs[(p, 1)])
            o_ref[0, :, p * LANES:(p + 1) * LANES] = jnp.transpose(o_t).astype(o_ref.dtype)

    def pipelined(scores_of, attend):
        outs, pending = {}, None
        for i in range(len(heads) + 1):
            issued = scores_of(*heads[i]) if i < len(heads) else None
            if pending is not None:
                outs[heads[i - 1]] = attend(heads[i - 1][0], *pending)
            pending = issued
        finish(outs)

    def softmax_pv(vt_parts, s_parts):
        m = functools.reduce(jnp.maximum, [jnp.max(s, axis=0, keepdims=True) for s in s_parts])
        e_parts = [jnp.exp2(s - m) for s in s_parts]
        l = functools.reduce(jnp.add, [jnp.sum(e, axis=0, keepdims=True) for e in e_parts])
        o_t = functools.reduce(jnp.add, [_dot(vt, e.astype(BF16)) for vt, e in zip(vt_parts, e_parts)])
        return o_t / l

    @pl.when(tb == 0)
    def _ctx_queries():
        def scores_of(p, hh):
            return (_dot_nt(k_ref[0, p, 0:CTX_LEN, :], q_ref[0, p] * q_scale[hh]),)

        def attend(p, s):
            return softmax_pv([vt_tiles(p, 0, ctx_tiles)], [s])

        pipelined(scores_of, attend)

    @pl.when(tb > 0)
    def _latent_queries():
        g = tb - 1
        n_groups = pl.num_programs(1) - 1 + tile_off
        ws = jnp.clip(NA_Q_ROWS * g - NA_WIN_H // 2, 0, GRID_W - NA_KEY_ROWS)
        case = (g > 0).astype(jnp.int32) + (g == n_groups - 1).astype(jnp.int32)
        start = pl.multiple_of(CTX_LEN + ws * GRID_W, LANES)
        first_tile = ctx_tiles + ws * GRID_W // LANES

        def scores_of(p, hh):
            qh = q_ref[0, p] * q_scale[hh]
            s_nb = _dot_nt(k_ref[0, p, pl.ds(start, NA_KEYS), :], qh) + bias_ref[case, 2 * p + hh]
            return s_nb, _dot_nt(k_ref[0, p, 0:CTX_LEN, :], qh)

        def attend(p, s_nb, s_cx):
            return softmax_pv([vt_tiles(p, first_tile, NA_KEYS // LANES), vt_tiles(p, 0, ctx_tiles)],
                              [s_nb, s_cx])

        pipelined(scores_of, attend)


def _neighborhood_attention(q, k, vt, rpb_tiles, layer, with_ctx_queries):
    n_batch, n_pairs, t_all, _ = q.shape
    n_tiles = t_all // TOKEN_TILE
    tile_off = 0 if with_ctx_queries else CTX_LEN // TOKEN_TILE
    n_steps = n_tiles - tile_off
    t_out = n_steps * TOKEN_TILE
    return pl.pallas_call(
        functools.partial(_na_kernel, tile_off),
        grid=(n_batch, n_steps),
        in_specs=[
            pl.BlockSpec((1, n_pairs, TOKEN_TILE, LANES), lambda b, g: (b, 0, g + tile_off, 0)),
            pl.BlockSpec((1, n_pairs, t_all, LANES), lambda b, g: (b, 0, 0, 0)),
            pl.BlockSpec((1, n_pairs, t_all // LANES, LANES, LANES), lambda b, g: (b, 0, 0, 0, 0)),
            pl.BlockSpec((None,) + rpb_tiles.shape[1:], lambda b, g: (layer, 0, 0, 0, 0),
                         pipeline_mode=pl.Buffered(1)),
        ],
        out_specs=pl.BlockSpec((1, TOKEN_TILE, NA_WIDTH), lambda b, g: (b, g, 0)),
        out_shape=jax.ShapeDtypeStruct((n_batch, t_out, NA_WIDTH), BF16),
        scratch_shapes=[pltpu.VMEM((3, NA_HEADS, NA_KEYS, TOKEN_TILE), F32)],
        compiler_params=pltpu.CompilerParams(
            dimension_semantics=("arbitrary", "arbitrary"),
            vmem_limit_bytes=VMEM_LIMIT),
        name="neighborhood_attention",
    )(q, k, vt, rpb_tiles)


def _na_bias_tiles(rpb):
    col = np.arange(GRID_W)
    col_start = np.clip(col - NA_WIN_W // 2, 0, GRID_W - NA_WIN_W)
    col_in = (col[:, None] >= col_start[None, :]) & (col[:, None] < col_start[None, :] + NA_WIN_W)
    col_idx = np.clip(col[:, None] - col[None, :], 1 - NA_WIN_W, NA_WIN_W - 1) + NA_WIN_W - 1
    return jnp.where(col_in, rpb[..., col_idx].astype(F32) * LOG2_E, -jnp.inf)


def _scan_constants():
    idx = np.arange(SCAN_BLOCK)
    c = idx // CHUNK
    pos = idx % CHUNK
    same = c[:, None] == c[None, :]
    return {"cum_f": (same & (pos[None, :] <= pos[:, None])).astype(np.float32),
            "cum_b": (same & (pos[None, :] >= pos[:, None])).astype(np.float32)}


def _block_matrix(n_rows, n_cols, block_shape, dtype, block_at):
    zero = jnp.zeros(block_shape, dtype)
    rows = []
    for r in range(n_rows):
        blocks = [block_at(r, c) for c in range(n_cols)]
        rows.append(jnp.concatenate([zero if b is None else b for b in blocks], axis=1))
    return jnp.concatenate(rows, axis=0)


def _run_interleaved(stagewise):
    results = [None] * len(stagewise)
    live = list(range(len(stagewise)))
    while live:
        for i in list(live):
            try:
                next(stagewise[i])
            except StopIteration as done:
                results[i] = done.value
                live.remove(i)
    return results


def _scan_direction(load_inputs, cum, st_ref, reverse):
    q, k, v, lf = load_inputs()
    n = SCAN_BLOCK
    hd = q.shape[1]
    dk = hd // SCAN_HEADS
    pack = 2 * LANES // hd
    low_half = lax.broadcasted_iota(jnp.int32, (1, LANES), 1) < HEAD_DIM
    lane_k = lax.broadcasted_iota(jnp.int32, (1, hd), 1) // dk
    lane_v = lax.broadcasted_iota(jnp.int32, (1, SCAN_HEADS * HEAD_DIM), 1) // HEAD_DIM

    def lane_mask(cond):
        return jnp.where(cond, 1.0, 0.0).astype(BF16)

    half_masks = {True: lane_mask(low_half), False: lane_mask(jnp.logical_not(low_half))}
    k_masks = [lane_mask(lane_k == h) for h in range(SCAN_HEADS)]
    v_masks = [lane_mask(lane_v == h) for h in range(SCAN_HEADS)]

    def keep_half(x, low):
        return x * half_masks[low]

    hi, mid, lo = _split3(lf)
    g = _dot(cum, hi) + _dot(cum, mid) + _dot(cum, lo)
    yield
    ref_pos, last_pos = (CHUNK // 2, 0) if reverse else (CHUNK // 2 - 1, CHUNK - 1)
    to_ref, to_last, decay = [], [], []
    for c in range(CHUNKS_PER_BLOCK):
        gc = g[c * CHUNK:(c + 1) * CHUNK]
        g_last = gc[last_pos:last_pos + 1]
        to_ref.append(gc - gc[ref_pos:ref_pos + 1])
        to_last.append(g_last - gc)
        decay.append(jnp.exp(g_last))
    x_ref = jnp.concatenate(to_ref, axis=0)
    a = jnp.exp(x_ref)
    a_inv = jnp.exp(-x_ref)
    e_in = jnp.exp(g)
    e_out = jnp.exp(jnp.concatenate(to_last, axis=0))

    q_t = (q * a).astype(BF16)
    k_t_tr = jnp.transpose(k * a_inv).astype(BF16)
    q_in = (q * e_in).astype(BF16)
    k_out = (k * e_out).astype(BF16)
    v_b = v.astype(BF16)

    k_blk = _block_matrix(SCAN_HEADS, SCAN_HEADS, (dk, n), BF16,
                          lambda r, c: k_t_tr[r * dk:(r + 1) * dk] if r == c else None)
    scores = _dot(q_t, k_blk)

    v_tr = jnp.transpose(v).astype(BF16)
    token_group = lax.broadcasted_iota(jnp.int32, (1, n), 1) // (CHUNK * pack)
    group_masks = [lane_mask(token_group == m) for m in range(CHUNKS_PER_BLOCK // pack)]
    v_blk = jnp.concatenate(
        [jnp.concatenate([v_tr[h * HEAD_DIM:(h + 1) * HEAD_DIM] * gm for h in range(SCAN_HEADS)], axis=1)
         for gm in group_masks], axis=0)
    k_rows = []
    for h in range(SCAN_HEADS):
        k_h = k_out * k_masks[h]
        if pack > 1:
            k_h = _block_matrix(CHUNKS_PER_BLOCK, pack, (CHUNK, hd), BF16,
                                lambda c, par, k_h=k_h: k_h[c * CHUNK:(c + 1) * CHUNK] if c % pack == par else None)
        k_rows.append(k_h)
    k_stack = jnp.concatenate(k_rows, axis=0)
    inc = _dot(v_blk, k_stack)
    yield

    tri = jnp.concatenate([cum.astype(F32)] * SCAN_HEADS, axis=1) > 0.5
    p = jnp.where(tri, scores, 0.0).astype(BF16)
    v_stack = jnp.concatenate([v_b * v_masks[h] for h in range(SCAN_HEADS)], axis=0)
    o_intra = _dot(p, v_stack)
    yield

    st = st_ref[...]
    states = [None] * CHUNKS_PER_BLOCK
    order = range(CHUNKS_PER_BLOCK - 1, -1, -1) if reverse else range(CHUNKS_PER_BLOCK)
    for c in order:
        states[c] = st
        m, par = divmod(c, pack)
        st = decay[c] * st + inc[m * HEAD_DIM:(m + 1) * HEAD_DIM, par * hd:(par + 1) * hd]
    st_ref[...] = st

    q_blk = _block_matrix(CHUNKS_PER_BLOCK, CHUNKS_PER_BLOCK, (CHUNK, hd), BF16,
                          lambda r, c: q_in[r * CHUNK:(r + 1) * CHUNK] if r == c else None)
    s_parts = []
    for s in states:
        s_tr = jnp.transpose(jnp.concatenate([s, s], axis=0)).astype(BF16)

        def s_block(h, j, s_tr=s_tr):
            return keep_half(s_tr[h * dk:(h + 1) * dk], h % 2 == 0) if h // 2 == j else None

        s_parts.append(_block_matrix(SCAN_HEADS, SCAN_HEADS // 2, (dk, LANES), BF16, s_block))
    return o_intra + _dot(q_blk, jnp.concatenate(s_parts, axis=0))


def _rope(x, cos, sin_signed):
    lane = lax.broadcasted_iota(jnp.int32, (1, x.shape[1]), 1)
    first = (lane % 16) < 8
    partner = jnp.where(first, pltpu.roll(x, x.shape[1] - 8, 1), pltpu.roll(x, 8, 1))
    return x * cos + partner * sin_signed


def _scan_kernel(n_ctx_blocks,
                 gla_f_ref, gla_b_ref, hg_f_ref, hg_b_ref,
                 cos_f_ref, sin_f_ref, cos_b_ref, sin_b_ref,
                 wgk_ref, bgk_ref, lb_ref, cum_f_ref, cum_b_ref,
                 ogf_ref, ogb_ref, ohf_ref, ohb_ref,
                 sgf_ref, sgb_ref, shf_ref, shb_ref):
    del n_ctx_blocks

    @pl.when(pl.program_id(1) == 0)
    def _reset():
        for ref in (sgf_ref, sgb_ref, shf_ref, shb_ref):
            ref[...] = jnp.zeros_like(ref)

    def gla_inputs(in_ref, cos_ref, sin_ref, direction, rows):
        blk = in_ref[0, rows, :]
        v = blk[:, 0:GLA_WIDTH]
        q = blk[:, GLA_WIDTH:GLA_WIDTH + GLA_KEY]
        k = blk[:, GLA_WIDTH + GLA_KEY:GLA_WIDTH + 2 * GLA_KEY]
        gk = blk[:, GLA_WIDTH + 2 * GLA_KEY:]
        cos = cos_ref[rows, :]
        sin = sin_ref[rows, :]
        q = _rope(q, cos, sin) * (GLA_KEY // SCAN_HEADS) ** -0.5
        k = _rope(k, cos, sin)
        lo = direction * GLA_KEY
        z = _dot(gk.astype(BF16), wgk_ref[:, lo:lo + GLA_KEY]) + bgk_ref[:, lo:lo + GLA_KEY]
        log_sig = jnp.minimum(z, 0.0) - jnp.log(1.0 + jnp.exp(-jnp.abs(z)))
        return q, k, v, log_sig / GLA_GATE_NORM

    def hgrn_inputs(in_ref, direction, rows):
        blk = in_ref[0, rows, :]
        q = blk[:, 0:HGRN_WIDTH] * HEAD_DIM ** -0.5
        f_raw = blk[:, (1 + direction) * HGRN_WIDTH:(2 + direction) * HGRN_WIDTH]
        v = blk[:, 3 * HGRN_WIDTH:]
        lb = lb_ref[direction:direction + 1, :]
        f = lb + (1.0 - lb) * _sigmoid(f_raw)
        return q, 1.0 - f, v, jnp.log(f)

    jobs, dests = [], []
    for s in range(SCAN_SUB):
        rows_f = pl.ds(s * SCAN_BLOCK, SCAN_BLOCK)
        rows_b = pl.ds((SCAN_SUB - 1 - s) * SCAN_BLOCK, SCAN_BLOCK)
        jobs += [
            _scan_direction(functools.partial(gla_inputs, gla_f_ref, cos_f_ref, sin_f_ref, 0, rows_f),
                            cum_f_ref[...],sgf_ref, reverse=False),
            _scan_direction(functools.partial(hgrn_inputs, hg_f_ref, 0, rows_f),
                            cum_f_ref[...],shf_ref, reverse=False),
            _scan_direction(functools.partial(gla_inputs, gla_b_ref, cos_b_ref, sin_b_ref, 1, rows_b),
                            cum_b_ref[...],sgb_ref, reverse=True),
            _scan_direction(functools.partial(hgrn_inputs, hg_b_ref, 1, rows_b),
                            cum_b_ref[...],shb_ref, reverse=True),
        ]
        dests += [(ogf_ref, rows_f), (ohf_ref, rows_f), (ogb_ref, rows_b), (ohb_ref, rows_b)]
    for (ref, rows), val in zip(dests, _run_interleaved(jobs)):
        ref[0, rows, :] = val.astype(ref.dtype)


def _scans(gla_in, hg_in, cos_t, sin_t, wgk, bgk, lb, consts):
    n_batch, t_all, _ = gla_in.shape
    step_rows = SCAN_SUB * SCAN_BLOCK
    assert CTX_LEN % step_rows == 0 and t_all % step_rows == 0
    n_blocks = t_all // step_rows
    n_ctx_blocks = CTX_LEN // step_rows

    def fwd(i):
        return i

    def bwd(i):
        return jnp.where(i < n_ctx_blocks, n_ctx_blocks - 1 - i, n_blocks - 1 + n_ctx_blocks - i)

    def tok_spec(width, order):
        return pl.BlockSpec((1, step_rows, width), lambda b, i: (b, order(i), 0))

    def tab_spec(order):
        return pl.BlockSpec((step_rows, LANES), lambda b, i: (order(i), 0))

    def const_spec(arr):
        return pl.BlockSpec(arr.shape, lambda b, i: (0,) * arr.ndim)

    const_arrays = [wgk, bgk, lb, consts["cum_f"], consts["cum_b"]]
    out_shape = jax.ShapeDtypeStruct((n_batch, t_all, GLA_WIDTH), BF16)
    return pl.pallas_call(
        functools.partial(_scan_kernel, n_ctx_blocks),
        grid=(n_batch, n_blocks),
        in_specs=[tok_spec(W_GLA, fwd), tok_spec(W_GLA, bwd), tok_spec(W_HG, fwd), tok_spec(W_HG, bwd),
                  tab_spec(fwd), tab_spec(fwd), tab_spec(bwd), tab_spec(bwd)]
                 + [const_spec(a) for a in const_arrays],
        out_specs=[tok_spec(GLA_WIDTH, fwd), tok_spec(GLA_WIDTH, bwd),
                   tok_spec(HGRN_WIDTH, fwd), tok_spec(HGRN_WIDTH, bwd)],
        out_shape=[out_shape] * 4,
        scratch_shapes=[pltpu.VMEM((HEAD_DIM, GLA_KEY), F32), pltpu.VMEM((HEAD_DIM, GLA_KEY), F32),
                        pltpu.VMEM((HEAD_DIM, HGRN_WIDTH), F32), pltpu.VMEM((HEAD_DIM, HGRN_WIDTH), F32)],
        compiler_params=pltpu.CompilerParams(
            dimension_semantics=("parallel", "arbitrary"), vmem_limit_bytes=VMEM_LIMIT),
        name="bidirectional_scans",
    )(gla_in, gla_in, hg_in, hg_in, cos_t, sin_t, cos_t, sin_t, *const_arrays)


def _rope_tables(t_lat):
    half = GLA_KEY // SCAN_HEADS // 2
    inv_freq = 1.0 / (ROPE_BASE ** (np.arange(0, half, 2, dtype=np.float32) / half))
    t = np.arange(t_lat)
    lane = np.arange(LANES) % (2 * half)
    pos = np.where(lane[None, :] < half, (t // GRID_W)[:, None], (t % GRID_W)[:, None]).astype(np.float32)
    ang = (pos * inv_freq[(lane % half) % (half // 2)][None, :]).astype(np.float32)
    first = ((lane % half) < half // 2)[None, :]
    cos = np.cos(ang)
    sin = np.where(first, -np.sin(ang), np.sin(ang))
    cos = np.concatenate([np.ones((CTX_LEN, LANES), np.float32), cos], axis=0)
    sin = np.concatenate([np.zeros((CTX_LEN, LANES), np.float32), sin], axis=0)
    return jnp.asarray(cos, F32), jnp.asarray(sin, F32)


def _out_proj_kernel(n_streams, n_batch, tile_off, final,
                     *refs):
    x_refs = refs[:n_streams]
    (ona_ref, ogf_ref, ogb_ref, ohf_ref, ohb_ref, gate_ref, mod_ref,
     gnw_ref, hnw_ref, bd_ref, w_ref) = refs[n_streams:n_streams + 11]
    rest = refs[n_streams + 11:]
    b = pl.program_id(0)
    t = pl.program_id(1) + tile_off
    is_ctx = t < CTX_LEN // TOKEN_TILE
    if n_streams == 2:
        x = jnp.where(is_ctx, x_refs[0][0], x_refs[1][0])
    else:
        x = x_refs[0][0]
    row = jnp.where(is_ctx, n_batch, b)
    gt = mod_ref[pl.ds(row, 1), :][:, 2 * D_MODEL:]
    bd = bd_ref[...]

    def head_norm(o, w):
        sq = o * o
        hi = sq.astype(BF16)
        lo = (sq - hi.astype(F32)).astype(BF16)
        ms = _dot(hi, bd) + _dot(lo, bd)
        return o * lax.rsqrt(ms + NORM_EPS) * w

    o_gla = head_norm(ogf_ref[0].astype(F32) + ogb_ref[0].astype(F32), gnw_ref[...])
    o_hg = head_norm(ohf_ref[0].astype(F32) + ohb_ref[0].astype(F32), hnw_ref[...])
    o_cat = jnp.concatenate([ona_ref[0].astype(F32), o_gla, o_hg], axis=1)
    act = o_cat * _silu(gate_ref[0].astype(F32))
    y = _dot(act.astype(BF16), w_ref[...])
    xn = x + gt * y
    if final:
        fw_ref, out_ref = rest
        ms = jnp.mean(xn * xn, axis=-1, keepdims=True)
        out_ref[0] = xn * lax.rsqrt(ms + NORM_EPS) * fw_ref[...]
    else:
        (out_ref,) = rest
        out_ref[0] = xn


def _out_proj(streams, o_na, scans, gates, mod_l, gla_nw, hg_nw, bd, w_out_b, n_batch, t_all,
              final_w=None):
    final = final_w is not None
    n_ctx_tiles = CTX_LEN // TOKEN_TILE
    tile_off = n_ctx_tiles if final else 0
    n_steps = t_all // TOKEN_TILE - tile_off
    if len(streams) == 2:
        x_specs = [
            pl.BlockSpec((1, TOKEN_TILE, D_MODEL),
                         lambda b, t: (b, jnp.minimum(t, n_ctx_tiles - 1), 0)),
            pl.BlockSpec((1, TOKEN_TILE, D_MODEL),
                         lambda b, t: (b, jnp.maximum(t - n_ctx_tiles, 0), 0)),
        ]
    else:
        x_specs = [pl.BlockSpec((1, TOKEN_TILE, D_MODEL), lambda b, t: (b, t + tile_off, 0))]

    def tok(width):
        return pl.BlockSpec((1, TOKEN_TILE, width), lambda b, t: (b, t + tile_off, 0))

    def const_spec(arr):
        return pl.BlockSpec(arr.shape, lambda b, t: (0,) * arr.ndim)

    ona_spec = pl.BlockSpec((1, TOKEN_TILE, NA_WIDTH), lambda b, t: (b, t, 0))
    consts = [mod_l, gla_nw, hg_nw, bd, w_out_b] + ([final_w] if final else [])
    return pl.pallas_call(
        functools.partial(_out_proj_kernel, len(streams), n_batch, tile_off, final),
        grid=(n_batch, n_steps),
        in_specs=x_specs + [ona_spec] + [tok(GLA_WIDTH)] * 4 + [tok(W_GATE)]
                 + [const_spec(a) for a in consts],
        out_specs=pl.BlockSpec((1, TOKEN_TILE, D_MODEL), lambda b, t: (b, t, 0)),
        out_shape=jax.ShapeDtypeStruct((n_batch, n_steps * TOKEN_TILE, D_MODEL), F32),
        compiler_params=pltpu.CompilerParams(
            dimension_semantics=("parallel", "parallel"), vmem_limit_bytes=VMEM_LIMIT),
        name="out_proj",
    )(*streams, o_na, *scans, gates, *consts)


def _permute_w_in(w):
    w = w.astype(BF16)
    seg = [w[..., IN_OFFS[i]:IN_OFFS[i + 1]] for i in range(len(IN_SPLITS))]
    pad = jnp.zeros(w.shape[:-1] + (GK_PAD,), w.dtype)
    cols = [seg[0], seg[1], seg[2],
            seg[3], seg[8], seg[13],
            seg[6], seg[4], seg[5], seg[7], pad,
            seg[9], seg[10], seg[11], seg[12]]
    return jnp.concatenate(cols, axis=-1)


def kernel(x, c, ctx, c_ctx, ada_w, ada_b, norm_w, w_in, na_rpb, gla_w_gk, gla_b_gk, gla_norm_w,
           hgrn_lower_bounds, hgrn_norm_w, w_out, final_norm_w):
    n_batch, t_lat, _ = x.shape
    depth = ada_w.shape[0]
    t_all = CTX_LEN + t_lat
    assert ctx.shape[1] == CTX_LEN and t_lat % TOKEN_TILE == 0 and t_lat // GRID_W >= NA_KEY_ROWS

    lbs = jnp.cumsum(jax.nn.softmax(hgrn_lower_bounds.astype(F32), axis=0), axis=0)
    lbs = lbs - lbs[0:1]
    mod_rows = -(-(n_batch + 1) // 8) * 8
    cc = jnp.zeros((mod_rows, D_MODEL), F32).at[:n_batch].set(c).at[n_batch].set(c_ctx)
    mod = _modulation(cc, ada_w, ada_b)

    consts = {k_: jnp.asarray(v_, BF16) for k_, v_ in _scan_constants().items()}
    cos_t, sin_t = _rope_tables(t_lat)
    head_of = np.arange(GLA_WIDTH) // HEAD_DIM
    bd = jnp.asarray((head_of[:, None] == head_of[None, :]).astype(np.float32) / HEAD_DIM, BF16)

    w_in_p = _permute_w_in(w_in)
    w_out_b = w_out.astype(BF16)
    rpb_tiles = _na_bias_tiles(na_rpb)
    wgk = jnp.zeros((depth, LANES, 2 * GLA_KEY), F32)
    wgk = wgk.at[:, 0:GLA_GK_RANK, 0:GLA_KEY].set(gla_w_gk[:, 0])
    wgk = wgk.at[:, GLA_GK_RANK:2 * GLA_GK_RANK, GLA_KEY:].set(gla_w_gk[:, 1]).astype(BF16)

    streams = [ctx, x]
    out = None
    for layer in range(depth):
        last = layer == depth - 1
        bgk = gla_b_gk[layer].reshape(1, 2 * GLA_KEY)
        gla_nw = jnp.tile(gla_norm_w[layer], SCAN_HEADS).reshape(1, GLA_WIDTH)
        hg_nw = jnp.tile(hgrn_norm_w[layer], SCAN_HEADS).reshape(1, HGRN_WIDTH)

        q, k, v, gates, gla_in, hg_in = _in_proj(
            streams, mod[layer], norm_w[layer].reshape(1, D_MODEL), w_in_p, layer, n_batch, t_all)
        o_na = _neighborhood_attention(q, k, v, rpb_tiles, layer, with_ctx_queries=not last)
        scans = _scans(gla_in, hg_in, cos_t, sin_t, wgk[layer], bgk, lbs[layer], consts)
        out = _out_proj(streams, o_na, scans, gates, mod[layer], gla_nw, hg_nw, bd,
                        w_out_b[layer], n_batch, t_all,
                        final_w=final_norm_w.reshape(1, D_MODEL) if last else None)
        streams = [out]
    return out
```

```python
import functools

import numpy as np
import jax
import jax.numpy as jnp
from jax import lax
from jax.experimental import pallas as pl
from jax.experimental.pallas import tpu as pltpu

F32 = jnp.float32
BF16 = jnp.bfloat16

D_MODEL = 1024
CTX_LEN = 256
GRID_W = 64
NORM_EPS = 1e-6
HEAD_DIM = 64
NA_WIDTH = 512
NA_HEADS = 8
NA_WIN_H = 8
NA_WIN_W = 16
GLA_WIDTH = 256
GLA_KEY = 128
GLA_GK_RANK = 16
GLA_GATE_NORM = 16.0
HGRN_WIDTH = 256
SCAN_HEADS = 4
CHUNK = 16
ROPE_BASE = 10000.0

LOG2_E = float(np.log2(np.e))
NA_Q_SCALE = HEAD_DIM ** -0.5 * LOG2_E

LANES = 128
TOKEN_TILE = 256
SCAN_BLOCK = 128
SCAN_SUB = 2
CHUNKS_PER_BLOCK = SCAN_BLOCK // CHUNK
NA_Q_ROWS = TOKEN_TILE // GRID_W
NA_KEY_ROWS = NA_Q_ROWS + NA_WIN_H
NA_KEYS = NA_KEY_ROWS * GRID_W
assert NA_KEYS % LANES == 0 and (NA_Q_ROWS * GRID_W) % LANES == 0 and NA_WIN_H % 4 == 0
VMEM_LIMIT = 56 * 1024 * 1024

IN_SPLITS = (512, 512, 512, 512, 128, 128, 256, 32, 256, 256, 256, 256, 256, 256)
IN_OFFS = tuple(int(v) for v in np.cumsum((0,) + IN_SPLITS))
GK_PAD = LANES - 2 * GLA_GK_RANK
W_NA = 3 * NA_WIDTH
W_GATE = NA_WIDTH + GLA_WIDTH + HGRN_WIDTH
W_GLA = GLA_WIDTH + 2 * GLA_KEY + LANES
W_HG = 4 * HGRN_WIDTH
W_IN_PAD = W_NA + W_GATE + W_GLA + W_HG


def _dot(a, b):
    return jnp.dot(a, b, preferred_element_type=F32)


def _dot_nt(a, b):
    return lax.dot_general(a, b, (((1,), (1,)), ((), ())), preferred_element_type=F32)


def _dot_tn(a, b):
    return lax.dot_general(a, b, (((0,), (0,)), ((), ())), preferred_element_type=F32)


def _sigmoid(x):
    return 1.0 / (1.0 + jnp.exp(-x))


def _silu(x):
    return x * _sigmoid(x)


def _mod_kernel(c_ref, w_ref, b_ref, o_ref):
    s = _silu(c_ref[...]).astype(BF16)
    o_ref[0] = _dot(s, w_ref[0].astype(BF16)) + b_ref[0]


def _modulation(cc, ada_w, ada_b):
    depth = ada_w.shape[0]
    rows = cc.shape[0]
    return pl.pallas_call(
        _mod_kernel,
        grid=(depth, 3),
        in_specs=[
            pl.BlockSpec((rows, D_MODEL), lambda l, j: (0, 0)),
            pl.BlockSpec((1, D_MODEL, D_MODEL), lambda l, j: (l, 0, j)),
            pl.BlockSpec((1, 1, D_MODEL), lambda l, j: (l, 0, j)),
        ],
        out_specs=pl.BlockSpec((1, rows, D_MODEL), lambda l, j: (l, 0, j)),
        out_shape=jax.ShapeDtypeStruct((depth, rows, 3 * D_MODEL), F32),
        compiler_params=pltpu.CompilerParams(
            dimension_semantics=("parallel", "parallel"), vmem_limit_bytes=VMEM_LIMIT),
        name="adaln_mod",
    )(cc, ada_w, ada_b.reshape(depth, 1, 3 * D_MODEL))


def _token_tile(x_refs, n_batch, tile_off=0):
    t = pl.program_id(1) + tile_off
    is_ctx = t < CTX_LEN // TOKEN_TILE
    x = jnp.where(is_ctx, x_refs[0][0], x_refs[1][0]) if len(x_refs) == 2 else x_refs[0][0]
    return x, jnp.where(is_ctx, n_batch, pl.program_id(0))


def _stream_specs(n_streams, tile_off=0):
    n_ctx_tiles = CTX_LEN // TOKEN_TILE
    if n_streams == 2:
        return [pl.BlockSpec((1, TOKEN_TILE, D_MODEL), lambda b, t: (b, jnp.minimum(t, n_ctx_tiles - 1), 0)),
                pl.BlockSpec((1, TOKEN_TILE, D_MODEL), lambda b, t: (b, jnp.maximum(t - n_ctx_tiles, 0), 0))]
    return [pl.BlockSpec((1, TOKEN_TILE, D_MODEL), lambda b, t: (b, t + tile_off, 0))]


def _in_proj_kernel(n_streams, n_batch, *refs):
    x, row = _token_tile(refs[:n_streams], n_batch)
    _project_tile(x, row, *refs[n_streams:])


def _project_tile(x, row, mod_ref, nw_ref, w_ref, q_ref, k_ref, v_ref, gate_ref, gla_ref, hg_ref):
    m = mod_ref[pl.ds(row, 1), :]
    sh = m[:, :D_MODEL]
    sc = m[:, D_MODEL:2 * D_MODEL]
    ms = jnp.mean(x * x, axis=-1, keepdims=True)
    y = x * lax.rsqrt(ms + NORM_EPS) * nw_ref[...]
    h = y * (1.0 + sc) + sh
    p = _dot(h.astype(BF16), w_ref[...])
    for pr in range(NA_HEADS // 2):
        lo = pr * LANES
        q_ref[0, pr] = (p[:, lo:lo + LANES] * NA_Q_SCALE).astype(BF16)
        k_ref[0, pr] = p[:, NA_WIDTH + lo:NA_WIDTH + lo + LANES].astype(BF16)
        for s in range(TOKEN_TILE // LANES):
            v_tile = p[s * LANES:(s + 1) * LANES, 2 * NA_WIDTH + lo:2 * NA_WIDTH + lo + LANES]
            v_ref[0, pr, s] = jnp.transpose(v_tile).astype(BF16)
    gate_ref[0] = p[:, W_NA:W_NA + W_GATE].astype(BF16)
    gla_ref[0] = p[:, W_NA + W_GATE:W_NA + W_GATE + W_GLA]
    hg_ref[0] = p[:, W_NA + W_GATE + W_GLA:]


def _projection_io(mod_l, layer, n_batch, t_all):
    in_specs = [
        pl.BlockSpec(mod_l.shape, lambda b, t: (0, 0)),
        pl.BlockSpec((1, D_MODEL), lambda b, t: (0, 0)),
        pl.BlockSpec((None, D_MODEL, W_IN_PAD), lambda b, t: (layer, 0, 0), pipeline_mode=pl.Buffered(1)),
    ]
    pair_spec = pl.BlockSpec((1, NA_HEADS // 2, TOKEN_TILE, LANES), lambda b, t: (b, 0, t, 0))
    pair_shape = jax.ShapeDtypeStruct((n_batch, NA_HEADS // 2, t_all, LANES), BF16)
    vt_spec = pl.BlockSpec((1, NA_HEADS // 2, TOKEN_TILE // LANES, LANES, LANES), lambda b, t: (b, 0, t, 0, 0))
    vt_shape = jax.ShapeDtypeStruct((n_batch, NA_HEADS // 2, t_all // LANES, LANES, LANES), BF16)

    def slab(width, dtype):
        return (pl.BlockSpec((1, TOKEN_TILE, width), lambda b, t: (b, t, 0)),
                jax.ShapeDtypeStruct((n_batch, t_all, width), dtype))

    gate_spec, gate_shape = slab(W_GATE, BF16)
    gla_spec, gla_shape = slab(W_GLA, F32)
    hg_spec, hg_shape = slab(W_HG, F32)
    return (in_specs, [pair_spec, pair_spec, vt_spec, gate_spec, gla_spec, hg_spec],
            [pair_shape, pair_shape, vt_shape, gate_shape, gla_shape, hg_shape])


def _in_proj(streams, mod_l, norm_w_l, w_in_p, layer, n_batch, t_all):
    in_specs, out_specs, out_shape = _projection_io(mod_l, layer, n_batch, t_all)
    return pl.pallas_call(
        functools.partial(_in_proj_kernel, len(streams), n_batch),
        grid=(n_batch, t_all // TOKEN_TILE),
        in_specs=_stream_specs(len(streams)) + in_specs,
        out_specs=out_specs,
        out_shape=out_shape,
        compiler_params=pltpu.CompilerParams(
            dimension_semantics=("parallel", "parallel"), vmem_limit_bytes=VMEM_LIMIT),
        name="in_proj",
    )(*streams, mod_l, norm_w_l, w_in_p)


def _na_bias_cases():
    cases = []
    for key_row_ok, rel in (
        (lambda lr, lk: lk < NA_WIN_H, lambda lr, lk: lk - lr + NA_WIN_H - 1),
        (lambda lr, lk: lr <= lk < lr + NA_WIN_H, lambda lr, lk: lk - lr + NA_WIN_H // 2 - 1),
        (lambda lr, lk: lk >= NA_KEY_ROWS - NA_WIN_H, lambda lr, lk: lk - lr - 1),
    ):
        cases.append([[rel(lr, lk) if key_row_ok(lr, lk) else None for lk in range(NA_KEY_ROWS)]
                      for lr in range(NA_Q_ROWS)])
    return cases


def _na_kernel(tile_off, q_ref, k_ref, vt_ref, rt_ref, o_ref, bias_ref):
    tb = pl.program_id(1) + tile_off
    n_pairs = q_ref.shape[1]
    lane = lax.broadcasted_iota(jnp.int32, (1, LANES), 1)
    q_scale = [jnp.where((lane // HEAD_DIM) == hh, 1.0, 0.0).astype(BF16) for hh in range(2)]
    heads = [(p, hh) for p in range(n_pairs) for hh in range(2)]
    ctx_tiles = CTX_LEN // LANES

    @pl.when((pl.program_id(0) == 0) & (pl.program_id(1) == 0))
    def _build_bias_tables():
        minus_inf = jnp.full((GRID_W, GRID_W), -jnp.inf, F32)

        def per_head(h, carry):
            for case, rel in enumerate(_na_bias_cases()):
                for lk in range(NA_KEY_ROWS):
                    for lr in range(0, NA_Q_ROWS, 2):
                        tiles = [minus_inf if rel[r][lk] is None else rt_ref[h, rel[r][lk]] for r in (lr, lr + 1)]
                        bias_ref[case, h, lk * GRID_W:(lk + 1) * GRID_W, lr * GRID_W:(lr + 2) * GRID_W] = (
                            jnp.concatenate(tiles, axis=1))
            return carry

        lax.fori_loop(0, NA_HEADS, per_head, 0)

    def vt_tiles(p, first, count):
        tiles = vt_ref[0, p, pl.ds(first, count)]
        return jnp.concatenate([tiles[i] for i in range(count)], axis=1)

    def finish(outs):
        row = lax.broadcasted_iota(jnp.int32, (LANES, 1), 0)
        for p in range(n_pairs):
            o_t = jnp.where(row < HEAD_DIM, outs[(p, 0)], outs[(p, 1)])
            o_ref[0, :, p * LANES:(p + 1) * LANES] = jnp.transpose(o_t).astype(o_ref.dtype)

    def pipelined(scores_of, attend):
        outs, pending = {}, None
        for i in range(len(heads) + 1):
            issued = scores_of(*heads[i]) if i < len(heads) else None
            if pending is not None:
                outs[heads[i - 1]] = attend(heads[i - 1][0], *pending)
            pending = issued
        finish(outs)

    def softmax_pv(vt_parts, s_parts):
        m = functools.reduce(jnp.maximum, [jnp.max(s, axis=0, keepdims=True) for s in s_parts])
        e_parts = [jnp.exp2(s - m) for s in s_parts]
        l = functools.reduce(jnp.add, [jnp.sum(e, axis=0, keepdims=True) for e in e_parts])
        o_t = functools.reduce(jnp.add, [_dot(vt, e.astype(BF16)) for vt, e in zip(vt_parts, e_parts)])
        return o_t / l

    @pl.when(tb == 0)
    def _ctx_queries():
        def scores_of(p, hh):
            return (_dot_nt(k_ref[0, p, 0:CTX_LEN, :], q_ref[0, p] * q_scale[hh]),)

        def attend(p, s):
            return softmax_pv([vt_tiles(p, 0, ctx_tiles)], [s])

        pipelined(scores_of, attend)

    @pl.when(tb > 0)
    def _latent_queries():
        g = tb - 1
        n_groups = pl.num_programs(1) - 1 + tile_off
        ws = jnp.clip(NA_Q_ROWS * g - NA_WIN_H // 2, 0, GRID_W - NA_KEY_ROWS)
        case = (g > 0).astype(jnp.int32) + (g == n_groups - 1).astype(jnp.int32)
        start = pl.multiple_of(CTX_LEN + ws * GRID_W, LANES)
        first_tile = ctx_tiles + ws * GRID_W // LANES

        def scores_of(p, hh):
            qh = q_ref[0, p] * q_scale[hh]
            s_nb = _dot_nt(k_ref[0, p, pl.ds(start, NA_KEYS), :], qh) + bias_ref[case, 2 * p + hh]
            return s_nb, _dot_nt(k_ref[0, p, 0:CTX_LEN, :], qh)

        def attend(p, s_nb, s_cx):
            return softmax_pv([vt_tiles(p, first_tile, NA_KEYS // LANES), vt_tiles(p, 0, ctx_tiles)],
                              [s_nb, s_cx])

        pipelined(scores_of, attend)


def _neighborhood_attention(q, k, vt, rpb_tiles, layer, with_ctx_queries):
    n_batch, n_pairs, t_all, _ = q.shape
    n_tiles = t_all // TOKEN_TILE
    tile_off = 0 if with_ctx_queries else CTX_LEN // TOKEN_TILE
    n_steps = n_tiles - tile_off
    t_out = n_steps * TOKEN_TILE
    return pl.pallas_call(
        functools.partial(_na_kernel, tile_off),
        grid=(n_batch, n_steps),
        in_specs=[
            pl.BlockSpec((1, n_pairs, TOKEN_TILE, LANES), lambda b, g: (b, 0, g + tile_off, 0)),
            pl.BlockSpec((1, n_pairs, t_all, LANES), lambda b, g: (b, 0, 0, 0)),
            pl.BlockSpec((1, n_pairs, t_all // LANES, LANES, LANES), lambda b, g: (b, 0, 0, 0, 0)),
            pl.BlockSpec((None,) + rpb_tiles.shape[1:], lambda b, g: (layer, 0, 0, 0, 0),
                         pipeline_mode=pl.Buffered(1)),
        ],
        out_specs=pl.BlockSpec((1, TOKEN_TILE, NA_WIDTH), lambda b, g: (b, g, 0)),
        out_shape=jax.ShapeDtypeStruct((n_batch, t_out, NA_WIDTH), BF16),
        scratch_shapes=[pltpu.VMEM((3, NA_HEADS, NA_KEYS, TOKEN_TILE), F32)],
        compiler_params=pltpu.CompilerParams(
            dimension_semantics=("arbitrary", "arbitrary"),
            vmem_limit_bytes=VMEM_LIMIT),
        name="neighborhood_attention",
    )(q, k, vt, rpb_tiles)


def _na_bias_tiles(rpb):
    col = np.arange(GRID_W)
    col_start = np.clip(col - NA_WIN_W // 2, 0, GRID_W - NA_WIN_W)
    col_in = (col[:, None] >= col_start[None, :]) & (col[:, None] < col_start[None, :] + NA_WIN_W)
    col_idx = np.clip(col[:, None] - col[None, :], 1 - NA_WIN_W, NA_WIN_W - 1) + NA_WIN_W - 1
    one_hot = (col_idx[None] == np.arange(2 * NA_WIN_W - 1)[:, None, None]).astype(np.float32)
    tiles = jnp.einsum("lhrw,wkj->lhrkj", rpb.astype(F32), jnp.asarray(one_hot),
                       precision=lax.Precision.HIGHEST)
    return jnp.where(col_in, tiles * LOG2_E, -jnp.inf)


def _scan_constants():
    idx = np.arange(SCAN_BLOCK)
    c = idx // CHUNK
    pos = idx % CHUNK
    same = c[:, None] == c[None, :]
    return {"cum_f": (same & (pos[None, :] <= pos[:, None])).astype(np.float32),
            "cum_b": (same & (pos[None, :] >= pos[:, None])).astype(np.float32)}


def _block_matrix(n_rows, n_cols, block_shape, dtype, block_at):
    zero = jnp.zeros(block_shape, dtype)
    rows = []
    for r in range(n_rows):
        blocks = [block_at(r, c) for c in range(n_cols)]
        rows.append(jnp.concatenate([zero if b is None else b for b in blocks], axis=1))
    return jnp.concatenate(rows, axis=0)


def _run_interleaved(stagewise):
    results = [None] * len(stagewise)
    live = list(range(len(stagewise)))
    while live:
        for i in list(live):
            try:
                next(stagewise[i])
            except StopIteration as done:
                results[i] = done.value
                live.remove(i)
    return results


def _scan_direction(load_inputs, cum, st_ref, reverse):
    q, k, v, lf = load_inputs()
    n = SCAN_BLOCK
    hd = q.shape[1]
    dk = hd // SCAN_HEADS
    pack = 2
    low_half = lax.broadcasted_iota(jnp.int32, (1, LANES), 1) < HEAD_DIM
    lane_k = lax.broadcasted_iota(jnp.int32, (1, hd), 1) // dk
    lane_v = lax.broadcasted_iota(jnp.int32, (1, SCAN_HEADS * HEAD_DIM), 1) // HEAD_DIM

    def lane_mask(cond):
        return jnp.where(cond, 1.0, 0.0).astype(BF16)

    half_masks = {True: lane_mask(low_half), False: lane_mask(jnp.logical_not(low_half))}
    k_masks = [lane_mask(lane_k == h) for h in range(SCAN_HEADS)]
    v_masks = [lane_mask(lane_v == h) for h in range(SCAN_HEADS)]

    def keep_half(x, low):
        return x * half_masks[low]

    hi = lf.astype(BF16)
    mid = (lf - hi.astype(F32)).astype(BF16)
    g = _dot(cum, hi) + _dot(cum, mid)
    yield
    ref_pos, last_pos = (CHUNK // 2, 0) if reverse else (CHUNK // 2 - 1, CHUNK - 1)
    to_ref, to_last, decay = [], [], []
    for c in range(CHUNKS_PER_BLOCK):
        gc = g[c * CHUNK:(c + 1) * CHUNK]
        g_last = gc[last_pos:last_pos + 1]
        to_ref.append(gc - gc[ref_pos:ref_pos + 1])
        to_last.append(g_last - gc)
        decay.append(jnp.exp(g_last))
    x_ref = jnp.concatenate(to_ref, axis=0)
    a = jnp.exp(x_ref)
    a_inv = jnp.exp(-x_ref)
    e_in = jnp.exp(g)
    e_out = jnp.exp(jnp.concatenate(to_last, axis=0))

    q_t = (q * a).astype(BF16)
    k_t_tr = jnp.transpose(k * a_inv).astype(BF16)
    q_in = (q * e_in).astype(BF16)
    k_out = (k * e_out).astype(BF16)
    v_b = v.astype(BF16)

    k_blk = _block_matrix(SCAN_HEADS, SCAN_HEADS, (dk, n), BF16,
                          lambda r, c: k_t_tr[r * dk:(r + 1) * dk] if r == c else None)
    scores = _dot(q_t, k_blk)

    v_tr = jnp.transpose(v).astype(BF16)
    token_pair = lax.broadcasted_iota(jnp.int32, (1, n), 1) // (CHUNK * pack)
    pair_masks = [lane_mask(token_pair == m) for m in range(CHUNKS_PER_BLOCK // pack)]
    heads_per_group = LANES // dk
    incs = []
    for grp in range(hd // LANES):
        heads = range(grp * heads_per_group, (grp + 1) * heads_per_group)
        v_blk = jnp.concatenate(
            [jnp.concatenate([v_tr[h * HEAD_DIM:(h + 1) * HEAD_DIM] * pm for h in heads], axis=1)
             for pm in pair_masks], axis=0)
        k_rows = []
        for h in heads:
            k_h = (k_out * k_masks[h])[:, grp * LANES:(grp + 1) * LANES]
            k_rows.append(_block_matrix(
                CHUNKS_PER_BLOCK, pack, (CHUNK, LANES), BF16,
                lambda c, par, k_h=k_h: k_h[c * CHUNK:(c + 1) * CHUNK] if c % pack == par else None))
        incs.append(_dot(v_blk, jnp.concatenate(k_rows, axis=0)))
    yield

    tri = jnp.concatenate([cum.astype(F32)] * SCAN_HEADS, axis=1) > 0.5
    p = jnp.where(tri, scores, 0.0).astype(BF16)
    v_stack = jnp.concatenate([v_b * v_masks[h] for h in range(SCAN_HEADS)], axis=0)
    o_intra = _dot(p, v_stack)

    st = st_ref[...]
    states = [None] * CHUNKS_PER_BLOCK
    order = range(CHUNKS_PER_BLOCK - 1, -1, -1) if reverse else range(CHUNKS_PER_BLOCK)
    for c in order:
        states[c] = st
        m, par = divmod(c, pack)
        inc_c = jnp.concatenate([inc[m * HEAD_DIM:(m + 1) * HEAD_DIM, par * LANES:(par + 1) * LANES]
                                 for inc in incs], axis=1)
        st = decay[c] * st + inc_c
    st_ref[...] = st

    q_blk = _block_matrix(CHUNKS_PER_BLOCK, CHUNKS_PER_BLOCK, (CHUNK, hd), BF16,
                          lambda r, c: q_in[r * CHUNK:(r + 1) * CHUNK] if r == c else None)
    s_parts = []
    for s in states:
        s_tr = jnp.transpose(jnp.concatenate([s, s], axis=0)).astype(BF16)

        def s_block(h, j, s_tr=s_tr):
            return keep_half(s_tr[h * dk:(h + 1) * dk], h % 2 == 0) if h // 2 == j else None

        s_parts.append(_block_matrix(SCAN_HEADS, SCAN_HEADS // 2, (dk, LANES), BF16, s_block))
    return o_intra + _dot(q_blk, jnp.concatenate(s_parts, axis=0))


def _rope(x, cos, sin_signed):
    lane = lax.broadcasted_iota(jnp.int32, (1, x.shape[1]), 1)
    first = (lane % 16) < 8
    partner = jnp.where(first, pltpu.roll(x, x.shape[1] - 8, 1), pltpu.roll(x, 8, 1))
    return x * cos + partner * sin_signed


def _scan_kernel(n_ctx_blocks,
                 gla_f_ref, gla_b_ref, hg_f_ref, hg_b_ref,
                 cos_f_ref, sin_f_ref, cos_b_ref, sin_b_ref,
                 wgk_ref, bgk_ref, lb_ref, cum_f_ref, cum_b_ref,
                 ogf_ref, ogb_ref, ohf_ref, ohb_ref,
                 sgf_ref, sgb_ref, shf_ref, shb_ref):
    del n_ctx_blocks

    @pl.when(pl.program_id(1) == 0)
    def _reset():
        for ref in (sgf_ref, sgb_ref, shf_ref, shb_ref):
            ref[...] = jnp.zeros_like(ref)

    def gla_inputs(in_ref, cos_ref, sin_ref, direction, rows):
        blk = in_ref[0, rows, :]
        v = blk[:, 0:GLA_WIDTH]
        q = blk[:, GLA_WIDTH:GLA_WIDTH + GLA_KEY]
        k = blk[:, GLA_WIDTH + GLA_KEY:GLA_WIDTH + 2 * GLA_KEY]
        gk = blk[:, GLA_WIDTH + 2 * GLA_KEY:]
        cos = cos_ref[rows, :]
        sin = sin_ref[rows, :]
        q = _rope(q, cos, sin) * (GLA_KEY // SCAN_HEADS) ** -0.5
        k = _rope(k, cos, sin)
        lo = direction * GLA_KEY
        z = _dot(gk.astype(BF16), wgk_ref[:, lo:lo + GLA_KEY]) + bgk_ref[:, lo:lo + GLA_KEY]
        log_sig = jnp.minimum(z, 0.0) - jnp.log(1.0 + jnp.exp(-jnp.abs(z)))
        return q, k, v, log_sig / GLA_GATE_NORM

    def hgrn_inputs(in_ref, direction, rows):
        blk = in_ref[0, rows, :]
        q = blk[:, 0:HGRN_WIDTH] * HEAD_DIM ** -0.5
        f_raw = blk[:, (1 + direction) * HGRN_WIDTH:(2 + direction) * HGRN_WIDTH]
        v = blk[:, 3 * HGRN_WIDTH:]
        lb = lb_ref[direction:direction + 1, :]
        f = lb + (1.0 - lb) * _sigmoid(f_raw)
        return q, 1.0 - f, v, jnp.log(f)

    jobs, dests = [], []
    for s in range(SCAN_SUB):
        rows_f = pl.ds(s * SCAN_BLOCK, SCAN_BLOCK)
        rows_b = pl.ds((SCAN_SUB - 1 - s) * SCAN_BLOCK, SCAN_BLOCK)
        jobs += [
            _scan_direction(functools.partial(gla_inputs, gla_f_ref, cos_f_ref, sin_f_ref, 0, rows_f),
                            cum_f_ref[...],sgf_ref, reverse=False),
            _scan_direction(functools.partial(hgrn_inputs, hg_f_ref, 0, rows_f),
                            cum_f_ref[...],shf_ref, reverse=False),
            _scan_direction(functools.partial(gla_inputs, gla_b_ref, cos_b_ref, sin_b_ref, 1, rows_b),
                            cum_b_ref[...],sgb_ref, reverse=True),
            _scan_direction(functools.partial(hgrn_inputs, hg_b_ref, 1, rows_b),
                            cum_b_ref[...],shb_ref, reverse=True),
        ]
        dests += [(ogf_ref, rows_f), (ohf_ref, rows_f), (ogb_ref, rows_b), (ohb_ref, rows_b)]
    for (ref, rows), val in zip(dests, _run_interleaved(jobs)):
        ref[0, rows, :] = val.astype(ref.dtype)


def _scans(gla_in, hg_in, cos_t, sin_t, wgk, bgk, lb, consts):
    n_batch, t_all, _ = gla_in.shape
    step_rows = SCAN_SUB * SCAN_BLOCK
    assert CTX_LEN % step_rows == 0 and t_all % step_rows == 0
    n_blocks = t_all // step_rows
    n_ctx_blocks = CTX_LEN // step_rows

    def fwd(i):
        return i

    def bwd(i):
        return jnp.where(i < n_ctx_blocks, n_ctx_blocks - 1 - i, n_blocks - 1 + n_ctx_blocks - i)

    def tok_spec(width, order):
        return pl.BlockSpec((1, step_rows, width), lambda b, i: (b, order(i), 0))

    def tab_spec(order):
        return pl.BlockSpec((step_rows, LANES), lambda b, i: (order(i), 0))

    def const_spec(arr):
        return pl.BlockSpec(arr.shape, lambda b, i: (0,) * arr.ndim)

    const_arrays = [wgk, bgk, lb, consts["cum_f"], consts["cum_b"]]
    out_shape = jax.ShapeDtypeStruct((n_batch, t_all, GLA_WIDTH), BF16)
    return pl.pallas_call(
        functools.partial(_scan_kernel, n_ctx_blocks),
        grid=(n_batch, n_blocks),
        in_specs=[tok_spec(W_GLA, fwd), tok_spec(W_GLA, bwd), tok_spec(W_HG, fwd), tok_spec(W_HG, bwd),
                  tab_spec(fwd), tab_spec(fwd), tab_spec(bwd), tab_spec(bwd)]
                 + [const_spec(a) for a in const_arrays],
        out_specs=[tok_spec(GLA_WIDTH, fwd), tok_spec(GLA_WIDTH, bwd),
                   tok_spec(HGRN_WIDTH, fwd), tok_spec(HGRN_WIDTH, bwd)],
        out_shape=[out_shape] * 4,
        scratch_shapes=[pltpu.VMEM((HEAD_DIM, GLA_KEY), F32), pltpu.VMEM((HEAD_DIM, GLA_KEY), F32),
                        pltpu.VMEM((HEAD_DIM, HGRN_WIDTH), F32), pltpu.VMEM((HEAD_DIM, HGRN_WIDTH), F32)],
        compiler_params=pltpu.CompilerParams(
            dimension_semantics=("parallel", "arbitrary"), vmem_limit_bytes=VMEM_LIMIT),
        name="bidirectional_scans",
    )(gla_in, gla_in, hg_in, hg_in, cos_t, sin_t, cos_t, sin_t, *const_arrays)


def _rope_tables(t_lat):
    half = GLA_KEY // SCAN_HEADS // 2
    inv_freq = 1.0 / (ROPE_BASE ** (np.arange(0, half, 2, dtype=np.float32) / half))
    t = np.arange(t_lat)
    lane = np.arange(LANES) % (2 * half)
    pos = np.where(lane[None, :] < half, (t // GRID_W)[:, None], (t % GRID_W)[:, None]).astype(np.float32)
    ang = (pos * inv_freq[(lane % half) % (half // 2)][None, :]).astype(np.float32)
    first = ((lane % half) < half // 2)[None, :]
    cos = np.cos(ang)
    sin = np.where(first, -np.sin(ang), np.sin(ang))
    cos = np.concatenate([np.ones((CTX_LEN, LANES), np.float32), cos], axis=0)
    sin = np.concatenate([np.zeros((CTX_LEN, LANES), np.float32), sin], axis=0)
    return jnp.asarray(cos, F32), jnp.asarray(sin, F32)


N_MIX_REFS = 11


def _mix_tile(x, row, ona_ref, ogf_ref, ogb_ref, ohf_ref, ohb_ref, gate_ref, mod_ref,
              gnw_ref, hnw_ref, bd_ref, w_ref):
    gt = mod_ref[pl.ds(row, 1), :][:, 2 * D_MODEL:]
    bd = bd_ref[...]

    def head_norm(o, w):
        sq = o * o
        hi = sq.astype(BF16)
        lo = (sq - hi.astype(F32)).astype(BF16)
        ms = _dot(hi, bd) + _dot(lo, bd)
        return o * lax.rsqrt(ms + NORM_EPS) * w

    o_gla = head_norm(ogf_ref[0].astype(F32) + ogb_ref[0].astype(F32), gnw_ref[...])
    o_hg = head_norm(ohf_ref[0].astype(F32) + ohb_ref[0].astype(F32), hnw_ref[...])
    o_cat = jnp.concatenate([ona_ref[0].astype(F32), o_gla, o_hg], axis=1)
    act = o_cat * _silu(gate_ref[0].astype(F32))
    y = _dot(act.astype(BF16), w_ref[...])
    return x + gt * y


def _final_out_kernel(n_streams, n_batch, tile_off, *refs):
    x, row = _token_tile(refs[:n_streams], n_batch, tile_off)
    xn = _mix_tile(x, row, *refs[n_streams:n_streams + N_MIX_REFS])
    fw_ref, out_ref = refs[n_streams + N_MIX_REFS:]
    ms = jnp.mean(xn * xn, axis=-1, keepdims=True)
    out_ref[0] = xn * lax.rsqrt(ms + NORM_EPS) * fw_ref[...]


def _out_in_kernel(n_streams, n_batch, *refs):
    x, row = _token_tile(refs[:n_streams], n_batch)
    xn = _mix_tile(x, row, *refs[n_streams:n_streams + N_MIX_REFS])
    rest = refs[n_streams + N_MIX_REFS:]
    proj_in, stream_ref, proj_out = rest[:3], rest[3], rest[4:]
    stream_ref[0] = xn
    _project_tile(xn, row, *proj_in, *proj_out)


def _mix_io(o_na, scans, gates, mod_l, gla_nw, hg_nw, bd, w_out_b, tile_off):
    def tok(width):
        return pl.BlockSpec((1, TOKEN_TILE, width), lambda b, t: (b, t + tile_off, 0))

    def const_spec(arr):
        return pl.BlockSpec(arr.shape, lambda b, t: (0,) * arr.ndim)

    consts = [mod_l, gla_nw, hg_nw, bd, w_out_b]
    specs = ([pl.BlockSpec((1, TOKEN_TILE, NA_WIDTH), lambda b, t: (b, t, 0))] + [tok(GLA_WIDTH)] * 4
             + [tok(W_GATE)] + [const_spec(a) for a in consts])
    return specs, [o_na, *scans, gates, *consts]


def _final_out(streams, mix_args, final_w, n_batch, t_all):
    tile_off = CTX_LEN // TOKEN_TILE
    n_steps = t_all // TOKEN_TILE - tile_off
    mix_specs, mix_arrays = _mix_io(*mix_args, tile_off)
    return pl.pallas_call(
        functools.partial(_final_out_kernel, len(streams), n_batch, tile_off),
        grid=(n_batch, n_steps),
        in_specs=_stream_specs(len(streams), tile_off) + mix_specs
                 + [pl.BlockSpec((1, D_MODEL), lambda b, t: (0, 0))],
        out_specs=pl.BlockSpec((1, TOKEN_TILE, D_MODEL), lambda b, t: (b, t, 0)),
        out_shape=jax.ShapeDtypeStruct((n_batch, n_steps * TOKEN_TILE, D_MODEL), F32),
        compiler_params=pltpu.CompilerParams(
            dimension_semantics=("parallel", "parallel"), vmem_limit_bytes=VMEM_LIMIT),
        name="out_proj_final",
    )(*streams, *mix_arrays, final_w)


def _out_in_proj(streams, mix_args, mod_next, norm_w_next, w_in_p, next_layer, n_batch, t_all):
    mix_specs, mix_arrays = _mix_io(*mix_args, 0)
    proj_specs, out_specs, out_shape = _projection_io(mod_next, next_layer, n_batch, t_all)
    stream_spec = pl.BlockSpec((1, TOKEN_TILE, D_MODEL), lambda b, t: (b, t, 0))
    stream_shape = jax.ShapeDtypeStruct((n_batch, t_all, D_MODEL), F32)
    outs = pl.pallas_call(
        functools.partial(_out_in_kernel, len(streams), n_batch),
        grid=(n_batch, t_all // TOKEN_TILE),
        in_specs=_stream_specs(len(streams)) + mix_specs + proj_specs,
        out_specs=[stream_spec] + out_specs,
        out_shape=[stream_shape] + out_shape,
        compiler_params=pltpu.CompilerParams(
            dimension_semantics=("parallel", "parallel"), vmem_limit_bytes=VMEM_LIMIT),
        name="out_in_proj",
    )(*streams, *mix_arrays, mod_next, norm_w_next, w_in_p)
    return outs[0], outs[1:]


def _permute_w_in(w):
    w = w.astype(BF16)
    seg = [w[..., IN_OFFS[i]:IN_OFFS[i + 1]] for i in range(len(IN_SPLITS))]
    pad = jnp.zeros(w.shape[:-1] + (GK_PAD,), w.dtype)
    cols = [seg[0], seg[1], seg[2],
            seg[3], seg[8], seg[13],
            seg[6], seg[4], seg[5], seg[7], pad,
            seg[9], seg[10], seg[11], seg[12]]
    return jnp.concatenate(cols, axis=-1)


def kernel(x, c, ctx, c_ctx, ada_w, ada_b, norm_w, w_in, na_rpb, gla_w_gk, gla_b_gk, gla_norm_w,
           hgrn_lower_bounds, hgrn_norm_w, w_out, final_norm_w):
    n_batch, t_lat, _ = x.shape
    depth = ada_w.shape[0]
    t_all = CTX_LEN + t_lat
    assert ctx.shape[1] == CTX_LEN and t_lat % TOKEN_TILE == 0 and t_lat // GRID_W >= NA_KEY_ROWS

    lbs = jnp.cumsum(jax.nn.softmax(hgrn_lower_bounds.astype(F32), axis=0), axis=0)
    lbs = lbs - lbs[0:1]
    mod_rows = -(-(n_batch + 1) // 8) * 8
    cc = jnp.zeros((mod_rows, D_MODEL), F32).at[:n_batch].set(c).at[n_batch].set(c_ctx)
    mod = _modulation(cc, ada_w, ada_b)

    consts = {k_: jnp.asarray(v_, BF16) for k_, v_ in _scan_constants().items()}
    cos_t, sin_t = _rope_tables(t_lat)
    head_of = np.arange(GLA_WIDTH) // HEAD_DIM
    bd = jnp.asarray((head_of[:, None] == head_of[None, :]).astype(np.float32) / HEAD_DIM, BF16)

    w_in_p = _permute_w_in(w_in)
    w_out_b = w_out.astype(BF16)
    rpb_tiles = _na_bias_tiles(na_rpb)
    wgk = jnp.zeros((depth, LANES, 2 * GLA_KEY), F32)
    wgk = wgk.at[:, 0:GLA_GK_RANK, 0:GLA_KEY].set(gla_w_gk[:, 0])
    wgk = wgk.at[:, GLA_GK_RANK:2 * GLA_GK_RANK, GLA_KEY:].set(gla_w_gk[:, 1]).astype(BF16)

    streams = [ctx, x]
    projections = _in_proj(streams, mod[0], norm_w[0].reshape(1, D_MODEL), w_in_p, 0, n_batch, t_all)
    for layer in range(depth):
        last = layer == depth - 1
        bgk = gla_b_gk[layer].reshape(1, 2 * GLA_KEY)
        gla_nw = jnp.tile(gla_norm_w[layer], SCAN_HEADS).reshape(1, GLA_WIDTH)
        hg_nw = jnp.tile(hgrn_norm_w[layer], SCAN_HEADS).reshape(1, HGRN_WIDTH)

        q, k, vt, gates, gla_in, hg_in = projections
        o_na = _neighborhood_attention(q, k, vt, rpb_tiles, layer, with_ctx_queries=not last)
        scans = _scans(gla_in, hg_in, cos_t, sin_t, wgk[layer], bgk, lbs[layer], consts)
        mix_args = (o_na, scans, gates, mod[layer], gla_nw, hg_nw, bd, w_out_b[layer])
        if last:
            return _final_out(streams, mix_args, final_norm_w.reshape(1, D_MODEL), n_batch, t_all)
        stream, projections = _out_in_proj(streams, mix_args, mod[layer + 1],
                                           norm_w[layer + 1].reshape(1, D_MODEL), w_in_p, layer + 1,
                                           n_batch, t_all)
        streams = [stream]
```

```python
import functools

import numpy as np
import jax
import jax.numpy as jnp
from jax import lax
from jax.experimental import pallas as pl
from jax.experimental.pallas import tpu as pltpu

F32 = jnp.float32
BF16 = jnp.bfloat16

D_MODEL = 1024
CTX_LEN = 256
GRID_W = 64
NORM_EPS = 1e-6
HEAD_DIM = 64
NA_WIDTH = 512
NA_HEADS = 8
NA_WIN_H = 8
NA_WIN_W = 16
GLA_WIDTH = 256
GLA_KEY = 128
GLA_GK_RANK = 16
GLA_GATE_NORM = 16.0
HGRN_WIDTH = 256
SCAN_HEADS = 4
CHUNK = 16
ROPE_BASE = 10000.0

LOG2_E = float(np.log2(np.e))
NA_Q_SCALE = HEAD_DIM ** -0.5 * LOG2_E

LANES = 128
TOKEN_TILE = 256
NA_PIPELINE_LAG = 3
FINAL_TILE = 512
SCAN_BLOCK = 128
SCAN_SUB = 2
CHUNKS_PER_BLOCK = SCAN_BLOCK // CHUNK
NA_Q_ROWS = TOKEN_TILE // GRID_W
NA_KEY_ROWS = NA_Q_ROWS + NA_WIN_H
NA_KEYS = NA_KEY_ROWS * GRID_W
assert NA_KEYS % LANES == 0 and (NA_Q_ROWS * GRID_W) % LANES == 0 and NA_WIN_H % 4 == 0
VMEM_LIMIT = 56 * 1024 * 1024

IN_SPLITS = (512, 512, 512, 512, 128, 128, 256, 32, 256, 256, 256, 256, 256, 256)
IN_OFFS = tuple(int(v) for v in np.cumsum((0,) + IN_SPLITS))
GK_PAD = LANES - 2 * GLA_GK_RANK
W_NA = 3 * NA_WIDTH
W_GATE = NA_WIDTH + GLA_WIDTH + HGRN_WIDTH
W_GLA = GLA_WIDTH + 2 * GLA_KEY + LANES
W_HG = 4 * HGRN_WIDTH
W_IN_PAD = W_NA + W_GATE + W_GLA + W_HG


def _dot(a, b):
    return jnp.dot(a, b, preferred_element_type=F32)


def _dot_nt(a, b):
    return lax.dot_general(a, b, (((1,), (1,)), ((), ())), preferred_element_type=F32)


def _dot_tn(a, b):
    return lax.dot_general(a, b, (((0,), (0,)), ((), ())), preferred_element_type=F32)


def _sigmoid(x):
    return 1.0 / (1.0 + jnp.exp(-x))


def _silu(x):
    return x * _sigmoid(x)


def _mod_kernel(c_ref, w_ref, b_ref, o_ref):
    s = _silu(c_ref[...]).astype(BF16)
    o_ref[0] = _dot(s, w_ref[0].astype(BF16)) + b_ref[0]


def _modulation(cc, ada_w, ada_b):
    depth = ada_w.shape[0]
    rows = cc.shape[0]
    return pl.pallas_call(
        _mod_kernel,
        grid=(depth, 3),
        in_specs=[
            pl.BlockSpec((rows, D_MODEL), lambda l, j: (0, 0)),
            pl.BlockSpec((1, D_MODEL, D_MODEL), lambda l, j: (l, 0, j)),
            pl.BlockSpec((1, 1, D_MODEL), lambda l, j: (l, 0, j)),
        ],
        out_specs=pl.BlockSpec((1, rows, D_MODEL), lambda l, j: (l, 0, j)),
        out_shape=jax.ShapeDtypeStruct((depth, rows, 3 * D_MODEL), F32),
        compiler_params=pltpu.CompilerParams(
            dimension_semantics=("parallel", "parallel"), vmem_limit_bytes=VMEM_LIMIT),
        name="adaln_mod",
    )(cc, ada_w, ada_b.reshape(depth, 1, 3 * D_MODEL))


def _token_tile(x_refs, n_batch, tile_off=0):
    t = pl.program_id(1) + tile_off
    is_ctx = t < CTX_LEN // TOKEN_TILE
    x = jnp.where(is_ctx, x_refs[0][0], x_refs[1][0]) if len(x_refs) == 2 else x_refs[0][0]
    return x, jnp.where(is_ctx, n_batch, pl.program_id(0))


def _stream_specs(n_streams, tile_off=0):
    n_ctx_tiles = CTX_LEN // TOKEN_TILE
    if n_streams == 2:
        return [pl.BlockSpec((1, TOKEN_TILE, D_MODEL), lambda b, t: (b, jnp.minimum(t, n_ctx_tiles - 1), 0)),
                pl.BlockSpec((1, TOKEN_TILE, D_MODEL), lambda b, t: (b, jnp.maximum(t - n_ctx_tiles, 0), 0))]
    return [pl.BlockSpec((1, TOKEN_TILE, D_MODEL), lambda b, t: (b, t + tile_off, 0))]


def _in_proj_kernel(n_streams, n_batch, *refs):
    x, row = _token_tile(refs[:n_streams], n_batch)
    _project_tile(x, row, *refs[n_streams:])


def _project_tile(x, row, mod_ref, nw_ref, w_ref, q_ref, k_ref, v_ref, gate_ref, scan_ref):
    m = mod_ref[pl.ds(row, 1), :]
    sh = m[:, :D_MODEL]
    sc = m[:, D_MODEL:2 * D_MODEL]
    ms = jnp.mean(x * x, axis=-1, keepdims=True)
    y = x * lax.rsqrt(ms + NORM_EPS) * nw_ref[...]
    h = y * (1.0 + sc) + sh
    p = _dot(h.astype(BF16), w_ref[...])
    for pr in range(NA_HEADS // 2):
        lo = pr * LANES
        q_ref[0, pr] = (p[:, lo:lo + LANES] * NA_Q_SCALE).astype(BF16)
        k_ref[0, pr] = p[:, NA_WIDTH + lo:NA_WIDTH + lo + LANES].astype(BF16)
        for s in range(TOKEN_TILE // LANES):
            v_tile = p[s * LANES:(s + 1) * LANES, 2 * NA_WIDTH + lo:2 * NA_WIDTH + lo + LANES]
            v_ref[0, pr, s] = jnp.transpose(v_tile).astype(BF16)
    gate_ref[0] = p[:, W_NA:W_NA + W_GATE].astype(BF16)
    scan_ref[0] = p[:, W_NA + W_GATE:]


def _projection_io(mod_l, layer, n_batch, t_all):
    in_specs = [
        pl.BlockSpec(mod_l.shape, lambda b, t: (0, 0)),
        pl.BlockSpec((1, D_MODEL), lambda b, t: (0, 0)),
        pl.BlockSpec((None, D_MODEL, W_IN_PAD), lambda b, t: (layer, 0, 0), pipeline_mode=pl.Buffered(1)),
    ]
    pair_spec = pl.BlockSpec((1, NA_HEADS // 2, TOKEN_TILE, LANES), lambda b, t: (b, 0, t, 0))
    pair_shape = jax.ShapeDtypeStruct((n_batch, NA_HEADS // 2, t_all, LANES), BF16)
    vt_spec = pl.BlockSpec((1, NA_HEADS // 2, TOKEN_TILE // LANES, LANES, LANES), lambda b, t: (b, 0, t, 0, 0))
    vt_shape = jax.ShapeDtypeStruct((n_batch, NA_HEADS // 2, t_all // LANES, LANES, LANES), BF16)

    def slab(width, dtype):
        return (pl.BlockSpec((1, TOKEN_TILE, width), lambda b, t: (b, t, 0)),
                jax.ShapeDtypeStruct((n_batch, t_all, width), dtype))

    gate_spec, gate_shape = slab(W_GATE, BF16)
    scan_spec, scan_shape = slab(W_GLA + W_HG, F32)
    return (in_specs, [pair_spec, pair_spec, vt_spec, gate_spec, scan_spec],
            [pair_shape, pair_shape, vt_shape, gate_shape, scan_shape])


def _in_proj(streams, mod_l, norm_w_l, w_in_p, layer, n_batch, t_all):
    in_specs, out_specs, out_shape = _projection_io(mod_l, layer, n_batch, t_all)
    return pl.pallas_call(
        functools.partial(_in_proj_kernel, len(streams), n_batch),
        grid=(n_batch, t_all // TOKEN_TILE),
        in_specs=_stream_specs(len(streams)) + in_specs,
        out_specs=out_specs,
        out_shape=out_shape,
        compiler_params=pltpu.CompilerParams(
            dimension_semantics=("parallel", "parallel"), vmem_limit_bytes=VMEM_LIMIT),
        name="in_proj",
    )(*streams, mod_l, norm_w_l, w_in_p)


def _na_bias_cases():
    cases = []
    for key_row_ok, rel in (
        (lambda lr, lk: lk < NA_WIN_H, lambda lr, lk: lk - lr + NA_WIN_H - 1),
        (lambda lr, lk: lr <= lk < lr + NA_WIN_H, lambda lr, lk: lk - lr + NA_WIN_H // 2 - 1),
        (lambda lr, lk: lk >= NA_KEY_ROWS - NA_WIN_H, lambda lr, lk: lk - lr - 1),
    ):
        cases.append([[rel(lr, lk) if key_row_ok(lr, lk) else None for lk in range(NA_KEY_ROWS)]
                      for lr in range(NA_Q_ROWS)])
    return cases


def _na_kernel(tile_off, q_ref, k_ref, vt_ref, rt_ref, o_ref, bias_ref):
    tb = pl.program_id(1) + tile_off
    n_pairs = q_ref.shape[1]
    lane = lax.broadcasted_iota(jnp.int32, (1, LANES), 1)
    q_scale = [jnp.where((lane // HEAD_DIM) == hh, 1.0, 0.0).astype(BF16) for hh in range(2)]
    heads = [(p, hh) for p in range(n_pairs) for hh in range(2)]
    ctx_tiles = CTX_LEN // LANES

    @pl.when((pl.program_id(0) == 0) & (pl.program_id(1) == 0))
    def _build_bias_tables():
        minus_inf = jnp.full((GRID_W, GRID_W), -jnp.inf, F32)

        def per_head(h, carry):
            for case, rel in enumerate(_na_bias_cases()):
                for lk in range(NA_KEY_ROWS):
                    for lr in range(0, NA_Q_ROWS, 2):
                        tiles = [minus_inf if rel[r][lk] is None else rt_ref[h, rel[r][lk]] for r in (lr, lr + 1)]
                        bias_ref[case, h, lk * GRID_W:(lk + 1) * GRID_W, lr * GRID_W:(lr + 2) * GRID_W] = (
                            jnp.concatenate(tiles, axis=1))
            return carry

        lax.fori_loop(0, NA_HEADS, per_head, 0)

    def vt_tiles(p, first, count):
        tiles = vt_ref[0, p, pl.ds(first, count)]
        return jnp.concatenate([tiles[i] for i in range(count)], axis=1)

    def finish(outs):
        row = lax.broadcasted_iota(jnp.int32, (LANES, 1), 0)
        for p in range(n_pairs):
            o_t = jnp.where(row < HEAD_DIM, outs[(p, 0)], outs[(p, 1)])
            o_ref[0, :, p * LANES:(p + 1) * LANES] = jnp.transpose(o_t).astype(o_ref.dtype)

    def pipelined(scores_of, attend):
        outs, issued = {}, {}
        for i in range(len(heads) + NA_PIPELINE_LAG):
            if i < len(heads):
                issued[i] = scores_of(*heads[i])
            j = i - NA_PIPELINE_LAG
            if j >= 0:
                outs[heads[j]] = attend(heads[j][0], *issued.pop(j))
        finish(outs)

    def softmax_pv(vt_parts, s_parts):
        m = functools.reduce(jnp.maximum, [jnp.max(s, axis=0, keepdims=True) for s in s_parts])
        e_parts = [jnp.exp2(s - m) for s in s_parts]
        l = functools.reduce(jnp.add, [jnp.sum(e, axis=0, keepdims=True) for e in e_parts])
        o_t = functools.reduce(jnp.add, [_dot(vt, e.astype(BF16)) for vt, e in zip(vt_parts, e_parts)])
        return o_t / l

    @pl.when(tb == 0)
    def _ctx_queries():
        def scores_of(p, hh):
            return (_dot_nt(k_ref[0, p, 0:CTX_LEN, :], q_ref[0, p] * q_scale[hh]),)

        def attend(p, s):
            return softmax_pv([vt_tiles(p, 0, ctx_tiles)], [s])

        pipelined(scores_of, attend)

    @pl.when(tb > 0)
    def _latent_queries():
        g = tb - 1
        n_groups = pl.num_programs(1) - 1 + tile_off
        ws = jnp.clip(NA_Q_ROWS * g - NA_WIN_H // 2, 0, GRID_W - NA_KEY_ROWS)
        case = (g > 0).astype(jnp.int32) + (g == n_groups - 1).astype(jnp.int32)
        start = pl.multiple_of(CTX_LEN + ws * GRID_W, LANES)
        first_tile = ctx_tiles + ws * GRID_W // LANES

        def scores_of(p, hh):
            qh = q_ref[0, p] * q_scale[hh]
            s_nb = _dot_nt(k_ref[0, p, pl.ds(start, NA_KEYS), :], qh) + bias_ref[case, 2 * p + hh]
            return s_nb, _dot_nt(k_ref[0, p, 0:CTX_LEN, :], qh)

        def attend(p, s_nb, s_cx):
            return softmax_pv([vt_tiles(p, first_tile, NA_KEYS // LANES), vt_tiles(p, 0, ctx_tiles)],
                              [s_nb, s_cx])

        pipelined(scores_of, attend)


def _neighborhood_attention(q, k, vt, rpb_tiles, layer, with_ctx_queries):
    n_batch, n_pairs, t_all, _ = q.shape
    n_tiles = t_all // TOKEN_TILE
    tile_off = 0 if with_ctx_queries else CTX_LEN // TOKEN_TILE
    n_steps = n_tiles - tile_off
    t_out = n_steps * TOKEN_TILE
    return pl.pallas_call(
        functools.partial(_na_kernel, tile_off),
        grid=(n_batch, n_steps),
        in_specs=[
            pl.BlockSpec((1, n_pairs, TOKEN_TILE, LANES), lambda b, g: (b, 0, g + tile_off, 0)),
            pl.BlockSpec((1, n_pairs, t_all, LANES), lambda b, g: (b, 0, 0, 0)),
            pl.BlockSpec((1, n_pairs, t_all // LANES, LANES, LANES), lambda b, g: (b, 0, 0, 0, 0)),
            pl.BlockSpec((None,) + rpb_tiles.shape[1:], lambda b, g: (layer, 0, 0, 0, 0),
                         pipeline_mode=pl.Buffered(1)),
        ],
        out_specs=pl.BlockSpec((1, TOKEN_TILE, NA_WIDTH), lambda b, g: (b, g, 0)),
        out_shape=jax.ShapeDtypeStruct((n_batch, t_out, NA_WIDTH), BF16),
        scratch_shapes=[pltpu.VMEM((3, NA_HEADS, NA_KEYS, TOKEN_TILE), F32)],
        compiler_params=pltpu.CompilerParams(
            dimension_semantics=("arbitrary", "arbitrary"),
            vmem_limit_bytes=VMEM_LIMIT),
        name="neighborhood_attention",
    )(q, k, vt, rpb_tiles)


def _na_bias_tiles(rpb):
    col = np.arange(GRID_W)
    col_start = np.clip(col - NA_WIN_W // 2, 0, GRID_W - NA_WIN_W)
    col_in = (col[:, None] >= col_start[None, :]) & (col[:, None] < col_start[None, :] + NA_WIN_W)
    col_idx = np.clip(col[:, None] - col[None, :], 1 - NA_WIN_W, NA_WIN_W - 1) + NA_WIN_W - 1
    one_hot = (col_idx[None] == np.arange(2 * NA_WIN_W - 1)[:, None, None]).astype(np.float32)
    tiles = jnp.einsum("lhrw,wkj->lhrkj", rpb.astype(F32), jnp.asarray(one_hot),
                       precision=lax.Precision.HIGHEST)
    return jnp.where(col_in, tiles * LOG2_E, -jnp.inf)


def _scan_constants():
    idx = np.arange(SCAN_BLOCK)
    c = idx // CHUNK
    pos = idx % CHUNK
    same = c[:, None] == c[None, :]
    return {"cum_f": (same & (pos[None, :] <= pos[:, None])).astype(np.float32),
            "cum_b": (same & (pos[None, :] >= pos[:, None])).astype(np.float32)}


def _block_matrix(n_rows, n_cols, block_shape, dtype, block_at):
    zero = jnp.zeros(block_shape, dtype)
    rows = []
    for r in range(n_rows):
        blocks = [block_at(r, c) for c in range(n_cols)]
        rows.append(jnp.concatenate([zero if b is None else b for b in blocks], axis=1))
    return jnp.concatenate(rows, axis=0)


def _run_interleaved(stagewise):
    results = [None] * len(stagewise)
    live = list(range(len(stagewise)))
    while live:
        for i in list(live):
            try:
                next(stagewise[i])
            except StopIteration as done:
                results[i] = done.value
                live.remove(i)
    return results


def _scan_direction(load_inputs, cum, st_ref, reverse):
    q, k, v, lf = load_inputs()
    n = SCAN_BLOCK
    hd = q.shape[1]
    dk = hd // SCAN_HEADS
    pack = 2
    low_half = lax.broadcasted_iota(jnp.int32, (1, LANES), 1) < HEAD_DIM
    lane_k = lax.broadcasted_iota(jnp.int32, (1, hd), 1) // dk
    lane_v = lax.broadcasted_iota(jnp.int32, (1, SCAN_HEADS * HEAD_DIM), 1) // HEAD_DIM

    def lane_mask(cond):
        return jnp.where(cond, 1.0, 0.0).astype(BF16)

    half_masks = {True: lane_mask(low_half), False: lane_mask(jnp.logical_not(low_half))}
    k_masks = [lane_mask(lane_k == h) for h in range(SCAN_HEADS)]
    v_masks = [lane_mask(lane_v == h) for h in range(SCAN_HEADS)]

    def keep_half(x, low):
        return x * half_masks[low]

    hi = lf.astype(BF16)
    mid = (lf - hi.astype(F32)).astype(BF16)
    g = _dot(cum, hi) + _dot(cum, mid)
    yield
    ref_pos, last_pos = (CHUNK // 2, 0) if reverse else (CHUNK // 2 - 1, CHUNK - 1)
    to_ref, to_last, decay = [], [], []
    for c in range(CHUNKS_PER_BLOCK):
        gc = g[c * CHUNK:(c + 1) * CHUNK]
        g_last = gc[last_pos:last_pos + 1]
        to_ref.append(gc - gc[ref_pos:ref_pos + 1])
        to_last.append(g_last - gc)
        decay.append(jnp.exp(g_last))
    x_ref = jnp.concatenate(to_ref, axis=0)
    a = jnp.exp(x_ref)
    a_inv = jnp.exp(-x_ref)
    e_in = jnp.exp(g)
    e_out = jnp.exp(jnp.concatenate(to_last, axis=0))

    q_t = (q * a).astype(BF16)
    k_t_tr = jnp.transpose(k * a_inv).astype(BF16)
    q_in = (q * e_in).astype(BF16)
    k_out = (k * e_out).astype(BF16)
    v_b = v.astype(BF16)

    k_blk = _block_matrix(SCAN_HEADS, SCAN_HEADS, (dk, n), BF16,
                          lambda r, c: k_t_tr[r * dk:(r + 1) * dk] if r == c else None)
    scores = _dot(q_t, k_blk)

    v_tr = jnp.transpose(v).astype(BF16)
    token_pair = lax.broadcasted_iota(jnp.int32, (1, n), 1) // (CHUNK * pack)
    pair_masks = [lane_mask(token_pair == m) for m in range(CHUNKS_PER_BLOCK // pack)]
    heads_per_group = LANES // dk
    incs = []
    for grp in range(hd // LANES):
        heads = range(grp * heads_per_group, (grp + 1) * heads_per_group)
        v_blk = jnp.concatenate(
            [jnp.concatenate([v_tr[h * HEAD_DIM:(h + 1) * HEAD_DIM] * pm for h in heads], axis=1)
             for pm in pair_masks], axis=0)
        k_rows = []
        for h in heads:
            k_h = (k_out * k_masks[h])[:, grp * LANES:(grp + 1) * LANES]
            k_rows.append(_block_matrix(
                CHUNKS_PER_BLOCK, pack, (CHUNK, LANES), BF16,
                lambda c, par, k_h=k_h: k_h[c * CHUNK:(c + 1) * CHUNK] if c % pack == par else None))
        incs.append(_dot(v_blk, jnp.concatenate(k_rows, axis=0)))
    yield

    tri = jnp.concatenate([cum.astype(F32)] * SCAN_HEADS, axis=1) > 0.5
    p = jnp.where(tri, scores, 0.0).astype(BF16)
    v_stack = jnp.concatenate([v_b * v_masks[h] for h in range(SCAN_HEADS)], axis=0)
    o_intra = _dot(p, v_stack)

    st = st_ref[...]
    states = [None] * CHUNKS_PER_BLOCK
    order = range(CHUNKS_PER_BLOCK - 1, -1, -1) if reverse else range(CHUNKS_PER_BLOCK)
    for c in order:
        states[c] = st
        m, par = divmod(c, pack)
        inc_c = jnp.concatenate([inc[m * HEAD_DIM:(m + 1) * HEAD_DIM, par * LANES:(par + 1) * LANES]
                                 for inc in incs], axis=1)
        st = decay[c] * st + inc_c
    st_ref[...] = st

    q_blk = _block_matrix(CHUNKS_PER_BLOCK, CHUNKS_PER_BLOCK, (CHUNK, hd), BF16,
                          lambda r, c: q_in[r * CHUNK:(r + 1) * CHUNK] if r == c else None)
    s_parts = []
    for s in states:
        s_tr = jnp.transpose(jnp.concatenate([s, s], axis=0)).astype(BF16)

        def s_block(h, j, s_tr=s_tr):
            return keep_half(s_tr[h * dk:(h + 1) * dk], h % 2 == 0) if h // 2 == j else None

        s_parts.append(_block_matrix(SCAN_HEADS, SCAN_HEADS // 2, (dk, LANES), BF16, s_block))
    return o_intra + _dot(q_blk, jnp.concatenate(s_parts, axis=0))


def _rope(x, cos, sin_signed):
    lane = lax.broadcasted_iota(jnp.int32, (1, x.shape[1]), 1)
    first = (lane % 16) < 8
    partner = jnp.where(first, pltpu.roll(x, x.shape[1] - 8, 1), pltpu.roll(x, 8, 1))
    return x * cos + partner * sin_signed


def _scan_kernel(n_ctx_blocks,
                 in_f_ref, in_b_ref,
                 cos_f_ref, sin_f_ref, cos_b_ref, sin_b_ref,
                 wgk_ref, bgk_ref, lb_ref, cum_f_ref, cum_b_ref,
                 of_ref, ob_ref,
                 sgf_ref, sgb_ref, shf_ref, shb_ref):
    del n_ctx_blocks
    gla_cols, hg_cols = pl.ds(0, GLA_WIDTH), pl.ds(GLA_WIDTH, HGRN_WIDTH)

    @pl.when(pl.program_id(1) == 0)
    def _reset():
        for ref in (sgf_ref, sgb_ref, shf_ref, shb_ref):
            ref[...] = jnp.zeros_like(ref)

    def gla_inputs(in_ref, cos_ref, sin_ref, direction, rows):
        blk = in_ref[0, rows, 0:W_GLA]
        v = blk[:, 0:GLA_WIDTH]
        q = blk[:, GLA_WIDTH:GLA_WIDTH + GLA_KEY]
        k = blk[:, GLA_WIDTH + GLA_KEY:GLA_WIDTH + 2 * GLA_KEY]
        gk = blk[:, GLA_WIDTH + 2 * GLA_KEY:]
        cos = cos_ref[rows, :]
        sin = sin_ref[rows, :]
        q = _rope(q, cos, sin) * (GLA_KEY // SCAN_HEADS) ** -0.5
        k = _rope(k, cos, sin)
        lo = direction * GLA_KEY
        z = _dot(gk.astype(BF16), wgk_ref[:, lo:lo + GLA_KEY]) + bgk_ref[:, lo:lo + GLA_KEY]
        log_sig = jnp.minimum(z, 0.0) - jnp.log(1.0 + jnp.exp(-jnp.abs(z)))
        return q, k, v, log_sig / GLA_GATE_NORM

    def hgrn_inputs(in_ref, direction, rows):
        blk = in_ref[0, rows, W_GLA:W_GLA + W_HG]
        q = blk[:, 0:HGRN_WIDTH] * HEAD_DIM ** -0.5
        f_raw = blk[:, (1 + direction) * HGRN_WIDTH:(2 + direction) * HGRN_WIDTH]
        v = blk[:, 3 * HGRN_WIDTH:]
        lb = lb_ref[direction:direction + 1, :]
        f = lb + (1.0 - lb) * _sigmoid(f_raw)
        return q, 1.0 - f, v, jnp.log(f)

    jobs, dests = [], []
    for s in range(SCAN_SUB):
        rows_f = pl.ds(s * SCAN_BLOCK, SCAN_BLOCK)
        rows_b = pl.ds((SCAN_SUB - 1 - s) * SCAN_BLOCK, SCAN_BLOCK)
        jobs += [
            _scan_direction(functools.partial(gla_inputs, in_f_ref, cos_f_ref, sin_f_ref, 0, rows_f),
                            cum_f_ref[...],sgf_ref, reverse=False),
            _scan_direction(functools.partial(hgrn_inputs, in_f_ref, 0, rows_f),
                            cum_f_ref[...],shf_ref, reverse=False),
            _scan_direction(functools.partial(gla_inputs, in_b_ref, cos_b_ref, sin_b_ref, 1, rows_b),
                            cum_b_ref[...],sgb_ref, reverse=True),
            _scan_direction(functools.partial(hgrn_inputs, in_b_ref, 1, rows_b),
                            cum_b_ref[...],shb_ref, reverse=True),
        ]
        dests += [(of_ref, rows_f, gla_cols), (of_ref, rows_f, hg_cols),
                  (ob_ref, rows_b, gla_cols), (ob_ref, rows_b, hg_cols)]
    for (ref, rows, cols), val in zip(dests, _run_interleaved(jobs)):
        ref[0, rows, cols] = val.astype(ref.dtype)


def _scans(scan_in, cos_t, sin_t, wgk, bgk, lb, consts):
    n_batch, t_all, _ = scan_in.shape
    step_rows = SCAN_SUB * SCAN_BLOCK
    assert CTX_LEN % step_rows == 0 and t_all % step_rows == 0
    n_blocks = t_all // step_rows
    n_ctx_blocks = CTX_LEN // step_rows

    def fwd(i):
        return i

    def bwd(i):
        return jnp.where(i < n_ctx_blocks, n_ctx_blocks - 1 - i, n_blocks - 1 + n_ctx_blocks - i)

    def tok_spec(width, order):
        return pl.BlockSpec((1, step_rows, width), lambda b, i: (b, order(i), 0))

    def tab_spec(order):
        return pl.BlockSpec((step_rows, LANES), lambda b, i: (order(i), 0))

    def const_spec(arr):
        return pl.BlockSpec(arr.shape, lambda b, i: (0,) * arr.ndim)

    const_arrays = [wgk, bgk, lb, consts["cum_f"], consts["cum_b"]]
    out_shape = jax.ShapeDtypeStruct((n_batch, t_all, GLA_WIDTH + HGRN_WIDTH), BF16)
    return pl.pallas_call(
        functools.partial(_scan_kernel, n_ctx_blocks),
        grid=(n_batch, n_blocks),
        in_specs=[tok_spec(W_GLA + W_HG, fwd), tok_spec(W_GLA + W_HG, bwd),
                  tab_spec(fwd), tab_spec(fwd), tab_spec(bwd), tab_spec(bwd)]
                 + [const_spec(a) for a in const_arrays],
        out_specs=[tok_spec(GLA_WIDTH + HGRN_WIDTH, fwd), tok_spec(GLA_WIDTH + HGRN_WIDTH, bwd)],
        out_shape=[out_shape] * 2,
        scratch_shapes=[pltpu.VMEM((HEAD_DIM, GLA_KEY), F32), pltpu.VMEM((HEAD_DIM, GLA_KEY), F32),
                        pltpu.VMEM((HEAD_DIM, HGRN_WIDTH), F32), pltpu.VMEM((HEAD_DIM, HGRN_WIDTH), F32)],
        compiler_params=pltpu.CompilerParams(
            dimension_semantics=("parallel", "arbitrary"), vmem_limit_bytes=VMEM_LIMIT),
        name="bidirectional_scans",
    )(scan_in, scan_in, cos_t, sin_t, cos_t, sin_t, *const_arrays)


def _rope_tables(t_lat):
    half = GLA_KEY // SCAN_HEADS // 2
    inv_freq = 1.0 / (ROPE_BASE ** (np.arange(0, half, 2, dtype=np.float32) / half))
    t = np.arange(t_lat)
    lane = np.arange(LANES) % (2 * half)
    pos = np.where(lane[None, :] < half, (t // GRID_W)[:, None], (t % GRID_W)[:, None]).astype(np.float32)
    ang = (pos * inv_freq[(lane % half) % (half // 2)][None, :]).astype(np.float32)
    first = ((lane % half) < half // 2)[None, :]
    cos = np.cos(ang)
    sin = np.where(first, -np.sin(ang), np.sin(ang))
    cos = np.concatenate([np.ones((CTX_LEN, LANES), np.float32), cos], axis=0)
    sin = np.concatenate([np.zeros((CTX_LEN, LANES), np.float32), sin], axis=0)
    return jnp.asarray(cos, F32), jnp.asarray(sin, F32)


N_MIX_REFS = 9


def _mix_tile(x, row, ona_ref, of_ref, ob_ref, gate_ref, mod_ref, gnw_ref, hnw_ref, bd_ref, w_ref):
    gt = mod_ref[pl.ds(row, 1), :][:, 2 * D_MODEL:]
    bd = bd_ref[...]

    def head_norm(o, w):
        sq = o * o
        hi = sq.astype(BF16)
        lo = (sq - hi.astype(F32)).astype(BF16)
        ms = _dot(hi, bd) + _dot(lo, bd)
        return o * lax.rsqrt(ms + NORM_EPS) * w

    o_scan = of_ref[0].astype(F32) + ob_ref[0].astype(F32)
    o_gla = head_norm(o_scan[:, :GLA_WIDTH], gnw_ref[...])
    o_hg = head_norm(o_scan[:, GLA_WIDTH:], hnw_ref[...])
    o_cat = jnp.concatenate([ona_ref[0].astype(F32), o_gla, o_hg], axis=1)
    act = o_cat * _silu(gate_ref[0].astype(F32))
    y = _dot(act.astype(BF16), w_ref[...])
    return x + gt * y


def _final_out_kernel(x_ref, *refs):
    xn = _mix_tile(x_ref[0], pl.program_id(0), *refs[:N_MIX_REFS])
    fw_ref, out_ref = refs[N_MIX_REFS:]
    ms = jnp.mean(xn * xn, axis=-1, keepdims=True)
    out_ref[0] = xn * lax.rsqrt(ms + NORM_EPS) * fw_ref[...]


def _out_in_kernel(n_streams, n_batch, *refs):
    x, row = _token_tile(refs[:n_streams], n_batch)
    xn = _mix_tile(x, row, *refs[n_streams:n_streams + N_MIX_REFS])
    rest = refs[n_streams + N_MIX_REFS:]
    proj_in, stream_ref, proj_out = rest[:3], rest[3], rest[4:]
    stream_ref[0] = xn
    _project_tile(xn, row, *proj_in, *proj_out)


def _token_spec(width, tile, first_token):
    if first_token % tile == 0:
        return pl.BlockSpec((1, tile, width), lambda b, t: (b, t + first_token // tile, 0))
    return pl.BlockSpec((pl.Element(1), pl.Element(tile), pl.Element(width)),
                        lambda b, t: (b, pl.multiple_of(first_token + t * tile, TOKEN_TILE), 0))


def _mix_io(o_na, scans, gates, mod_l, gla_nw, hg_nw, bd, w_out_b, layer, tile=TOKEN_TILE, first_token=0):
    def tok(width):
        return _token_spec(width, tile, first_token)

    def const_spec(arr):
        return pl.BlockSpec(arr.shape, lambda b, t: (0,) * arr.ndim)

    consts = [mod_l, gla_nw, hg_nw, bd]
    specs = ([pl.BlockSpec((1, tile, NA_WIDTH), lambda b, t: (b, t, 0))]
             + [tok(GLA_WIDTH + HGRN_WIDTH)] * 2
             + [tok(W_GATE)] + [const_spec(a) for a in consts]
             + [pl.BlockSpec((None,) + w_out_b.shape[1:], lambda b, t: (layer, 0, 0))])
    return specs, [o_na, *scans, gates, *consts, w_out_b]


def _final_out(streams, mix_args, final_w, n_batch, t_all):
    (stream,) = streams
    t_lat = t_all - CTX_LEN
    assert t_lat % FINAL_TILE == 0
    mix_specs, mix_arrays = _mix_io(*mix_args, tile=FINAL_TILE, first_token=CTX_LEN)
    return pl.pallas_call(
        _final_out_kernel,
        grid=(n_batch, t_lat // FINAL_TILE),
        in_specs=[_token_spec(D_MODEL, FINAL_TILE, CTX_LEN)] + mix_specs
                 + [pl.BlockSpec((1, D_MODEL), lambda b, t: (0, 0))],
        out_specs=pl.BlockSpec((1, FINAL_TILE, D_MODEL), lambda b, t: (b, t, 0)),
        out_shape=jax.ShapeDtypeStruct((n_batch, t_lat, D_MODEL), F32),
        compiler_params=pltpu.CompilerParams(
            dimension_semantics=("parallel", "parallel"), vmem_limit_bytes=VMEM_LIMIT),
        name="out_proj_final",
    )(stream, *mix_arrays, final_w)


def _out_in_proj(streams, mix_args, mod_next, norm_w_next, w_in_p, next_layer, n_batch, t_all):
    mix_specs, mix_arrays = _mix_io(*mix_args)
    proj_specs, out_specs, out_shape = _projection_io(mod_next, next_layer, n_batch, t_all)
    stream_spec = pl.BlockSpec((1, TOKEN_TILE, D_MODEL), lambda b, t: (b, t, 0))
    stream_shape = jax.ShapeDtypeStruct((n_batch, t_all, D_MODEL), F32)
    outs = pl.pallas_call(
        functools.partial(_out_in_kernel, len(streams), n_batch),
        grid=(n_batch, t_all // TOKEN_TILE),
        in_specs=_stream_specs(len(streams)) + mix_specs + proj_specs,
        out_specs=[stream_spec] + out_specs,
        out_shape=[stream_shape] + out_shape,
        compiler_params=pltpu.CompilerParams(
            dimension_semantics=("parallel", "parallel"), vmem_limit_bytes=VMEM_LIMIT),
        name="out_in_proj",
    )(*streams, *mix_arrays, mod_next, norm_w_next, w_in_p)
    return outs[0], outs[1:]


def _permute_w_in(w):
    w = w.astype(BF16)
    seg = [w[..., IN_OFFS[i]:IN_OFFS[i + 1]] for i in range(len(IN_SPLITS))]
    pad = jnp.zeros(w.shape[:-1] + (GK_PAD,), w.dtype)
    cols = [seg[0], seg[1], seg[2],
            seg[3], seg[8], seg[13],
            seg[6], seg[4], seg[5], seg[7], pad,
            seg[9], seg[10], seg[11], seg[12]]
    return jnp.concatenate(cols, axis=-1)


def kernel(x, c, ctx, c_ctx, ada_w, ada_b, norm_w, w_in, na_rpb, gla_w_gk, gla_b_gk, gla_norm_w,
           hgrn_lower_bounds, hgrn_norm_w, w_out, final_norm_w):
    n_batch, t_lat, _ = x.shape
    depth = ada_w.shape[0]
    t_all = CTX_LEN + t_lat
    assert ctx.shape[1] == CTX_LEN and t_lat % TOKEN_TILE == 0 and t_lat // GRID_W >= NA_KEY_ROWS

    lbs = jnp.cumsum(jax.nn.softmax(hgrn_lower_bounds.astype(F32), axis=0), axis=0)
    lbs = lbs - lbs[0:1]
    mod_rows = -(-(n_batch + 1) // 8) * 8
    cc = jnp.zeros((mod_rows, D_MODEL), F32).at[:n_batch].set(c).at[n_batch].set(c_ctx)
    mod = _modulation(cc, ada_w, ada_b)

    consts = {k_: jnp.asarray(v_, BF16) for k_, v_ in _scan_constants().items()}
    cos_t, sin_t = _rope_tables(t_lat)
    head_of = np.arange(GLA_WIDTH) // HEAD_DIM
    bd = jnp.asarray((head_of[:, None] == head_of[None, :]).astype(np.float32) / HEAD_DIM, BF16)

    w_in_p = _permute_w_in(w_in)
    w_out_b = w_out.astype(BF16)
    rpb_tiles = _na_bias_tiles(na_rpb)
    wgk = jnp.zeros((depth, LANES, 2 * GLA_KEY), F32)
    wgk = wgk.at[:, 0:GLA_GK_RANK, 0:GLA_KEY].set(gla_w_gk[:, 0])
    wgk = wgk.at[:, GLA_GK_RANK:2 * GLA_GK_RANK, GLA_KEY:].set(gla_w_gk[:, 1]).astype(BF16)

    streams = [ctx, x]
    projections = _in_proj(streams, mod[0], norm_w[0].reshape(1, D_MODEL), w_in_p, 0, n_batch, t_all)
    for layer in range(depth):
        last = layer == depth - 1
        bgk = gla_b_gk[layer].reshape(1, 2 * GLA_KEY)
        gla_nw = jnp.tile(gla_norm_w[layer], SCAN_HEADS).reshape(1, GLA_WIDTH)
        hg_nw = jnp.tile(hgrn_norm_w[layer], SCAN_HEADS).reshape(1, HGRN_WIDTH)

        q, k, vt, gates, scan_in = projections
        o_na = _neighborhood_attention(q, k, vt, rpb_tiles, layer, with_ctx_queries=not last)
        scans = _scans(scan_in, cos_t, sin_t, wgk[layer], bgk, lbs[layer], consts)
        mix_args = (o_na, scans, gates, mod[layer], gla_nw, hg_nw, bd, w_out_b, layer)
        if last:
            return _final_out(streams, mix_args, final_norm_w.reshape(1, D_MODEL), n_batch, t_all)
        stream, projections = _out_in_proj(streams, mix_args, mod[layer + 1],
                                           norm_w[layer + 1].reshape(1, D_MODEL), w_in_p, layer + 1,
                                           n_batch, t_all)
        streams = [stream]
```

```python
import functools

import numpy as np
import jax
import jax.numpy as jnp
from jax import lax
from jax.experimental import pallas as pl
from jax.experimental.pallas import tpu as pltpu

F32 = jnp.float32
BF16 = jnp.bfloat16

D_MODEL = 1024
CTX_LEN = 256
GRID_W = 64
NORM_EPS = 1e-6
HEAD_DIM = 64
NA_WIDTH = 512
NA_HEADS = 8
NA_WIN_H = 8
NA_WIN_W = 16
GLA_WIDTH = 256
GLA_KEY = 128
GLA_GK_RANK = 16
GLA_GATE_NORM = 16.0
HGRN_WIDTH = 256
SCAN_HEADS = 4
CHUNK = 16
ROPE_BASE = 10000.0

LOG2_E = float(np.log2(np.e))
NA_Q_SCALE = HEAD_DIM ** -0.5 * LOG2_E

LANES = 128
TOKEN_TILE = 256
NA_PIPELINE_LAG = 3
FINAL_TILE = 512
SCAN_BLOCK = 128
SCAN_SUB = 2
CHUNKS_PER_BLOCK = SCAN_BLOCK // CHUNK
NA_Q_ROWS = TOKEN_TILE // GRID_W
NA_KEY_ROWS = NA_Q_ROWS + NA_WIN_H
NA_KEYS = NA_KEY_ROWS * GRID_W
assert NA_KEYS % LANES == 0 and (NA_Q_ROWS * GRID_W) % LANES == 0 and NA_WIN_H % 4 == 0
VMEM_LIMIT = 56 * 1024 * 1024

IN_NAMES = ("na_q", "na_k", "na_v", "na_g", "gla_q", "gla_k", "gla_v", "gla_gk", "gla_g",
            "hg_q", "hg_ff", "hg_fb", "hg_i", "hg_g")
IN_SPLITS = (512, 512, 512, 512, 128, 128, 256, 32, 256, 256, 256, 256, 256, 256)
IN_OFFS = tuple(int(v) for v in np.cumsum((0,) + IN_SPLITS))
W_GATE = NA_WIDTH + GLA_WIDTH + HGRN_WIDTH
W_GLA = GLA_WIDTH + 2 * GLA_KEY + LANES
W_HG = 4 * HGRN_WIDTH
W_IN_PAD = -(-IN_OFFS[-1] // LANES) * LANES


def _dot(a, b):
    return jnp.dot(a, b, preferred_element_type=F32)


def _dot_nt(a, b):
    return lax.dot_general(a, b, (((1,), (1,)), ((), ())), preferred_element_type=F32)


def _dot_tn(a, b):
    return lax.dot_general(a, b, (((0,), (0,)), ((), ())), preferred_element_type=F32)


def _sigmoid(x):
    return 1.0 / (1.0 + jnp.exp(-x))


def _silu(x):
    return x * _sigmoid(x)


def _mod_kernel(c_ref, w_ref, b_ref, o_ref):
    s = _silu(c_ref[...]).astype(BF16)
    o_ref[0] = _dot(s, w_ref[0].astype(BF16)) + b_ref[0]


def _modulation(cc, ada_w, ada_b):
    depth = ada_w.shape[0]
    rows = cc.shape[0]
    return pl.pallas_call(
        _mod_kernel,
        grid=(depth, 3),
        in_specs=[
            pl.BlockSpec((rows, D_MODEL), lambda l, j: (0, 0)),
            pl.BlockSpec((1, D_MODEL, D_MODEL), lambda l, j: (l, 0, j)),
            pl.BlockSpec((1, 1, D_MODEL), lambda l, j: (l, 0, j)),
        ],
        out_specs=pl.BlockSpec((1, rows, D_MODEL), lambda l, j: (l, 0, j)),
        out_shape=jax.ShapeDtypeStruct((depth, rows, 3 * D_MODEL), F32),
        compiler_params=pltpu.CompilerParams(
            dimension_semantics=("parallel", "parallel"), vmem_limit_bytes=VMEM_LIMIT),
        name="adaln_mod",
    )(cc, ada_w, ada_b.reshape(depth, 1, 3 * D_MODEL))


def _token_tile(x_refs, n_batch, tile_off=0):
    t = pl.program_id(1) + tile_off
    is_ctx = t < CTX_LEN // TOKEN_TILE
    x = jnp.where(is_ctx, x_refs[0][0], x_refs[1][0]) if len(x_refs) == 2 else x_refs[0][0]
    return x, jnp.where(is_ctx, n_batch, pl.program_id(0))


def _stream_specs(n_streams, tile_off=0):
    n_ctx_tiles = CTX_LEN // TOKEN_TILE
    if n_streams == 2:
        return [pl.BlockSpec((1, TOKEN_TILE, D_MODEL), lambda b, t: (b, jnp.minimum(t, n_ctx_tiles - 1), 0)),
                pl.BlockSpec((1, TOKEN_TILE, D_MODEL), lambda b, t: (b, jnp.maximum(t - n_ctx_tiles, 0), 0))]
    return [pl.BlockSpec((1, TOKEN_TILE, D_MODEL), lambda b, t: (b, t + tile_off, 0))]


def _in_proj_kernel(n_streams, n_batch, *refs):
    x, row = _token_tile(refs[:n_streams], n_batch)
    _project_tile(x, row, *refs[n_streams:])


def _project_tile(x, row, mod_ref, nw_ref, w_ref, q_ref, k_ref, v_ref, gate_ref, scan_ref):
    m = mod_ref[pl.ds(row, 1), :]
    sh = m[:, :D_MODEL]
    sc = m[:, D_MODEL:2 * D_MODEL]
    ms = jnp.mean(x * x, axis=-1, keepdims=True)
    y = x * lax.rsqrt(ms + NORM_EPS) * nw_ref[...]
    h = y * (1.0 + sc) + sh
    p = _dot(h.astype(BF16), w_ref[...])
    for pr in range(NA_HEADS // 2):
        lo = pr * LANES
        q_ref[0, pr] = (p[:, lo:lo + LANES] * NA_Q_SCALE).astype(BF16)
        k_ref[0, pr] = p[:, NA_WIDTH + lo:NA_WIDTH + lo + LANES].astype(BF16)
        for s in range(TOKEN_TILE // LANES):
            v_tile = p[s * LANES:(s + 1) * LANES, 2 * NA_WIDTH + lo:2 * NA_WIDTH + lo + LANES]
            v_ref[0, pr, s] = jnp.transpose(v_tile).astype(BF16)
    off = dict(zip(IN_NAMES, IN_OFFS))
    tail = p[:, off["gla_gk"]:]
    aligned = pltpu.roll(tail, tail.shape[1] - IN_SPLITS[IN_NAMES.index("gla_gk")], 1)
    seg = lambda name: aligned[:, off[name] - off["gla_g"]:off[name] - off["gla_g"] + IN_SPLITS[IN_NAMES.index(name)]]
    gate_ref[0] = jnp.concatenate([p[:, off["na_g"]:off["gla_q"]], seg("gla_g"), seg("hg_g")],
                                  axis=1).astype(BF16)
    scan_ref[0] = jnp.concatenate(
        [p[:, off["gla_v"]:off["gla_gk"]], p[:, off["gla_q"]:off["gla_v"]], tail[:, :LANES],
         aligned[:, off["hg_q"] - off["gla_g"]:off["hg_g"] - off["gla_g"]]], axis=1)


def _projection_io(mod_l, layer, n_batch, t_all):
    in_specs = [
        pl.BlockSpec(mod_l.shape, lambda b, t: (0, 0)),
        pl.BlockSpec((1, D_MODEL), lambda b, t: (0, 0)),
        pl.BlockSpec((None, D_MODEL, W_IN_PAD), lambda b, t: (layer, 0, 0), pipeline_mode=pl.Buffered(1)),
    ]
    pair_spec = pl.BlockSpec((1, NA_HEADS // 2, TOKEN_TILE, LANES), lambda b, t: (b, 0, t, 0))
    pair_shape = jax.ShapeDtypeStruct((n_batch, NA_HEADS // 2, t_all, LANES), BF16)
    vt_spec = pl.BlockSpec((1, NA_HEADS // 2, TOKEN_TILE // LANES, LANES, LANES), lambda b, t: (b, 0, t, 0, 0))
    vt_shape = jax.ShapeDtypeStruct((n_batch, NA_HEADS // 2, t_all // LANES, LANES, LANES), BF16)

    def slab(width, dtype):
        return (pl.BlockSpec((1, TOKEN_TILE, width), lambda b, t: (b, t, 0)),
                jax.ShapeDtypeStruct((n_batch, t_all, width), dtype))

    gate_spec, gate_shape = slab(W_GATE, BF16)
    scan_spec, scan_shape = slab(W_GLA + W_HG, F32)
    return (in_specs, [pair_spec, pair_spec, vt_spec, gate_spec, scan_spec],
            [pair_shape, pair_shape, vt_shape, gate_shape, scan_shape])


def _in_proj(streams, mod_l, norm_w_l, w_in_p, layer, n_batch, t_all):
    in_specs, out_specs, out_shape = _projection_io(mod_l, layer, n_batch, t_all)
    return pl.pallas_call(
        functools.partial(_in_proj_kernel, len(streams), n_batch),
        grid=(n_batch, t_all // TOKEN_TILE),
        in_specs=_stream_specs(len(streams)) + in_specs,
        out_specs=out_specs,
        out_shape=out_shape,
        compiler_params=pltpu.CompilerParams(
            dimension_semantics=("parallel", "parallel"), vmem_limit_bytes=VMEM_LIMIT),
        name="in_proj",
    )(*streams, mod_l, norm_w_l, w_in_p)


def _na_bias_cases():
    cases = []
    for key_row_ok, rel in (
        (lambda lr, lk: lk < NA_WIN_H, lambda lr, lk: lk - lr + NA_WIN_H - 1),
        (lambda lr, lk: lr <= lk < lr + NA_WIN_H, lambda lr, lk: lk - lr + NA_WIN_H // 2 - 1),
        (lambda lr, lk: lk >= NA_KEY_ROWS - NA_WIN_H, lambda lr, lk: lk - lr - 1),
    ):
        cases.append([[rel(lr, lk) if key_row_ok(lr, lk) else None for lk in range(NA_KEY_ROWS)]
                      for lr in range(NA_Q_ROWS)])
    return cases


def _na_kernel(tile_off, q_ref, k_ref, vt_ref, rt_ref, o_ref, bias_ref):
    tb = pl.program_id(1) + tile_off
    n_pairs = q_ref.shape[1]
    lane = lax.broadcasted_iota(jnp.int32, (1, LANES), 1)
    q_scale = [jnp.where((lane // HEAD_DIM) == hh, 1.0, 0.0).astype(BF16) for hh in range(2)]
    heads = [(p, hh) for p in range(n_pairs) for hh in range(2)]
    ctx_tiles = CTX_LEN // LANES

    @pl.when((pl.program_id(0) == 0) & (pl.program_id(1) == 0))
    def _build_bias_tables():
        minus_inf = jnp.full((GRID_W, GRID_W), -jnp.inf, F32)

        def per_head(h, carry):
            for case, rel in enumerate(_na_bias_cases()):
                for lk in range(NA_KEY_ROWS):
                    for lr in range(0, NA_Q_ROWS, 2):
                        tiles = [minus_inf if rel[r][lk] is None else rt_ref[h, rel[r][lk]] for r in (lr, lr + 1)]
                        bias_ref[case, h, lk * GRID_W:(lk + 1) * GRID_W, lr * GRID_W:(lr + 2) * GRID_W] = (
                            jnp.concatenate(tiles, axis=1))
            return carry

        lax.fori_loop(0, NA_HEADS, per_head, 0)

    def vt_tiles(p, first, count):
        tiles = vt_ref[0, p, pl.ds(first, count)]
        return jnp.concatenate([tiles[i] for i in range(count)], axis=1)

    def finish(outs):
        row = lax.broadcasted_iota(jnp.int32, (LANES, 1), 0)
        for p in range(n_pairs):
            o_t = jnp.where(row < HEAD_DIM, outs[(p, 0)], outs[(p, 1)])
            o_ref[0, :, p * LANES:(p + 1) * LANES] = jnp.transpose(o_t).astype(o_ref.dtype)

    def pipelined(scores_of, attend):
        outs, issued = {}, {}
        for i in range(len(heads) + NA_PIPELINE_LAG):
            if i < len(heads):
                issued[i] = scores_of(*heads[i])
            j = i - NA_PIPELINE_LAG
            if j >= 0:
                outs[heads[j]] = attend(heads[j][0], *issued.pop(j))
        finish(outs)

    def softmax_pv(vt_parts, s_parts):
        m = functools.reduce(jnp.maximum, [jnp.max(s, axis=0, keepdims=True) for s in s_parts])
        e_parts = [jnp.exp2(s - m) for s in s_parts]
        l = functools.reduce(jnp.add, [jnp.sum(e, axis=0, keepdims=True) for e in e_parts])
        o_t = functools.reduce(jnp.add, [_dot(vt, e.astype(BF16)) for vt, e in zip(vt_parts, e_parts)])
        return o_t / l

    @pl.when(tb == 0)
    def _ctx_queries():
        def scores_of(p, hh):
            return (_dot_nt(k_ref[0, p, 0:CTX_LEN, :], q_ref[0, p] * q_scale[hh]),)

        def attend(p, s):
            return softmax_pv([vt_tiles(p, 0, ctx_tiles)], [s])

        pipelined(scores_of, attend)

    @pl.when(tb > 0)
    def _latent_queries():
        g = tb - 1
        n_groups = pl.num_programs(1) - 1 + tile_off
        ws = jnp.clip(NA_Q_ROWS * g - NA_WIN_H // 2, 0, GRID_W - NA_KEY_ROWS)
        case = (g > 0).astype(jnp.int32) + (g == n_groups - 1).astype(jnp.int32)
        start = pl.multiple_of(CTX_LEN + ws * GRID_W, LANES)
        first_tile = ctx_tiles + ws * GRID_W // LANES

        def scores_of(p, hh):
            qh = q_ref[0, p] * q_scale[hh]
            s_nb = _dot_nt(k_ref[0, p, pl.ds(start, NA_KEYS), :], qh) + bias_ref[case, 2 * p + hh]
            return s_nb, _dot_nt(k_ref[0, p, 0:CTX_LEN, :], qh)

        def attend(p, s_nb, s_cx):
            return softmax_pv([vt_tiles(p, first_tile, NA_KEYS // LANES), vt_tiles(p, 0, ctx_tiles)],
                              [s_nb, s_cx])

        pipelined(scores_of, attend)


def _neighborhood_attention(q, k, vt, rpb_tiles, layer, with_ctx_queries):
    n_batch, n_pairs, t_all, _ = q.shape
    n_tiles = t_all // TOKEN_TILE
    tile_off = 0 if with_ctx_queries else CTX_LEN // TOKEN_TILE
    n_steps = n_tiles - tile_off
    t_out = n_steps * TOKEN_TILE
    return pl.pallas_call(
        functools.partial(_na_kernel, tile_off),
        grid=(n_batch, n_steps),
        in_specs=[
            pl.BlockSpec((1, n_pairs, TOKEN_TILE, LANES), lambda b, g: (b, 0, g + tile_off, 0)),
            pl.BlockSpec((1, n_pairs, t_all, LANES), lambda b, g: (b, 0, 0, 0)),
            pl.BlockSpec((1, n_pairs, t_all // LANES, LANES, LANES), lambda b, g: (b, 0, 0, 0, 0)),
            pl.BlockSpec((None,) + rpb_tiles.shape[1:], lambda b, g: (layer, 0, 0, 0, 0),
                         pipeline_mode=pl.Buffered(1)),
        ],
        out_specs=pl.BlockSpec((1, TOKEN_TILE, NA_WIDTH), lambda b, g: (b, g, 0)),
        out_shape=jax.ShapeDtypeStruct((n_batch, t_out, NA_WIDTH), BF16),
        scratch_shapes=[pltpu.VMEM((3, NA_HEADS, NA_KEYS, TOKEN_TILE), F32)],
        compiler_params=pltpu.CompilerParams(
            dimension_semantics=("arbitrary", "arbitrary"),
            vmem_limit_bytes=VMEM_LIMIT),
        name="neighborhood_attention",
    )(q, k, vt, rpb_tiles)


def _na_bias_tiles(rpb):
    col = np.arange(GRID_W)
    col_start = np.clip(col - NA_WIN_W // 2, 0, GRID_W - NA_WIN_W)
    col_in = (col[:, None] >= col_start[None, :]) & (col[:, None] < col_start[None, :] + NA_WIN_W)
    col_idx = np.clip(col[:, None] - col[None, :], 1 - NA_WIN_W, NA_WIN_W - 1) + NA_WIN_W - 1
    one_hot = (col_idx[None] == np.arange(2 * NA_WIN_W - 1)[:, None, None]).astype(np.float32)
    tiles = jnp.einsum("lhrw,wkj->lhrkj", rpb.astype(F32), jnp.asarray(one_hot),
                       precision=lax.Precision.HIGHEST)
    return jnp.where(col_in, tiles * LOG2_E, -jnp.inf)


def _scan_constants():
    idx = np.arange(SCAN_BLOCK)
    c = idx // CHUNK
    pos = idx % CHUNK
    same = c[:, None] == c[None, :]
    return {"cum_f": (same & (pos[None, :] <= pos[:, None])).astype(np.float32),
            "cum_b": (same & (pos[None, :] >= pos[:, None])).astype(np.float32)}


def _block_matrix(n_rows, n_cols, block_shape, dtype, block_at):
    zero = jnp.zeros(block_shape, dtype)
    rows = []
    for r in range(n_rows):
        blocks = [block_at(r, c) for c in range(n_cols)]
        rows.append(jnp.concatenate([zero if b is None else b for b in blocks], axis=1))
    return jnp.concatenate(rows, axis=0)


def _run_interleaved(stagewise):
    results = [None] * len(stagewise)
    live = list(range(len(stagewise)))
    while live:
        for i in list(live):
            try:
                next(stagewise[i])
            except StopIteration as done:
                results[i] = done.value
                live.remove(i)
    return results


def _scan_direction(load_inputs, cum, st_ref, reverse):
    q, k, v, lf = load_inputs()
    n = SCAN_BLOCK
    hd = q.shape[1]
    dk = hd // SCAN_HEADS
    pack = 2
    low_half = lax.broadcasted_iota(jnp.int32, (1, LANES), 1) < HEAD_DIM
    lane_k = lax.broadcasted_iota(jnp.int32, (1, hd), 1) // dk
    lane_v = lax.broadcasted_iota(jnp.int32, (1, SCAN_HEADS * HEAD_DIM), 1) // HEAD_DIM

    def lane_mask(cond):
        return jnp.where(cond, 1.0, 0.0).astype(BF16)

    half_masks = {True: lane_mask(low_half), False: lane_mask(jnp.logical_not(low_half))}
    k_masks = [lane_mask(lane_k == h) for h in range(SCAN_HEADS)]
    v_masks = [lane_mask(lane_v == h) for h in range(SCAN_HEADS)]

    def keep_half(x, low):
        return x * half_masks[low]

    hi = lf.astype(BF16)
    mid = (lf - hi.astype(F32)).astype(BF16)
    g = _dot(cum, hi) + _dot(cum, mid)
    yield
    ref_pos, last_pos = (CHUNK // 2, 0) if reverse else (CHUNK // 2 - 1, CHUNK - 1)
    to_ref, to_last, decay = [], [], []
    for c in range(CHUNKS_PER_BLOCK):
        gc = g[c * CHUNK:(c + 1) * CHUNK]
        g_last = gc[last_pos:last_pos + 1]
        to_ref.append(gc - gc[ref_pos:ref_pos + 1])
        to_last.append(g_last - gc)
        decay.append(jnp.exp(g_last))
    x_ref = jnp.concatenate(to_ref, axis=0)
    a = jnp.exp(x_ref)
    a_inv = jnp.exp(-x_ref)
    e_in = jnp.exp(g)
    e_out = jnp.exp(jnp.concatenate(to_last, axis=0))

    q_t = (q * a).astype(BF16)
    k_t_tr = jnp.transpose(k * a_inv).astype(BF16)
    q_in = (q * e_in).astype(BF16)
    k_out = (k * e_out).astype(BF16)
    v_b = v.astype(BF16)

    k_blk = _block_matrix(SCAN_HEADS, SCAN_HEADS, (dk, n), BF16,
                          lambda r, c: k_t_tr[r * dk:(r + 1) * dk] if r == c else None)
    scores = _dot(q_t, k_blk)

    v_tr = jnp.transpose(v).astype(BF16)
    token_pair = lax.broadcasted_iota(jnp.int32, (1, n), 1) // (CHUNK * pack)
    pair_masks = [lane_mask(token_pair == m) for m in range(CHUNKS_PER_BLOCK // pack)]
    heads_per_group = LANES // dk
    incs = []
    for grp in range(hd // LANES):
        heads = range(grp * heads_per_group, (grp + 1) * heads_per_group)
        v_blk = jnp.concatenate(
            [jnp.concatenate([v_tr[h * HEAD_DIM:(h + 1) * HEAD_DIM] * pm for h in heads], axis=1)
             for pm in pair_masks], axis=0)
        k_rows = []
        for h in heads:
            k_h = (k_out * k_masks[h])[:, grp * LANES:(grp + 1) * LANES]
            k_rows.append(_block_matrix(
                CHUNKS_PER_BLOCK, pack, (CHUNK, LANES), BF16,
                lambda c, par, k_h=k_h: k_h[c * CHUNK:(c + 1) * CHUNK] if c % pack == par else None))
        incs.append(_dot(v_blk, jnp.concatenate(k_rows, axis=0)))
    yield

    tri = jnp.concatenate([cum.astype(F32)] * SCAN_HEADS, axis=1) > 0.5
    p = jnp.where(tri, scores, 0.0).astype(BF16)
    v_stack = jnp.concatenate([v_b * v_masks[h] for h in range(SCAN_HEADS)], axis=0)
    o_intra = _dot(p, v_stack)

    st = st_ref[...]
    states = [None] * CHUNKS_PER_BLOCK
    order = range(CHUNKS_PER_BLOCK - 1, -1, -1) if reverse else range(CHUNKS_PER_BLOCK)
    for c in order:
        states[c] = st
        m, par = divmod(c, pack)
        inc_c = jnp.concatenate([inc[m * HEAD_DIM:(m + 1) * HEAD_DIM, par * LANES:(par + 1) * LANES]
                                 for inc in incs], axis=1)
        st = decay[c] * st + inc_c
    st_ref[...] = st

    q_blk = _block_matrix(CHUNKS_PER_BLOCK, CHUNKS_PER_BLOCK, (CHUNK, hd), BF16,
                          lambda r, c: q_in[r * CHUNK:(r + 1) * CHUNK] if r == c else None)
    s_parts = []
    for s in states:
        s_tr = jnp.transpose(jnp.concatenate([s, s], axis=0)).astype(BF16)

        def s_block(h, j, s_tr=s_tr):
            return keep_half(s_tr[h * dk:(h + 1) * dk], h % 2 == 0) if h // 2 == j else None

        s_parts.append(_block_matrix(SCAN_HEADS, SCAN_HEADS // 2, (dk, LANES), BF16, s_block))
    return o_intra + _dot(q_blk, jnp.concatenate(s_parts, axis=0))


def _rope(x, cos, sin_signed):
    lane = lax.broadcasted_iota(jnp.int32, (1, x.shape[1]), 1)
    first = (lane % 16) < 8
    partner = jnp.where(first, pltpu.roll(x, x.shape[1] - 8, 1), pltpu.roll(x, 8, 1))
    return x * cos + partner * sin_signed


def _scan_kernel(n_ctx_blocks,
                 in_f_ref, in_b_ref,
                 cos_f_ref, sin_f_ref, cos_b_ref, sin_b_ref,
                 wgk_ref, bgk_ref, lb_ref, cum_f_ref, cum_b_ref,
                 of_ref, ob_ref,
                 sgf_ref, sgb_ref, shf_ref, shb_ref):
    del n_ctx_blocks
    gla_cols, hg_cols = pl.ds(0, GLA_WIDTH), pl.ds(GLA_WIDTH, HGRN_WIDTH)

    @pl.when(pl.program_id(1) == 0)
    def _reset():
        for ref in (sgf_ref, sgb_ref, shf_ref, shb_ref):
            ref[...] = jnp.zeros_like(ref)

    def gla_inputs(in_ref, cos_ref, sin_ref, direction, rows):
        blk = in_ref[0, rows, 0:W_GLA]
        v = blk[:, 0:GLA_WIDTH]
        q = blk[:, GLA_WIDTH:GLA_WIDTH + GLA_KEY]
        k = blk[:, GLA_WIDTH + GLA_KEY:GLA_WIDTH + 2 * GLA_KEY]
        gk = blk[:, GLA_WIDTH + 2 * GLA_KEY:]
        cos = cos_ref[rows, :]
        sin = sin_ref[rows, :]
        q = _rope(q, cos, sin) * (GLA_KEY // SCAN_HEADS) ** -0.5
        k = _rope(k, cos, sin)
        lo = direction * GLA_KEY
        z = _dot(gk.astype(BF16), wgk_ref[:, lo:lo + GLA_KEY]) + bgk_ref[:, lo:lo + GLA_KEY]
        log_sig = jnp.minimum(z, 0.0) - jnp.log(1.0 + jnp.exp(-jnp.abs(z)))
        return q, k, v, log_sig / GLA_GATE_NORM

    def hgrn_inputs(in_ref, direction, rows):
        blk = in_ref[0, rows, W_GLA:W_GLA + W_HG]
        q = blk[:, 0:HGRN_WIDTH] * HEAD_DIM ** -0.5
        f_raw = blk[:, (1 + direction) * HGRN_WIDTH:(2 + direction) * HGRN_WIDTH]
        v = blk[:, 3 * HGRN_WIDTH:]
        lb = lb_ref[direction:direction + 1, :]
        f = lb + (1.0 - lb) * _sigmoid(f_raw)
        return q, 1.0 - f, v, jnp.log(f)

    jobs, dests = [], []
    for s in range(SCAN_SUB):
        rows_f = pl.ds(s * SCAN_BLOCK, SCAN_BLOCK)
        rows_b = pl.ds((SCAN_SUB - 1 - s) * SCAN_BLOCK, SCAN_BLOCK)
        jobs += [
            _scan_direction(functools.partial(gla_inputs, in_f_ref, cos_f_ref, sin_f_ref, 0, rows_f),
                            cum_f_ref[...],sgf_ref, reverse=False),
            _scan_direction(functools.partial(hgrn_inputs, in_f_ref, 0, rows_f),
                            cum_f_ref[...],shf_ref, reverse=False),
            _scan_direction(functools.partial(gla_inputs, in_b_ref, cos_b_ref, sin_b_ref, 1, rows_b),
                            cum_b_ref[...],sgb_ref, reverse=True),
            _scan_direction(functools.partial(hgrn_inputs, in_b_ref, 1, rows_b),
                            cum_b_ref[...],shb_ref, reverse=True),
        ]
        dests += [(of_ref, rows_f, gla_cols), (of_ref, rows_f, hg_cols),
                  (ob_ref, rows_b, gla_cols), (ob_ref, rows_b, hg_cols)]
    for (ref, rows, cols), val in zip(dests, _run_interleaved(jobs)):
        ref[0, rows, cols] = val.astype(ref.dtype)


def _scans(scan_in, cos_t, sin_t, wgk, bgk, lb, consts):
    n_batch, t_all, _ = scan_in.shape
    step_rows = SCAN_SUB * SCAN_BLOCK
    assert CTX_LEN % step_rows == 0 and t_all % step_rows == 0
    n_blocks = t_all // step_rows
    n_ctx_blocks = CTX_LEN // step_rows

    def fwd(i):
        return i

    def bwd(i):
        return jnp.where(i < n_ctx_blocks, n_ctx_blocks - 1 - i, n_blocks - 1 + n_ctx_blocks - i)

    def tok_spec(width, order):
        return pl.BlockSpec((1, step_rows, width), lambda b, i: (b, order(i), 0))

    def tab_spec(order):
        return pl.BlockSpec((step_rows, LANES), lambda b, i: (order(i), 0))

    def const_spec(arr):
        return pl.BlockSpec(arr.shape, lambda b, i: (0,) * arr.ndim)

    const_arrays = [wgk, bgk, lb, consts["cum_f"], consts["cum_b"]]
    out_shape = jax.ShapeDtypeStruct((n_batch, t_all, GLA_WIDTH + HGRN_WIDTH), BF16)
    return pl.pallas_call(
        functools.partial(_scan_kernel, n_ctx_blocks),
        grid=(n_batch, n_blocks),
        in_specs=[tok_spec(W_GLA + W_HG, fwd), tok_spec(W_GLA + W_HG, bwd),
                  tab_spec(fwd), tab_spec(fwd), tab_spec(bwd), tab_spec(bwd)]
                 + [const_spec(a) for a in const_arrays],
        out_specs=[tok_spec(GLA_WIDTH + HGRN_WIDTH, fwd), tok_spec(GLA_WIDTH + HGRN_WIDTH, bwd)],
        out_shape=[out_shape] * 2,
        scratch_shapes=[pltpu.VMEM((HEAD_DIM, GLA_KEY), F32), pltpu.VMEM((HEAD_DIM, GLA_KEY), F32),
                        pltpu.VMEM((HEAD_DIM, HGRN_WIDTH), F32), pltpu.VMEM((HEAD_DIM, HGRN_WIDTH), F32)],
        compiler_params=pltpu.CompilerParams(
            dimension_semantics=("parallel", "arbitrary"), vmem_limit_bytes=VMEM_LIMIT),
        name="bidirectional_scans",
    )(scan_in, scan_in, cos_t, sin_t, cos_t, sin_t, *const_arrays)


def _rope_tables(t_lat):
    half = GLA_KEY // SCAN_HEADS // 2
    inv_freq = 1.0 / (ROPE_BASE ** (np.arange(0, half, 2, dtype=np.float32) / half))
    t = np.arange(t_lat)
    lane = np.arange(LANES) % (2 * half)
    pos = np.where(lane[None, :] < half, (t // GRID_W)[:, None], (t % GRID_W)[:, None]).astype(np.float32)
    ang = (pos * inv_freq[(lane % half) % (half // 2)][None, :]).astype(np.float32)
    first = ((lane % half) < half // 2)[None, :]
    cos = np.cos(ang)
    sin = np.where(first, -np.sin(ang), np.sin(ang))
    cos = np.concatenate([np.ones((CTX_LEN, LANES), np.float32), cos], axis=0)
    sin = np.concatenate([np.zeros((CTX_LEN, LANES), np.float32), sin], axis=0)
    return jnp.asarray(cos, F32), jnp.asarray(sin, F32)


N_MIX_REFS = 9


def _mix_tile(x, row, ona_ref, of_ref, ob_ref, gate_ref, mod_ref, gnw_ref, hnw_ref, bd_ref, w_ref):
    gt = mod_ref[pl.ds(row, 1), :][:, 2 * D_MODEL:]
    bd = bd_ref[...]

    def head_norm(o, w):
        sq = o * o
        hi = sq.astype(BF16)
        lo = (sq - hi.astype(F32)).astype(BF16)
        ms = _dot(hi, bd) + _dot(lo, bd)
        return o * lax.rsqrt(ms + NORM_EPS) * w

    o_scan = of_ref[0].astype(F32) + ob_ref[0].astype(F32)
    o_gla = head_norm(o_scan[:, :GLA_WIDTH], gnw_ref[...])
    o_hg = head_norm(o_scan[:, GLA_WIDTH:], hnw_ref[...])
    o_cat = jnp.concatenate([ona_ref[0].astype(F32), o_gla, o_hg], axis=1)
    act = o_cat * _silu(gate_ref[0].astype(F32))
    y = _dot(act.astype(BF16), w_ref[...])
    return x + gt * y


def _final_out_kernel(x_ref, *refs):
    xn = _mix_tile(x_ref[0], pl.program_id(0), *refs[:N_MIX_REFS])
    fw_ref, out_ref = refs[N_MIX_REFS:]
    ms = jnp.mean(xn * xn, axis=-1, keepdims=True)
    out_ref[0] = xn * lax.rsqrt(ms + NORM_EPS) * fw_ref[...]


def _out_in_kernel(n_streams, n_batch, *refs):
    x, row = _token_tile(refs[:n_streams], n_batch)
    xn = _mix_tile(x, row, *refs[n_streams:n_streams + N_MIX_REFS])
    rest = refs[n_streams + N_MIX_REFS:]
    proj_in, stream_ref, proj_out = rest[:3], rest[3], rest[4:]
    stream_ref[0] = xn
    _project_tile(xn, row, *proj_in, *proj_out)


def _token_spec(width, tile, first_token):
    if first_token % tile == 0:
        return pl.BlockSpec((1, tile, width), lambda b, t: (b, t + first_token // tile, 0))
    return pl.BlockSpec((pl.Element(1), pl.Element(tile), pl.Element(width)),
                        lambda b, t: (b, pl.multiple_of(first_token + t * tile, TOKEN_TILE), 0))


def _mix_io(o_na, scans, gates, mod_l, gla_nw, hg_nw, bd, w_out_b, layer, tile=TOKEN_TILE, first_token=0):
    def tok(width):
        return _token_spec(width, tile, first_token)

    def const_spec(arr):
        return pl.BlockSpec(arr.shape, lambda b, t: (0,) * arr.ndim)

    consts = [mod_l, gla_nw, hg_nw, bd]
    specs = ([pl.BlockSpec((1, tile, NA_WIDTH), lambda b, t: (b, t, 0))]
             + [tok(GLA_WIDTH + HGRN_WIDTH)] * 2
             + [tok(W_GATE)] + [const_spec(a) for a in consts]
             + [pl.BlockSpec((None,) + w_out_b.shape[1:], lambda b, t: (layer, 0, 0))])
    return specs, [o_na, *scans, gates, *consts, w_out_b]


def _final_out(streams, mix_args, final_w, n_batch, t_all):
    (stream,) = streams
    t_lat = t_all - CTX_LEN
    assert t_lat % FINAL_TILE == 0
    mix_specs, mix_arrays = _mix_io(*mix_args, tile=FINAL_TILE, first_token=CTX_LEN)
    return pl.pallas_call(
        _final_out_kernel,
        grid=(n_batch, t_lat // FINAL_TILE),
        in_specs=[_token_spec(D_MODEL, FINAL_TILE, CTX_LEN)] + mix_specs
                 + [pl.BlockSpec((1, D_MODEL), lambda b, t: (0, 0))],
        out_specs=pl.BlockSpec((1, FINAL_TILE, D_MODEL), lambda b, t: (b, t, 0)),
        out_shape=jax.ShapeDtypeStruct((n_batch, t_lat, D_MODEL), F32),
        compiler_params=pltpu.CompilerParams(
            dimension_semantics=("parallel", "parallel"), vmem_limit_bytes=VMEM_LIMIT),
        name="out_proj_final",
    )(stream, *mix_arrays, final_w)


def _out_in_proj(streams, mix_args, mod_next, norm_w_next, w_in_p, next_layer, n_batch, t_all):
    mix_specs, mix_arrays = _mix_io(*mix_args)
    proj_specs, out_specs, out_shape = _projection_io(mod_next, next_layer, n_batch, t_all)
    stream_spec = pl.BlockSpec((1, TOKEN_TILE, D_MODEL), lambda b, t: (b, t, 0))
    stream_shape = jax.ShapeDtypeStruct((n_batch, t_all, D_MODEL), F32)
    outs = pl.pallas_call(
        functools.partial(_out_in_kernel, len(streams), n_batch),
        grid=(n_batch, t_all // TOKEN_TILE),
        in_specs=_stream_specs(len(streams)) + mix_specs + proj_specs,
        out_specs=[stream_spec] + out_specs,
        out_shape=[stream_shape] + out_shape,
        compiler_params=pltpu.CompilerParams(
            dimension_semantics=("parallel", "parallel"), vmem_limit_bytes=VMEM_LIMIT),
        name="out_in_proj",
    )(*streams, *mix_arrays, mod_next, norm_w_next, w_in_p)
    return outs[0], outs[1:]


def _pad_w_in(w):
    assert w.shape[-1] == IN_OFFS[-1]
    return jnp.pad(w.astype(BF16), [(0, 0)] * (w.ndim - 1) + [(0, W_IN_PAD - w.shape[-1])])


def kernel(x, c, ctx, c_ctx, ada_w, ada_b, norm_w, w_in, na_rpb, gla_w_gk, gla_b_gk, gla_norm_w,
           hgrn_lower_bounds, hgrn_norm_w, w_out, final_norm_w):
    n_batch, t_lat, _ = x.shape
    depth = ada_w.shape[0]
    t_all = CTX_LEN + t_lat
    assert ctx.shape[1] == CTX_LEN and t_lat % TOKEN_TILE == 0 and t_lat // GRID_W >= NA_KEY_ROWS

    lbs = jnp.cumsum(jax.nn.softmax(hgrn_lower_bounds.astype(F32), axis=0), axis=0)
    lbs = lbs - lbs[0:1]
    mod_rows = -(-(n_batch + 1) // 8) * 8
    cc = jnp.zeros((mod_rows, D_MODEL), F32).at[:n_batch].set(c).at[n_batch].set(c_ctx)
    mod = _modulation(cc, ada_w, ada_b)

    consts = {k_: jnp.asarray(v_, BF16) for k_, v_ in _scan_constants().items()}
    cos_t, sin_t = _rope_tables(t_lat)
    head_of = np.arange(GLA_WIDTH) // HEAD_DIM
    bd = jnp.asarray((head_of[:, None] == head_of[None, :]).astype(np.float32) / HEAD_DIM, BF16)

    w_in_p = _pad_w_in(w_in)
    w_out_b = w_out.astype(BF16)
    rpb_tiles = _na_bias_tiles(na_rpb)
    wgk = jnp.zeros((depth, LANES, 2 * GLA_KEY), F32)
    wgk = wgk.at[:, 0:GLA_GK_RANK, 0:GLA_KEY].set(gla_w_gk[:, 0])
    wgk = wgk.at[:, GLA_GK_RANK:2 * GLA_GK_RANK, GLA_KEY:].set(gla_w_gk[:, 1]).astype(BF16)

    streams = [ctx, x]
    projections = _in_proj(streams, mod[0], norm_w[0].reshape(1, D_MODEL), w_in_p, 0, n_batch, t_all)
    for layer in range(depth):
        last = layer == depth - 1
        bgk = gla_b_gk[layer].reshape(1, 2 * GLA_KEY)
        gla_nw = jnp.tile(gla_norm_w[layer], SCAN_HEADS).reshape(1, GLA_WIDTH)
        hg_nw = jnp.tile(hgrn_norm_w[layer], SCAN_HEADS).reshape(1, HGRN_WIDTH)

        q, k, vt, gates, scan_in = projections
        o_na = _neighborhood_attention(q, k, vt, rpb_tiles, layer, with_ctx_queries=not last)
        scans = _scans(scan_in, cos_t, sin_t, wgk[layer], bgk, lbs[layer], consts)
        mix_args = (o_na, scans, gates, mod[layer], gla_nw, hg_nw, bd, w_out_b, layer)
        if last:
            return _final_out(streams, mix_args, final_norm_w.reshape(1, D_MODEL), n_batch, t_all)
        stream, projections = _out_in_proj(streams, mix_args, mod[layer + 1],
                                           norm_w[layer + 1].reshape(1, D_MODEL), w_in_p, layer + 1,
                                           n_batch, t_all)
        streams = [stream]
```

```python
import functools

import numpy as np
import jax
import jax.numpy as jnp
from jax import lax
from jax.experimental import pallas as pl
from jax.experimental.pallas import tpu as pltpu

F32 = jnp.float32
BF16 = jnp.bfloat16

D_MODEL = 1024
CTX_LEN = 256
GRID_W = 64
NORM_EPS = 1e-6
HEAD_DIM = 64
NA_WIDTH = 512
NA_HEADS = 8
NA_WIN_H = 8
NA_WIN_W = 16
GLA_WIDTH = 256
GLA_KEY = 128
GLA_GK_RANK = 16
GLA_GATE_NORM = 16.0
HGRN_WIDTH = 256
SCAN_HEADS = 4
CHUNK = 16
ROPE_BASE = 10000.0

LOG2_E = float(np.log2(np.e))
NA_Q_SCALE = HEAD_DIM ** -0.5 * LOG2_E

LANES = 128
TOKEN_TILE = 256
NA_PIPELINE_LAG = 3
FINAL_TILE = 512
SCAN_BLOCK = 128
SCAN_SUB = 2
CHUNKS_PER_BLOCK = SCAN_BLOCK // CHUNK
NA_Q_ROWS = TOKEN_TILE // GRID_W
NA_KEY_ROWS = NA_Q_ROWS + NA_WIN_H
NA_KEYS = NA_KEY_ROWS * GRID_W
assert NA_KEYS % LANES == 0 and (NA_Q_ROWS * GRID_W) % LANES == 0 and NA_WIN_H % 4 == 0
VMEM_LIMIT = 56 * 1024 * 1024

IN_NAMES = ("na_q", "na_k", "na_v", "na_g", "gla_q", "gla_k", "gla_v", "gla_gk", "gla_g",
            "hg_q", "hg_ff", "hg_fb", "hg_i", "hg_g")
IN_SPLITS = (512, 512, 512, 512, 128, 128, 256, 32, 256, 256, 256, 256, 256, 256)
IN_OFFS = tuple(int(v) for v in np.cumsum((0,) + IN_SPLITS))
W_GATE = NA_WIDTH + GLA_WIDTH + HGRN_WIDTH
W_GLA = GLA_WIDTH + 2 * GLA_KEY + LANES
W_HG = 4 * HGRN_WIDTH
W_IN_PAD = -(-IN_OFFS[-1] // LANES) * LANES


def _dot(a, b):
    return jnp.dot(a, b, preferred_element_type=F32)


def _dot_nt(a, b):
    return lax.dot_general(a, b, (((1,), (1,)), ((), ())), preferred_element_type=F32)


def _dot_tn(a, b):
    return lax.dot_general(a, b, (((0,), (0,)), ((), ())), preferred_element_type=F32)


def _sigmoid(x):
    return 1.0 / (1.0 + jnp.exp(-x))


def _silu(x):
    return x * _sigmoid(x)


def _mod_kernel(c_ref, w_ref, b_ref, o_ref):
    s = _silu(c_ref[...]).astype(BF16)
    o_ref[0] = _dot(s, w_ref[0].astype(BF16)) + b_ref[0]


def _modulation(cc, ada_w, ada_b):
    depth = ada_w.shape[0]
    rows = cc.shape[0]
    return pl.pallas_call(
        _mod_kernel,
        grid=(depth, 3),
        in_specs=[
            pl.BlockSpec((rows, D_MODEL), lambda l, j: (0, 0)),
            pl.BlockSpec((1, D_MODEL, D_MODEL), lambda l, j: (l, 0, j)),
            pl.BlockSpec((1, 1, D_MODEL), lambda l, j: (l, 0, j)),
        ],
        out_specs=pl.BlockSpec((1, rows, D_MODEL), lambda l, j: (l, 0, j)),
        out_shape=jax.ShapeDtypeStruct((depth, rows, 3 * D_MODEL), F32),
        compiler_params=pltpu.CompilerParams(
            dimension_semantics=("parallel", "parallel"), vmem_limit_bytes=VMEM_LIMIT),
        name="adaln_mod",
    )(cc, ada_w, ada_b.reshape(depth, 1, 3 * D_MODEL))


def _token_tile(x_refs, n_batch, tile_off=0):
    t = pl.program_id(1) + tile_off
    is_ctx = t < CTX_LEN // TOKEN_TILE
    x = jnp.where(is_ctx, x_refs[0][0], x_refs[1][0]) if len(x_refs) == 2 else x_refs[0][0]
    return x, jnp.where(is_ctx, n_batch, pl.program_id(0))


def _stream_specs(n_streams, tile_off=0):
    n_ctx_tiles = CTX_LEN // TOKEN_TILE
    if n_streams == 2:
        return [pl.BlockSpec((1, TOKEN_TILE, D_MODEL), lambda b, t: (b, jnp.minimum(t, n_ctx_tiles - 1), 0)),
                pl.BlockSpec((1, TOKEN_TILE, D_MODEL), lambda b, t: (b, jnp.maximum(t - n_ctx_tiles, 0), 0))]
    return [pl.BlockSpec((1, TOKEN_TILE, D_MODEL), lambda b, t: (b, t + tile_off, 0))]


def _in_proj_kernel(n_streams, n_batch, *refs):
    x, row = _token_tile(refs[:n_streams], n_batch)
    _project_tile(x, row, *refs[n_streams:])


def _project_tile(x, row, mod_ref, nw_ref, w_ref, q_ref, k_ref, v_ref, gate_ref, scan_ref):
    m = mod_ref[pl.ds(row, 1), :]
    sh = m[:, :D_MODEL]
    sc = m[:, D_MODEL:2 * D_MODEL]
    ms = jnp.mean(x * x, axis=-1, keepdims=True)
    y = x * lax.rsqrt(ms + NORM_EPS) * nw_ref[...]
    h = y * (1.0 + sc) + sh
    p = _dot(h.astype(BF16), w_ref[...])
    for pr in range(NA_HEADS // 2):
        lo = pr * LANES
        q_ref[0, pr] = (p[:, lo:lo + LANES] * NA_Q_SCALE).astype(BF16)
        k_ref[0, pr] = p[:, NA_WIDTH + lo:NA_WIDTH + lo + LANES].astype(BF16)
        for s in range(TOKEN_TILE // LANES):
            v_tile = p[s * LANES:(s + 1) * LANES, 2 * NA_WIDTH + lo:2 * NA_WIDTH + lo + LANES]
            v_ref[0, pr, s] = jnp.transpose(v_tile).astype(BF16)
    off = dict(zip(IN_NAMES, IN_OFFS))
    tail = p[:, off["gla_gk"]:]
    aligned = pltpu.roll(tail, tail.shape[1] - IN_SPLITS[IN_NAMES.index("gla_gk")], 1)
    seg = lambda name: aligned[:, off[name] - off["gla_g"]:off[name] - off["gla_g"] + IN_SPLITS[IN_NAMES.index(name)]]
    gate_ref[0] = jnp.concatenate([p[:, off["na_g"]:off["gla_q"]], seg("gla_g"), seg("hg_g")],
                                  axis=1).astype(BF16)
    scan_ref[0] = jnp.concatenate(
        [p[:, off["gla_v"]:off["gla_gk"]], p[:, off["gla_q"]:off["gla_v"]], tail[:, :LANES],
         aligned[:, off["hg_q"] - off["gla_g"]:off["hg_g"] - off["gla_g"]]], axis=1)


def _projection_io(mod_l, layer, n_batch, t_all):
    in_specs = [
        pl.BlockSpec(mod_l.shape, lambda b, t: (0, 0)),
        pl.BlockSpec((1, D_MODEL), lambda b, t: (0, 0)),
        pl.BlockSpec((None, D_MODEL, W_IN_PAD), lambda b, t: (layer, 0, 0), pipeline_mode=pl.Buffered(1)),
    ]
    pair_spec = pl.BlockSpec((1, NA_HEADS // 2, TOKEN_TILE, LANES), lambda b, t: (b, 0, t, 0))
    pair_shape = jax.ShapeDtypeStruct((n_batch, NA_HEADS // 2, t_all, LANES), BF16)
    vt_spec = pl.BlockSpec((1, NA_HEADS // 2, TOKEN_TILE // LANES, LANES, LANES), lambda b, t: (b, 0, t, 0, 0))
    vt_shape = jax.ShapeDtypeStruct((n_batch, NA_HEADS // 2, t_all // LANES, LANES, LANES), BF16)

    def slab(width, dtype):
        return (pl.BlockSpec((1, TOKEN_TILE, width), lambda b, t: (b, t, 0)),
                jax.ShapeDtypeStruct((n_batch, t_all, width), dtype))

    gate_spec, gate_shape = slab(W_GATE, BF16)
    scan_spec, scan_shape = slab(W_GLA + W_HG, F32)
    return (in_specs, [pair_spec, pair_spec, vt_spec, gate_spec, scan_spec],
            [pair_shape, pair_shape, vt_shape, gate_shape, scan_shape])


def _in_proj(streams, mod_l, norm_w_l, w_in_p, layer, n_batch, t_all):
    in_specs, out_specs, out_shape = _projection_io(mod_l, layer, n_batch, t_all)
    return pl.pallas_call(
        functools.partial(_in_proj_kernel, len(streams), n_batch),
        grid=(n_batch, t_all // TOKEN_TILE),
        in_specs=_stream_specs(len(streams)) + in_specs,
        out_specs=out_specs,
        out_shape=out_shape,
        compiler_params=pltpu.CompilerParams(
            dimension_semantics=("parallel", "parallel"), vmem_limit_bytes=VMEM_LIMIT),
        name="in_proj",
    )(*streams, mod_l, norm_w_l, w_in_p)


def _na_bias_cases():
    cases = []
    for key_row_ok, rel in (
        (lambda lr, lk: lk < NA_WIN_H, lambda lr, lk: lk - lr + NA_WIN_H - 1),
        (lambda lr, lk: lr <= lk < lr + NA_WIN_H, lambda lr, lk: lk - lr + NA_WIN_H // 2 - 1),
        (lambda lr, lk: lk >= NA_KEY_ROWS - NA_WIN_H, lambda lr, lk: lk - lr - 1),
    ):
        cases.append([[rel(lr, lk) if key_row_ok(lr, lk) else None for lk in range(NA_KEY_ROWS)]
                      for lr in range(NA_Q_ROWS)])
    return cases


def _na_kernel(tile_off, q_ref, k_ref, vt_ref, rt_ref, o_ref, bias_ref):
    tb = pl.program_id(1) + tile_off
    n_pairs = q_ref.shape[1]
    lane = lax.broadcasted_iota(jnp.int32, (1, LANES), 1)
    q_scale = [jnp.where((lane // HEAD_DIM) == hh, 1.0, 0.0).astype(BF16) for hh in range(2)]
    heads = [(p, hh) for p in range(n_pairs) for hh in range(2)]
    ctx_tiles = CTX_LEN // LANES

    @pl.when((pl.program_id(0) == 0) & (pl.program_id(1) == 0))
    def _build_bias_tables():
        minus_inf = jnp.full((GRID_W, GRID_W), -jnp.inf, F32)

        def per_head(h, carry):
            for case, rel in enumerate(_na_bias_cases()):
                for lk in range(NA_KEY_ROWS):
                    for lr in range(0, NA_Q_ROWS, 2):
                        tiles = [minus_inf if rel[r][lk] is None else rt_ref[h, rel[r][lk]] for r in (lr, lr + 1)]
                        bias_ref[case, h, lk * GRID_W:(lk + 1) * GRID_W, lr * GRID_W:(lr + 2) * GRID_W] = (
                            jnp.concatenate(tiles, axis=1))
            return carry

        lax.fori_loop(0, NA_HEADS, per_head, 0)

    def vt_tiles(p, first, count):
        tiles = vt_ref[0, p, pl.ds(first, count)]
        return jnp.concatenate([tiles[i] for i in range(count)], axis=1)

    def finish(outs):
        for p in range(n_pairs):
            o_t = jnp.concatenate([outs[(p, 0)], outs[(p, 1)]], axis=0)
            o_ref[0, :, p * LANES:(p + 1) * LANES] = jnp.transpose(o_t).astype(o_ref.dtype)

    def pipelined(scores_of, attend):
        outs, issued = {}, {}
        for i in range(len(heads) + NA_PIPELINE_LAG):
            if i < len(heads):
                issued[i] = scores_of(*heads[i])
            j = i - NA_PIPELINE_LAG
            if j >= 0:
                outs[heads[j]] = attend(*heads[j], *issued.pop(j))
        finish(outs)

    def softmax_pv(hh, vt_parts, s_parts):
        m = functools.reduce(jnp.maximum, [jnp.max(s, axis=0, keepdims=True) for s in s_parts])
        o_t = None
        for vt, s in zip(vt_parts, s_parts):
            ones = jnp.ones((HEAD_DIM, vt.shape[1]), BF16)
            lhs = jnp.concatenate([vt[:HEAD_DIM], ones] if hh == 0 else [ones, vt[HEAD_DIM:]], axis=0)
            part = _dot(lhs, jnp.exp2(s - m).astype(BF16))
            o_t = part if o_t is None else o_t + part
        own, other = (0, HEAD_DIM) if hh == 0 else (HEAD_DIM, 0)
        return o_t[own:own + HEAD_DIM] / o_t[other:other + 1]

    @pl.when(tb == 0)
    def _ctx_queries():
        def scores_of(p, hh):
            return (_dot_nt(k_ref[0, p, 0:CTX_LEN, :], q_ref[0, p] * q_scale[hh]),)

        def attend(p, hh, s):
            return softmax_pv(hh, [vt_tiles(p, 0, ctx_tiles)], [s])

        pipelined(scores_of, attend)

    @pl.when(tb > 0)
    def _latent_queries():
        g = tb - 1
        n_groups = pl.num_programs(1) - 1 + tile_off
        ws = jnp.clip(NA_Q_ROWS * g - NA_WIN_H // 2, 0, GRID_W - NA_KEY_ROWS)
        case = (g > 0).astype(jnp.int32) + (g == n_groups - 1).astype(jnp.int32)
        start = pl.multiple_of(CTX_LEN + ws * GRID_W, LANES)
        first_tile = ctx_tiles + ws * GRID_W // LANES

        def scores_of(p, hh):
            qh = q_ref[0, p] * q_scale[hh]
            s_nb = _dot_nt(k_ref[0, p, pl.ds(start, NA_KEYS), :], qh) + bias_ref[case, 2 * p + hh]
            return s_nb, _dot_nt(k_ref[0, p, 0:CTX_LEN, :], qh)

        def attend(p, hh, s_nb, s_cx):
            return softmax_pv(hh, [vt_tiles(p, first_tile, NA_KEYS // LANES), vt_tiles(p, 0, ctx_tiles)],
                              [s_nb, s_cx])

        pipelined(scores_of, attend)


def _neighborhood_attention(q, k, vt, rpb_tiles, layer, with_ctx_queries):
    n_batch, n_pairs, t_all, _ = q.shape
    n_tiles = t_all // TOKEN_TILE
    tile_off = 0 if with_ctx_queries else CTX_LEN // TOKEN_TILE
    n_steps = n_tiles - tile_off
    t_out = n_steps * TOKEN_TILE
    return pl.pallas_call(
        functools.partial(_na_kernel, tile_off),
        grid=(n_batch, n_steps),
        in_specs=[
            pl.BlockSpec((1, n_pairs, TOKEN_TILE, LANES), lambda b, g: (b, 0, g + tile_off, 0)),
            pl.BlockSpec((1, n_pairs, t_all, LANES), lambda b, g: (b, 0, 0, 0)),
            pl.BlockSpec((1, n_pairs, t_all // LANES, LANES, LANES), lambda b, g: (b, 0, 0, 0, 0)),
            pl.BlockSpec((None,) + rpb_tiles.shape[1:], lambda b, g: (layer, 0, 0, 0, 0),
                         pipeline_mode=pl.Buffered(1)),
        ],
        out_specs=pl.BlockSpec((1, TOKEN_TILE, NA_WIDTH), lambda b, g: (b, g, 0)),
        out_shape=jax.ShapeDtypeStruct((n_batch, t_out, NA_WIDTH), BF16),
        scratch_shapes=[pltpu.VMEM((3, NA_HEADS, NA_KEYS, TOKEN_TILE), F32)],
        compiler_params=pltpu.CompilerParams(
            dimension_semantics=("arbitrary", "arbitrary"),
            vmem_limit_bytes=VMEM_LIMIT),
        name="neighborhood_attention",
    )(q, k, vt, rpb_tiles)


def _na_bias_tiles(rpb):
    col = np.arange(GRID_W)
    col_start = np.clip(col - NA_WIN_W // 2, 0, GRID_W - NA_WIN_W)
    col_in = (col[:, None] >= col_start[None, :]) & (col[:, None] < col_start[None, :] + NA_WIN_W)
    col_idx = np.clip(col[:, None] - col[None, :], 1 - NA_WIN_W, NA_WIN_W - 1) + NA_WIN_W - 1
    one_hot = (col_idx[None] == np.arange(2 * NA_WIN_W - 1)[:, None, None]).astype(np.float32)
    tiles = jnp.einsum("lhrw,wkj->lhrkj", rpb.astype(F32), jnp.asarray(one_hot),
                       precision=lax.Precision.HIGHEST)
    return jnp.where(col_in, tiles * LOG2_E, -jnp.inf)


def _scan_constants():
    idx = np.arange(SCAN_BLOCK)
    c = idx // CHUNK
    pos = idx % CHUNK
    same = c[:, None] == c[None, :]
    return {"cum_f": (same & (pos[None, :] <= pos[:, None])).astype(np.float32),
            "cum_b": (same & (pos[None, :] >= pos[:, None])).astype(np.float32)}


def _block_matrix(n_rows, n_cols, block_shape, dtype, block_at):
    zero = jnp.zeros(block_shape, dtype)
    rows = []
    for r in range(n_rows):
        blocks = [block_at(r, c) for c in range(n_cols)]
        rows.append(jnp.concatenate([zero if b is None else b for b in blocks], axis=1))
    return jnp.concatenate(rows, axis=0)


def _run_interleaved(stagewise):
    results = [None] * len(stagewise)
    live = list(range(len(stagewise)))
    while live:
        for i in list(live):
            try:
                next(stagewise[i])
            except StopIteration as done:
                results[i] = done.value
                live.remove(i)
    return results


def _scan_direction(load_inputs, cum, st_ref, reverse):
    q, k, v, lf = load_inputs()
    n = SCAN_BLOCK
    hd = q.shape[1]
    dk = hd // SCAN_HEADS
    pack = 2
    low_half = lax.broadcasted_iota(jnp.int32, (1, LANES), 1) < HEAD_DIM
    lane_k = lax.broadcasted_iota(jnp.int32, (1, hd), 1) // dk
    lane_v = lax.broadcasted_iota(jnp.int32, (1, SCAN_HEADS * HEAD_DIM), 1) // HEAD_DIM

    def lane_mask(cond):
        return jnp.where(cond, 1.0, 0.0).astype(BF16)

    half_masks = {True: lane_mask(low_half), False: lane_mask(jnp.logical_not(low_half))}
    k_masks = [lane_mask(lane_k == h) for h in range(SCAN_HEADS)]
    v_masks = [lane_mask(lane_v == h) for h in range(SCAN_HEADS)]

    def keep_half(x, low):
        return x * half_masks[low]

    hi = lf.astype(BF16)
    mid = (lf - hi.astype(F32)).astype(BF16)
    g = _dot(cum, hi) + _dot(cum, mid)
    yield
    ref_pos, last_pos = (CHUNK // 2, 0) if reverse else (CHUNK // 2 - 1, CHUNK - 1)
    to_ref, to_last, decay = [], [], []
    for c in range(CHUNKS_PER_BLOCK):
        gc = g[c * CHUNK:(c + 1) * CHUNK]
        g_last = gc[last_pos:last_pos + 1]
        to_ref.append(gc - gc[ref_pos:ref_pos + 1])
        to_last.append(g_last - gc)
        decay.append(jnp.exp(g_last))
    x_ref = jnp.concatenate(to_ref, axis=0)
    a = jnp.exp(x_ref)
    a_inv = jnp.exp(-x_ref)
    e_in = jnp.exp(g)
    e_out = jnp.exp(jnp.concatenate(to_last, axis=0))

    q_t = (q * a).astype(BF16)
    k_t_tr = jnp.transpose(k * a_inv).astype(BF16)
    q_in = (q * e_in).astype(BF16)
    k_out = (k * e_out).astype(BF16)
    v_b = v.astype(BF16)

    k_blk = _block_matrix(SCAN_HEADS, SCAN_HEADS, (dk, n), BF16,
                          lambda r, c: k_t_tr[r * dk:(r + 1) * dk] if r == c else None)
    scores = _dot(q_t, k_blk)

    v_tr = jnp.transpose(v).astype(BF16)
    token_pair = lax.broadcasted_iota(jnp.int32, (1, n), 1) // (CHUNK * pack)
    pair_masks = [lane_mask(token_pair == m) for m in range(CHUNKS_PER_BLOCK // pack)]
    heads_per_group = LANES // dk
    incs = []
    for grp in range(hd // LANES):
        heads = range(grp * heads_per_group, (grp + 1) * heads_per_group)
        v_blk = jnp.concatenate(
            [jnp.concatenate([v_tr[h * HEAD_DIM:(h + 1) * HEAD_DIM] * pm for h in heads], axis=1)
             for pm in pair_masks], axis=0)
        k_rows = []
        for h in heads:
            k_h = (k_out * k_masks[h])[:, grp * LANES:(grp + 1) * LANES]
            k_rows.append(_block_matrix(
                CHUNKS_PER_BLOCK, pack, (CHUNK, LANES), BF16,
                lambda c, par, k_h=k_h: k_h[c * CHUNK:(c + 1) * CHUNK] if c % pack == par else None))
        incs.append(_dot(v_blk, jnp.concatenate(k_rows, axis=0)))
    yield

    tri = jnp.concatenate([cum.astype(F32)] * SCAN_HEADS, axis=1) > 0.5
    p = jnp.where(tri, scores, 0.0).astype(BF16)
    v_stack = jnp.concatenate([v_b * v_masks[h] for h in range(SCAN_HEADS)], axis=0)
    o_intra = _dot(p, v_stack)

    st = st_ref[...]
    states = [None] * CHUNKS_PER_BLOCK
    order = range(CHUNKS_PER_BLOCK - 1, -1, -1) if reverse else range(CHUNKS_PER_BLOCK)
    for c in order:
        states[c] = st
        m, par = divmod(c, pack)
        inc_c = jnp.concatenate([inc[m * HEAD_DIM:(m + 1) * HEAD_DIM, par * LANES:(par + 1) * LANES]
                                 for inc in incs], axis=1)
        st = decay[c] * st + inc_c
    st_ref[...] = st

    q_blk = _block_matrix(CHUNKS_PER_BLOCK, CHUNKS_PER_BLOCK, (CHUNK, hd), BF16,
                          lambda r, c: q_in[r * CHUNK:(r + 1) * CHUNK] if r == c else None)
    s_parts = []
    for s in states:
        s_tr = jnp.transpose(jnp.concatenate([s, s], axis=0)).astype(BF16)

        def s_block(h, j, s_tr=s_tr):
            return keep_half(s_tr[h * dk:(h + 1) * dk], h % 2 == 0) if h // 2 == j else None

        s_parts.append(_block_matrix(SCAN_HEADS, SCAN_HEADS // 2, (dk, LANES), BF16, s_block))
    return o_intra + _dot(q_blk, jnp.concatenate(s_parts, axis=0))


def _rope(x, cos, sin_signed):
    lane = lax.broadcasted_iota(jnp.int32, (1, x.shape[1]), 1)
    first = (lane % 16) < 8
    partner = jnp.where(first, pltpu.roll(x, x.shape[1] - 8, 1), pltpu.roll(x, 8, 1))
    return x * cos + partner * sin_signed


def _scan_kernel(n_ctx_blocks,
                 in_f_ref, in_b_ref,
                 cos_f_ref, sin_f_ref, cos_b_ref, sin_b_ref,
                 wgk_ref, bgk_ref, lb_ref, cum_f_ref, cum_b_ref,
                 of_ref, ob_ref,
                 sgf_ref, sgb_ref, shf_ref, shb_ref):
    del n_ctx_blocks
    gla_cols, hg_cols = pl.ds(0, GLA_WIDTH), pl.ds(GLA_WIDTH, HGRN_WIDTH)

    @pl.when(pl.program_id(1) == 0)
    def _reset():
        for ref in (sgf_ref, sgb_ref, shf_ref, shb_ref):
            ref[...] = jnp.zeros_like(ref)

    def gla_inputs(in_ref, cos_ref, sin_ref, direction, rows):
        blk = in_ref[0, rows, 0:W_GLA]
        v = blk[:, 0:GLA_WIDTH]
        q = blk[:, GLA_WIDTH:GLA_WIDTH + GLA_KEY]
        k = blk[:, GLA_WIDTH + GLA_KEY:GLA_WIDTH + 2 * GLA_KEY]
        gk = blk[:, GLA_WIDTH + 2 * GLA_KEY:]
        cos = cos_ref[rows, :]
        sin = sin_ref[rows, :]
        q = _rope(q, cos, sin) * (GLA_KEY // SCAN_HEADS) ** -0.5
        k = _rope(k, cos, sin)
        lo = direction * GLA_KEY
        z = _dot(gk.astype(BF16), wgk_ref[:, lo:lo + GLA_KEY]) + bgk_ref[:, lo:lo + GLA_KEY]
        log_sig = jnp.minimum(z, 0.0) - jnp.log(1.0 + jnp.exp(-jnp.abs(z)))
        return q, k, v, log_sig / GLA_GATE_NORM

    def hgrn_inputs(in_ref, direction, rows):
        blk = in_ref[0, rows, W_GLA:W_GLA + W_HG]
        q = blk[:, 0:HGRN_WIDTH] * HEAD_DIM ** -0.5
        f_raw = blk[:, (1 + direction) * HGRN_WIDTH:(2 + direction) * HGRN_WIDTH]
        v = blk[:, 3 * HGRN_WIDTH:]
        lb = lb_ref[direction:direction + 1, :]
        f = lb + (1.0 - lb) * _sigmoid(f_raw)
        return q, 1.0 - f, v, jnp.log(f)

    jobs, dests = [], []
    for s in range(SCAN_SUB):
        rows_f = pl.ds(s * SCAN_BLOCK, SCAN_BLOCK)
        rows_b = pl.ds((SCAN_SUB - 1 - s) * SCAN_BLOCK, SCAN_BLOCK)
        jobs += [
            _scan_direction(functools.partial(gla_inputs, in_f_ref, cos_f_ref, sin_f_ref, 0, rows_f),
                            cum_f_ref[...],sgf_ref, reverse=False),
            _scan_direction(functools.partial(hgrn_inputs, in_f_ref, 0, rows_f),
                            cum_f_ref[...],shf_ref, reverse=False),
            _scan_direction(functools.partial(gla_inputs, in_b_ref, cos_b_ref, sin_b_ref, 1, rows_b),
                            cum_b_ref[...],sgb_ref, reverse=True),
            _scan_direction(functools.partial(hgrn_inputs, in_b_ref, 1, rows_b),
                            cum_b_ref[...],shb_ref, reverse=True),
        ]
        dests += [(of_ref, rows_f, gla_cols), (of_ref, rows_f, hg_cols),
                  (ob_ref, rows_b, gla_cols), (ob_ref, rows_b, hg_cols)]
    for (ref, rows, cols), val in zip(dests, _run_interleaved(jobs)):
        ref[0, rows, cols] = val.astype(ref.dtype)


def _scans(scan_in, cos_t, sin_t, wgk, bgk, lb, consts):
    n_batch, t_all, _ = scan_in.shape
    step_rows = SCAN_SUB * SCAN_BLOCK
    assert CTX_LEN % step_rows == 0 and t_all % step_rows == 0
    n_blocks = t_all // step_rows
    n_ctx_blocks = CTX_LEN // step_rows

    def fwd(i):
        return i

    def bwd(i):
        return jnp.where(i < n_ctx_blocks, n_ctx_blocks - 1 - i, n_blocks - 1 + n_ctx_blocks - i)

    def tok_spec(width, order):
        return pl.BlockSpec((1, step_rows, width), lambda b, i: (b, order(i), 0))

    def tab_spec(order):
        return pl.BlockSpec((step_rows, LANES), lambda b, i: (order(i), 0))

    def const_spec(arr):
        return pl.BlockSpec(arr.shape, lambda b, i: (0,) * arr.ndim)

    const_arrays = [wgk, bgk, lb, consts["cum_f"], consts["cum_b"]]
    out_shape = jax.ShapeDtypeStruct((n_batch, t_all, GLA_WIDTH + HGRN_WIDTH), BF16)
    return pl.pallas_call(
        functools.partial(_scan_kernel, n_ctx_blocks),
        grid=(n_batch, n_blocks),
        in_specs=[tok_spec(W_GLA + W_HG, fwd), tok_spec(W_GLA + W_HG, bwd),
                  tab_spec(fwd), tab_spec(fwd), tab_spec(bwd), tab_spec(bwd)]
                 + [const_spec(a) for a in const_arrays],
        out_specs=[tok_spec(GLA_WIDTH + HGRN_WIDTH, fwd), tok_spec(GLA_WIDTH + HGRN_WIDTH, bwd)],
        out_shape=[out_shape] * 2,
        scratch_shapes=[pltpu.VMEM((HEAD_DIM, GLA_KEY), F32), pltpu.VMEM((HEAD_DIM, GLA_KEY), F32),
                        pltpu.VMEM((HEAD_DIM, HGRN_WIDTH), F32), pltpu.VMEM((HEAD_DIM, HGRN_WIDTH), F32)],
        compiler_params=pltpu.CompilerParams(
            dimension_semantics=("parallel", "arbitrary"), vmem_limit_bytes=VMEM_LIMIT),
        name="bidirectional_scans",
    )(scan_in, scan_in, cos_t, sin_t, cos_t, sin_t, *const_arrays)


def _rope_tables(t_lat):
    half = GLA_KEY // SCAN_HEADS // 2
    inv_freq = 1.0 / (ROPE_BASE ** (np.arange(0, half, 2, dtype=np.float32) / half))
    t = np.arange(t_lat)
    lane = np.arange(LANES) % (2 * half)
    pos = np.where(lane[None, :] < half, (t // GRID_W)[:, None], (t % GRID_W)[:, None]).astype(np.float32)
    ang = (pos * inv_freq[(lane % half) % (half // 2)][None, :]).astype(np.float32)
    first = ((lane % half) < half // 2)[None, :]
    cos = np.cos(ang)
    sin = np.where(first, -np.sin(ang), np.sin(ang))
    cos = np.concatenate([np.ones((CTX_LEN, LANES), np.float32), cos], axis=0)
    sin = np.concatenate([np.zeros((CTX_LEN, LANES), np.float32), sin], axis=0)
    return jnp.asarray(cos, F32), jnp.asarray(sin, F32)


N_MIX_REFS = 9


def _mix_tile(x, row, ona_ref, of_ref, ob_ref, gate_ref, mod_ref, gnw_ref, hnw_ref, bd_ref, w_ref):
    gt = mod_ref[pl.ds(row, 1), :][:, 2 * D_MODEL:]
    bd = bd_ref[...]

    def head_norm(o, w):
        sq = o * o
        hi = sq.astype(BF16)
        lo = (sq - hi.astype(F32)).astype(BF16)
        ms = _dot(hi, bd) + _dot(lo, bd)
        return o * lax.rsqrt(ms + NORM_EPS) * w

    o_scan = of_ref[0].astype(F32) + ob_ref[0].astype(F32)
    o_gla = head_norm(o_scan[:, :GLA_WIDTH], gnw_ref[...])
    o_hg = head_norm(o_scan[:, GLA_WIDTH:], hnw_ref[...])
    o_cat = jnp.concatenate([ona_ref[0].astype(F32), o_gla, o_hg], axis=1)
    act = o_cat * _silu(gate_ref[0].astype(F32))
    y = _dot(act.astype(BF16), w_ref[...])
    return x + gt * y


def _final_out_kernel(x_ref, *refs):
    xn = _mix_tile(x_ref[0], pl.program_id(0), *refs[:N_MIX_REFS])
    fw_ref, out_ref = refs[N_MIX_REFS:]
    ms = jnp.mean(xn * xn, axis=-1, keepdims=True)
    out_ref[0] = xn * lax.rsqrt(ms + NORM_EPS) * fw_ref[...]


def _out_in_kernel(n_streams, n_batch, *refs):
    x, row = _token_tile(refs[:n_streams], n_batch)
    xn = _mix_tile(x, row, *refs[n_streams:n_streams + N_MIX_REFS])
    rest = refs[n_streams + N_MIX_REFS:]
    proj_in, stream_ref, proj_out = rest[:3], rest[3], rest[4:]
    stream_ref[0] = xn
    _project_tile(xn, row, *proj_in, *proj_out)


def _token_spec(width, tile, first_token):
    if first_token % tile == 0:
        return pl.BlockSpec((1, tile, width), lambda b, t: (b, t + first_token // tile, 0))
    return pl.BlockSpec((pl.Element(1), pl.Element(tile), pl.Element(width)),
                        lambda b, t: (b, pl.multiple_of(first_token + t * tile, TOKEN_TILE), 0))


def _mix_io(o_na, scans, gates, mod_l, gla_nw, hg_nw, bd, w_out_b, layer, tile=TOKEN_TILE, first_token=0):
    def tok(width):
        return _token_spec(width, tile, first_token)

    def const_spec(arr):
        return pl.BlockSpec(arr.shape, lambda b, t: (0,) * arr.ndim)

    consts = [mod_l, gla_nw, hg_nw, bd]
    specs = ([pl.BlockSpec((1, tile, NA_WIDTH), lambda b, t: (b, t, 0))]
             + [tok(GLA_WIDTH + HGRN_WIDTH)] * 2
             + [tok(W_GATE)] + [const_spec(a) for a in consts]
             + [pl.BlockSpec((None,) + w_out_b.shape[1:], lambda b, t: (layer, 0, 0))])
    return specs, [o_na, *scans, gates, *consts, w_out_b]


def _final_out(streams, mix_args, final_w, n_batch, t_all):
    (stream,) = streams
    t_lat = t_all - CTX_LEN
    assert t_lat % FINAL_TILE == 0
    mix_specs, mix_arrays = _mix_io(*mix_args, tile=FINAL_TILE, first_token=CTX_LEN)
    return pl.pallas_call(
        _final_out_kernel,
        grid=(n_batch, t_lat // FINAL_TILE),
        in_specs=[_token_spec(D_MODEL, FINAL_TILE, CTX_LEN)] + mix_specs
                 + [pl.BlockSpec((1, D_MODEL), lambda b, t: (0, 0))],
        out_specs=pl.BlockSpec((1, FINAL_TILE, D_MODEL), lambda b, t: (b, t, 0)),
        out_shape=jax.ShapeDtypeStruct((n_batch, t_lat, D_MODEL), F32),
        compiler_params=pltpu.CompilerParams(
            dimension_semantics=("parallel", "parallel"), vmem_limit_bytes=VMEM_LIMIT),
        name="out_proj_final",
    )(stream, *mix_arrays, final_w)


def _out_in_proj(streams, mix_args, mod_next, norm_w_next, w_in_p, next_layer, n_batch, t_all):
    mix_specs, mix_arrays = _mix_io(*mix_args)
    proj_specs, out_specs, out_shape = _projection_io(mod_next, next_layer, n_batch, t_all)
    stream_spec = pl.BlockSpec((1, TOKEN_TILE, D_MODEL), lambda b, t: (b, t, 0))
    stream_shape = jax.ShapeDtypeStruct((n_batch, t_all, D_MODEL), F32)
    outs = pl.pallas_call(
        functools.partial(_out_in_kernel, len(streams), n_batch),
        grid=(n_batch, t_all // TOKEN_TILE),
        in_specs=_stream_specs(len(streams)) + mix_specs + proj_specs,
        out_specs=[stream_spec] + out_specs,
        out_shape=[stream_shape] + out_shape,
        compiler_params=pltpu.CompilerParams(
            dimension_semantics=("parallel", "parallel"), vmem_limit_bytes=VMEM_LIMIT),
        name="out_in_proj",
    )(*streams, *mix_arrays, mod_next, norm_w_next, w_in_p)
    return outs[0], outs[1:]


def _pad_w_in(w):
    assert w.shape[-1] == IN_OFFS[-1]
    return jnp.pad(w.astype(BF16), [(0, 0)] * (w.ndim - 1) + [(0, W_IN_PAD - w.shape[-1])])


def kernel(x, c, ctx, c_ctx, ada_w, ada_b, norm_w, w_in, na_rpb, gla_w_gk, gla_b_gk, gla_norm_w,
           hgrn_lower_bounds, hgrn_norm_w, w_out, final_norm_w):
    n_batch, t_lat, _ = x.shape
    depth = ada_w.shape[0]
    t_all = CTX_LEN + t_lat
    assert ctx.shape[1] == CTX_LEN and t_lat % TOKEN_TILE == 0 and t_lat // GRID_W >= NA_KEY_ROWS

    lbs = jnp.cumsum(jax.nn.softmax(hgrn_lower_bounds.astype(F32), axis=0), axis=0)
    lbs = lbs - lbs[0:1]
    mod_rows = -(-(n_batch + 1) // 8) * 8
    cc = jnp.zeros((mod_rows, D_MODEL), F32).at[:n_batch].set(c).at[n_batch].set(c_ctx)
    mod = _modulation(cc, ada_w, ada_b)

    consts = {k_: jnp.asarray(v_, BF16) for k_, v_ in _scan_constants().items()}
    cos_t, sin_t = _rope_tables(t_lat)
    head_of = np.arange(GLA_WIDTH) // HEAD_DIM
    bd = jnp.asarray((head_of[:, None] == head_of[None, :]).astype(np.float32) / HEAD_DIM, BF16)

    w_in_p = _pad_w_in(w_in)
    w_out_b = w_out.astype(BF16)
    rpb_tiles = _na_bias_tiles(na_rpb)
    wgk = jnp.zeros((depth, LANES, 2 * GLA_KEY), F32)
    wgk = wgk.at[:, 0:GLA_GK_RANK, 0:GLA_KEY].set(gla_w_gk[:, 0])
    wgk = wgk.at[:, GLA_GK_RANK:2 * GLA_GK_RANK, GLA_KEY:].set(gla_w_gk[:, 1]).astype(BF16)

    streams = [ctx, x]
    projections = _in_proj(streams, mod[0], norm_w[0].reshape(1, D_MODEL), w_in_p, 0, n_batch, t_all)
    for layer in range(depth):
        last = layer == depth - 1
        bgk = gla_b_gk[layer].reshape(1, 2 * GLA_KEY)
        gla_nw = jnp.tile(gla_norm_w[layer], SCAN_HEADS).reshape(1, GLA_WIDTH)
        hg_nw = jnp.tile(hgrn_norm_w[layer], SCAN_HEADS).reshape(1, HGRN_WIDTH)

        q, k, vt, gates, scan_in = projections
        o_na = _neighborhood_attention(q, k, vt, rpb_tiles, layer, with_ctx_queries=not last)
        scans = _scans(scan_in, cos_t, sin_t, wgk[layer], bgk, lbs[layer], consts)
        mix_args = (o_na, scans, gates, mod[layer], gla_nw, hg_nw, bd, w_out_b, layer)
        if last:
            return _final_out(streams, mix_args, final_norm_w.reshape(1, D_MODEL), n_batch, t_all)
        stream, projections = _out_in_proj(streams, mix_args, mod[layer + 1],
                                           norm_w[layer + 1].reshape(1, D_MODEL), w_in_p, layer + 1,
                                           n_batch, t_all)
        streams = [stream]
```

```python
import functools

import numpy as np
import jax
import jax.numpy as jnp
from jax import lax
from jax.experimental import pallas as pl
from jax.experimental.pallas import tpu as pltpu

F32 = jnp.float32
BF16 = jnp.bfloat16

D_MODEL = 1024
CTX_LEN = 256
GRID_W = 64
NORM_EPS = 1e-6
HEAD_DIM = 64
NA_WIDTH = 512
NA_HEADS = 8
NA_WIN_H = 8
NA_WIN_W = 16
GLA_WIDTH = 256
GLA_KEY = 128
GLA_GK_RANK = 16
GLA_GATE_NORM = 16.0
HGRN_WIDTH = 256
SCAN_HEADS = 4
CHUNK = 16
ROPE_BASE = 10000.0

LOG2_E = float(np.log2(np.e))
NA_Q_SCALE = HEAD_DIM ** -0.5 * LOG2_E

LANES = 128
TOKEN_TILE = 256
NA_PIPELINE_LAG = 3
FINAL_TILE = 512
SCAN_BLOCK = 128
SCAN_SUB = 2
CHUNKS_PER_BLOCK = SCAN_BLOCK // CHUNK
NA_Q_ROWS = TOKEN_TILE // GRID_W
NA_KEY_ROWS = NA_Q_ROWS + NA_WIN_H
NA_KEYS = NA_KEY_ROWS * GRID_W
assert NA_KEYS % LANES == 0 and (NA_Q_ROWS * GRID_W) % LANES == 0 and NA_WIN_H % 4 == 0
VMEM_LIMIT = 56 * 1024 * 1024

IN_NAMES = ("na_q", "na_k", "na_v", "na_g", "gla_q", "gla_k", "gla_v", "gla_gk", "gla_g",
            "hg_q", "hg_ff", "hg_fb", "hg_i", "hg_g")
IN_SPLITS = (512, 512, 512, 512, 128, 128, 256, 32, 256, 256, 256, 256, 256, 256)
IN_OFFS = tuple(int(v) for v in np.cumsum((0,) + IN_SPLITS))
W_GATE = NA_WIDTH + GLA_WIDTH + HGRN_WIDTH
W_GLA = GLA_WIDTH + 2 * GLA_KEY + LANES
W_HG = 4 * HGRN_WIDTH
W_IN_PAD = -(-IN_OFFS[-1] // LANES) * LANES


def _dot(a, b):
    return jnp.dot(a, b, preferred_element_type=F32)


def _dot_nt(a, b):
    return lax.dot_general(a, b, (((1,), (1,)), ((), ())), preferred_element_type=F32)


def _dot_tn(a, b):
    return lax.dot_general(a, b, (((0,), (0,)), ((), ())), preferred_element_type=F32)


def _sigmoid(x):
    return 1.0 / (1.0 + jnp.exp(-x))


def _silu(x):
    return x * _sigmoid(x)


def _mod_kernel(c_ref, w_ref, b_ref, o_ref):
    s = _silu(c_ref[...]).astype(BF16)
    o_ref[0] = _dot(s, w_ref[0].astype(BF16)) + b_ref[0]


def _modulation(cc, ada_w, ada_b):
    depth = ada_w.shape[0]
    rows = cc.shape[0]
    return pl.pallas_call(
        _mod_kernel,
        grid=(depth, 3),
        in_specs=[
            pl.BlockSpec((rows, D_MODEL), lambda l, j: (0, 0)),
            pl.BlockSpec((1, D_MODEL, D_MODEL), lambda l, j: (l, 0, j)),
            pl.BlockSpec((1, 1, D_MODEL), lambda l, j: (l, 0, j)),
        ],
        out_specs=pl.BlockSpec((1, rows, D_MODEL), lambda l, j: (l, 0, j)),
        out_shape=jax.ShapeDtypeStruct((depth, rows, 3 * D_MODEL), F32),
        compiler_params=pltpu.CompilerParams(
            dimension_semantics=("parallel", "parallel"), vmem_limit_bytes=VMEM_LIMIT),
        name="adaln_mod",
    )(cc, ada_w, ada_b.reshape(depth, 1, 3 * D_MODEL))


def _token_tile(x_refs, n_batch, tile_off=0):
    t = pl.program_id(1) + tile_off
    is_ctx = t < CTX_LEN // TOKEN_TILE
    x = jnp.where(is_ctx, x_refs[0][0], x_refs[1][0]) if len(x_refs) == 2 else x_refs[0][0]
    return x, jnp.where(is_ctx, n_batch, pl.program_id(0))


def _stream_specs(n_streams, tile_off=0):
    n_ctx_tiles = CTX_LEN // TOKEN_TILE
    if n_streams == 2:
        return [pl.BlockSpec((1, TOKEN_TILE, D_MODEL), lambda b, t: (b, jnp.minimum(t, n_ctx_tiles - 1), 0)),
                pl.BlockSpec((1, TOKEN_TILE, D_MODEL), lambda b, t: (b, jnp.maximum(t - n_ctx_tiles, 0), 0))]
    return [pl.BlockSpec((1, TOKEN_TILE, D_MODEL), lambda b, t: (b, t + tile_off, 0))]


def _in_proj_kernel(n_streams, n_batch, *refs):
    x, row = _token_tile(refs[:n_streams], n_batch)
    _project_tile(x, row, *refs[n_streams:])


def _project_tile(x, row, mod_ref, nw_ref, w_ref, q_ref, k_ref, v_ref, gate_ref, scan_ref):
    m = mod_ref[pl.ds(row, 1), :]
    sh = m[:, :D_MODEL]
    sc = m[:, D_MODEL:2 * D_MODEL]
    ms = jnp.mean(x * x, axis=-1, keepdims=True)
    y = x * lax.rsqrt(ms + NORM_EPS) * nw_ref[...]
    h = y * (1.0 + sc) + sh
    p = _dot(h.astype(BF16), w_ref[...])
    for pr in range(NA_HEADS // 2):
        lo = pr * LANES
        q_ref[0, pr] = (p[:, lo:lo + LANES] * NA_Q_SCALE).astype(BF16)
        k_ref[0, pr] = p[:, NA_WIDTH + lo:NA_WIDTH + lo + LANES].astype(BF16)
        for s in range(TOKEN_TILE // LANES):
            v_tile = p[s * LANES:(s + 1) * LANES, 2 * NA_WIDTH + lo:2 * NA_WIDTH + lo + LANES]
            v_ref[0, pr, s] = jnp.transpose(v_tile).astype(BF16)
    off = dict(zip(IN_NAMES, IN_OFFS))
    tail = p[:, off["gla_gk"]:]
    aligned = pltpu.roll(tail, tail.shape[1] - IN_SPLITS[IN_NAMES.index("gla_gk")], 1)
    seg = lambda name: aligned[:, off[name] - off["gla_g"]:off[name] - off["gla_g"] + IN_SPLITS[IN_NAMES.index(name)]]
    gate_ref[0] = jnp.concatenate([p[:, off["na_g"]:off["gla_q"]], seg("gla_g"), seg("hg_g")],
                                  axis=1).astype(BF16)
    scan_ref[0] = jnp.concatenate(
        [p[:, off["gla_v"]:off["gla_gk"]], p[:, off["gla_q"]:off["gla_v"]], tail[:, :LANES],
         aligned[:, off["hg_q"] - off["gla_g"]:off["hg_g"] - off["gla_g"]]], axis=1)


def _projection_io(mod_l, layer, n_batch, t_all):
    in_specs = [
        pl.BlockSpec(mod_l.shape, lambda b, t: (0, 0)),
        pl.BlockSpec((1, D_MODEL), lambda b, t: (0, 0)),
        pl.BlockSpec((None, D_MODEL, W_IN_PAD), lambda b, t: (layer, 0, 0), pipeline_mode=pl.Buffered(1)),
    ]
    pair_spec = pl.BlockSpec((1, NA_HEADS // 2, TOKEN_TILE, LANES), lambda b, t: (b, 0, t, 0))
    pair_shape = jax.ShapeDtypeStruct((n_batch, NA_HEADS // 2, t_all, LANES), BF16)
    vt_spec = pl.BlockSpec((1, NA_HEADS // 2, TOKEN_TILE // LANES, LANES, LANES), lambda b, t: (b, 0, t, 0, 0))
    vt_shape = jax.ShapeDtypeStruct((n_batch, NA_HEADS // 2, t_all // LANES, LANES, LANES), BF16)

    def slab(width, dtype):
        return (pl.BlockSpec((1, TOKEN_TILE, width), lambda b, t: (b, t, 0)),
                jax.ShapeDtypeStruct((n_batch, t_all, width), dtype))

    gate_spec, gate_shape = slab(W_GATE, BF16)
    scan_spec, scan_shape = slab(W_GLA + W_HG, F32)
    return (in_specs, [pair_spec, pair_spec, vt_spec, gate_spec, scan_spec],
            [pair_shape, pair_shape, vt_shape, gate_shape, scan_shape])


def _in_proj(streams, mod_l, norm_w_l, w_in_p, layer, n_batch, t_all):
    in_specs, out_specs, out_shape = _projection_io(mod_l, layer, n_batch, t_all)
    return pl.pallas_call(
        functools.partial(_in_proj_kernel, len(streams), n_batch),
        grid=(n_batch, t_all // TOKEN_TILE),
        in_specs=_stream_specs(len(streams)) + in_specs,
        out_specs=out_specs,
        out_shape=out_shape,
        compiler_params=pltpu.CompilerParams(
            dimension_semantics=("parallel", "parallel"), vmem_limit_bytes=VMEM_LIMIT),
        name="in_proj",
    )(*streams, mod_l, norm_w_l, w_in_p)


def _na_bias_cases():
    cases = []
    for key_row_ok, rel in (
        (lambda lr, lk: lk < NA_WIN_H, lambda lr, lk: lk - lr + NA_WIN_H - 1),
        (lambda lr, lk: lr <= lk < lr + NA_WIN_H, lambda lr, lk: lk - lr + NA_WIN_H // 2 - 1),
        (lambda lr, lk: lk >= NA_KEY_ROWS - NA_WIN_H, lambda lr, lk: lk - lr - 1),
    ):
        cases.append([[rel(lr, lk) if key_row_ok(lr, lk) else None for lk in range(NA_KEY_ROWS)]
                      for lr in range(NA_Q_ROWS)])
    return cases


def _na_kernel(tile_off, q_ref, k_ref, vt_ref, rt_ref, o_ref, bias_ref):
    tb = pl.program_id(1) + tile_off
    n_pairs = q_ref.shape[1]
    lane = lax.broadcasted_iota(jnp.int32, (1, LANES), 1)
    q_scale = [jnp.where((lane // HEAD_DIM) == hh, 1.0, 0.0).astype(BF16) for hh in range(2)]
    heads = [(p, hh) for p in range(n_pairs) for hh in range(2)]
    ctx_tiles = CTX_LEN // LANES

    @pl.when((pl.program_id(0) == 0) & (pl.program_id(1) == 0))
    def _build_bias_tables():
        minus_inf = jnp.full((GRID_W, GRID_W), -jnp.inf, F32)

        def per_head(h, carry):
            for case, rel in enumerate(_na_bias_cases()):
                for lk in range(NA_KEY_ROWS):
                    for lr in range(0, NA_Q_ROWS, 2):
                        tiles = [minus_inf if rel[r][lk] is None else rt_ref[h, rel[r][lk]] for r in (lr, lr + 1)]
                        bias_ref[case, h, lk * GRID_W:(lk + 1) * GRID_W, lr * GRID_W:(lr + 2) * GRID_W] = (
                            jnp.concatenate(tiles, axis=1))
            return carry

        lax.fori_loop(0, NA_HEADS, per_head, 0)

    def vt_tiles(p, first, count):
        tiles = vt_ref[0, p, pl.ds(first, count)]
        return jnp.concatenate([tiles[i] for i in range(count)], axis=1)

    def finish(outs):
        for p in range(n_pairs):
            o_t = jnp.concatenate([outs[(p, 0)], outs[(p, 1)]], axis=0)
            o_ref[0, :, p * LANES:(p + 1) * LANES] = jnp.transpose(o_t).astype(o_ref.dtype)

    def pipelined(scores_of, attend):
        outs, issued = {}, {}
        for i in range(len(heads) + NA_PIPELINE_LAG):
            if i < len(heads):
                issued[i] = scores_of(*heads[i])
            j = i - NA_PIPELINE_LAG
            if j >= 0:
                outs[heads[j]] = attend(*heads[j], *issued.pop(j))
        finish(outs)

    def softmax_pv(hh, vt_parts, s_parts):
        m = functools.reduce(jnp.maximum, [jnp.max(s, axis=0, keepdims=True) for s in s_parts])
        o_t = None
        for vt, s in zip(vt_parts, s_parts):
            ones = jnp.ones((HEAD_DIM, vt.shape[1]), BF16)
            lhs = jnp.concatenate([vt[:HEAD_DIM], ones] if hh == 0 else [ones, vt[HEAD_DIM:]], axis=0)
            part = _dot(lhs, jnp.exp2(s - m).astype(BF16))
            o_t = part if o_t is None else o_t + part
        own, other = (0, HEAD_DIM) if hh == 0 else (HEAD_DIM, 0)
        return o_t[own:own + HEAD_DIM] / o_t[other:other + 1]

    @pl.when(tb == 0)
    def _ctx_queries():
        def scores_of(p, hh):
            return (_dot_nt(k_ref[0, p, 0:CTX_LEN, :], q_ref[0, p] * q_scale[hh]),)

        def attend(p, hh, s):
            return softmax_pv(hh, [vt_tiles(p, 0, ctx_tiles)], [s])

        pipelined(scores_of, attend)

    @pl.when(tb > 0)
    def _latent_queries():
        g = tb - 1
        n_groups = pl.num_programs(1) - 1 + tile_off
        ws = jnp.clip(NA_Q_ROWS * g - NA_WIN_H // 2, 0, GRID_W - NA_KEY_ROWS)
        case = (g > 0).astype(jnp.int32) + (g == n_groups - 1).astype(jnp.int32)
        start = pl.multiple_of(CTX_LEN + ws * GRID_W, LANES)
        first_tile = ctx_tiles + ws * GRID_W // LANES

        def scores_of(p, hh):
            qh = q_ref[0, p] * q_scale[hh]
            s_nb = _dot_nt(k_ref[0, p, pl.ds(start, NA_KEYS), :], qh) + bias_ref[case, 2 * p + hh]
            return s_nb, _dot_nt(k_ref[0, p, 0:CTX_LEN, :], qh)

        def attend(p, hh, s_nb, s_cx):
            return softmax_pv(hh, [vt_tiles(p, first_tile, NA_KEYS // LANES), vt_tiles(p, 0, ctx_tiles)],
                              [s_nb, s_cx])

        pipelined(scores_of, attend)


def _neighborhood_attention(q, k, vt, rpb_tiles, layer, with_ctx_queries):
    n_batch, n_pairs, t_all, _ = q.shape
    n_tiles = t_all // TOKEN_TILE
    tile_off = 0 if with_ctx_queries else CTX_LEN // TOKEN_TILE
    n_steps = n_tiles - tile_off
    t_out = n_steps * TOKEN_TILE
    return pl.pallas_call(
        functools.partial(_na_kernel, tile_off),
        grid=(n_batch, n_steps),
        in_specs=[
            pl.BlockSpec((1, n_pairs, TOKEN_TILE, LANES), lambda b, g: (b, 0, g + tile_off, 0)),
            pl.BlockSpec((1, n_pairs, t_all, LANES), lambda b, g: (b, 0, 0, 0)),
            pl.BlockSpec((1, n_pairs, t_all // LANES, LANES, LANES), lambda b, g: (b, 0, 0, 0, 0)),
            pl.BlockSpec((None,) + rpb_tiles.shape[1:], lambda b, g: (layer, 0, 0, 0, 0),
                         pipeline_mode=pl.Buffered(1)),
        ],
        out_specs=pl.BlockSpec((1, TOKEN_TILE, NA_WIDTH), lambda b, g: (b, g, 0)),
        out_shape=jax.ShapeDtypeStruct((n_batch, t_out, NA_WIDTH), BF16),
        scratch_shapes=[pltpu.VMEM((3, NA_HEADS, NA_KEYS, TOKEN_TILE), F32)],
        compiler_params=pltpu.CompilerParams(
            dimension_semantics=("arbitrary", "arbitrary"),
            vmem_limit_bytes=VMEM_LIMIT),
        name="neighborhood_attention",
    )(q, k, vt, rpb_tiles)


def _na_bias_tiles(rpb):
    col = np.arange(GRID_W)
    col_start = np.clip(col - NA_WIN_W // 2, 0, GRID_W - NA_WIN_W)
    col_in = (col[:, None] >= col_start[None, :]) & (col[:, None] < col_start[None, :] + NA_WIN_W)
    col_idx = np.clip(col[:, None] - col[None, :], 1 - NA_WIN_W, NA_WIN_W - 1) + NA_WIN_W - 1
    one_hot = (col_idx[None] == np.arange(2 * NA_WIN_W - 1)[:, None, None]).astype(np.float32)
    tiles = jnp.einsum("lhrw,wkj->lhrkj", rpb.astype(F32), jnp.asarray(one_hot),
                       precision=lax.Precision.HIGHEST)
    return jnp.where(col_in, tiles * LOG2_E, -jnp.inf)


def _scan_constants():
    idx = np.arange(SCAN_BLOCK)
    c = idx // CHUNK
    pos = idx % CHUNK
    same = c[:, None] == c[None, :]
    return {"cum_f": (same & (pos[None, :] <= pos[:, None])).astype(np.float32),
            "cum_b": (same & (pos[None, :] >= pos[:, None])).astype(np.float32)}


def _block_matrix(n_rows, n_cols, block_shape, dtype, block_at):
    zero = jnp.zeros(block_shape, dtype)
    rows = []
    for r in range(n_rows):
        blocks = [block_at(r, c) for c in range(n_cols)]
        rows.append(jnp.concatenate([zero if b is None else b for b in blocks], axis=1))
    return jnp.concatenate(rows, axis=0)


def _run_interleaved(stagewise):
    results = [None] * len(stagewise)
    live = list(range(len(stagewise)))
    while live:
        for i in list(live):
            try:
                next(stagewise[i])
            except StopIteration as done:
                results[i] = done.value
                live.remove(i)
    return results


def _scan_direction(load_inputs, cum, st_ref, reverse):
    q, k, v, lf = load_inputs()
    yield
    n = SCAN_BLOCK
    hd = q.shape[1]
    dk = hd // SCAN_HEADS
    pack = 2
    low_half = lax.broadcasted_iota(jnp.int32, (1, LANES), 1) < HEAD_DIM
    lane_k = lax.broadcasted_iota(jnp.int32, (1, hd), 1) // dk
    lane_v = lax.broadcasted_iota(jnp.int32, (1, SCAN_HEADS * HEAD_DIM), 1) // HEAD_DIM

    def lane_mask(cond):
        return jnp.where(cond, 1.0, 0.0).astype(BF16)

    half_masks = {True: lane_mask(low_half), False: lane_mask(jnp.logical_not(low_half))}
    k_masks = [lane_mask(lane_k == h) for h in range(SCAN_HEADS)]
    v_masks = [lane_mask(lane_v == h) for h in range(SCAN_HEADS)]

    def keep_half(x, low):
        return x * half_masks[low]

    hi = lf.astype(BF16)
    mid = (lf - hi.astype(F32)).astype(BF16)
    g = _dot(cum, hi) + _dot(cum, mid)
    yield
    ref_pos, last_pos = (CHUNK // 2, 0) if reverse else (CHUNK // 2 - 1, CHUNK - 1)
    to_ref, to_last, decay = [], [], []
    for c in range(CHUNKS_PER_BLOCK):
        gc = g[c * CHUNK:(c + 1) * CHUNK]
        g_last = gc[last_pos:last_pos + 1]
        to_ref.append(gc - gc[ref_pos:ref_pos + 1])
        to_last.append(g_last - gc)
        decay.append(jnp.exp(g_last))
    x_ref = jnp.concatenate(to_ref, axis=0)
    a = jnp.exp(x_ref)
    a_inv = jnp.exp(-x_ref)
    e_in = jnp.exp(g)
    e_out = jnp.exp(jnp.concatenate(to_last, axis=0))

    q_t = (q * a).astype(BF16)
    k_t_tr = jnp.transpose(k * a_inv).astype(BF16)
    q_in = (q * e_in).astype(BF16)
    k_out = (k * e_out).astype(BF16)
    v_b = v.astype(BF16)

    k_blk = _block_matrix(SCAN_HEADS, SCAN_HEADS, (dk, n), BF16,
                          lambda r, c: k_t_tr[r * dk:(r + 1) * dk] if r == c else None)
    scores = _dot(q_t, k_blk)
    yield

    v_tr = jnp.transpose(v).astype(BF16)
    token_pair = lax.broadcasted_iota(jnp.int32, (1, n), 1) // (CHUNK * pack)
    pair_masks = [lane_mask(token_pair == m) for m in range(CHUNKS_PER_BLOCK // pack)]
    heads_per_group = LANES // dk
    incs = []
    for grp in range(hd // LANES):
        heads = range(grp * heads_per_group, (grp + 1) * heads_per_group)
        v_blk = jnp.concatenate(
            [jnp.concatenate([v_tr[h * HEAD_DIM:(h + 1) * HEAD_DIM] * pm for h in heads], axis=1)
             for pm in pair_masks], axis=0)
        k_rows = []
        for h in heads:
            k_h = (k_out * k_masks[h])[:, grp * LANES:(grp + 1) * LANES]
            k_rows.append(_block_matrix(
                CHUNKS_PER_BLOCK, pack, (CHUNK, LANES), BF16,
                lambda c, par, k_h=k_h: k_h[c * CHUNK:(c + 1) * CHUNK] if c % pack == par else None))
        incs.append(_dot(v_blk, jnp.concatenate(k_rows, axis=0)))
    yield

    tri =jnp.concatenate([cum.astype(F32)] * SCAN_HEADS, axis=1) > 0.5
    p = jnp.where(tri, scores, 0.0).astype(BF16)
    v_stack = jnp.concatenate([v_b * v_masks[h] for h in range(SCAN_HEADS)], axis=0)
    o_intra = _dot(p, v_stack)
    yield

    st = st_ref[...]
    states = [None] * CHUNKS_PER_BLOCK
    order = range(CHUNKS_PER_BLOCK - 1, -1, -1) if reverse else range(CHUNKS_PER_BLOCK)
    for c in order:
        states[c] = st
        m, par = divmod(c, pack)
        inc_c = jnp.concatenate([inc[m * HEAD_DIM:(m + 1) * HEAD_DIM, par * LANES:(par + 1) * LANES]
                                 for inc in incs], axis=1)
        st = decay[c] * st + inc_c
    st_ref[...] = st

    q_blk = _block_matrix(CHUNKS_PER_BLOCK, CHUNKS_PER_BLOCK, (CHUNK, hd), BF16,
                          lambda r, c: q_in[r * CHUNK:(r + 1) * CHUNK] if r == c else None)
    s_parts = []
    for s in states:
        s_tr = jnp.transpose(jnp.concatenate([s, s], axis=0)).astype(BF16)

        def s_block(h, j, s_tr=s_tr):
            return keep_half(s_tr[h * dk:(h + 1) * dk], h % 2 == 0) if h // 2 == j else None

        s_parts.append(_block_matrix(SCAN_HEADS, SCAN_HEADS // 2, (dk, LANES), BF16, s_block))
    return o_intra + _dot(q_blk, jnp.concatenate(s_parts, axis=0))


def _rope(x, cos, sin_signed):
    lane = lax.broadcasted_iota(jnp.int32, (1, x.shape[1]), 1)
    first = (lane % 16) < 8
    partner = jnp.where(first, pltpu.roll(x, x.shape[1] - 8, 1), pltpu.roll(x, 8, 1))
    return x * cos + partner * sin_signed


def _scan_kernel(n_ctx_blocks,
                 in_f_ref, in_b_ref,
                 cos_f_ref, sin_f_ref, cos_b_ref, sin_b_ref,
                 wgk_ref, bgk_ref, lb_ref, cum_f_ref, cum_b_ref,
                 of_ref, ob_ref,
                 sgf_ref, sgb_ref, shf_ref, shb_ref):
    del n_ctx_blocks
    gla_cols, hg_cols = pl.ds(0, GLA_WIDTH), pl.ds(GLA_WIDTH, HGRN_WIDTH)

    @pl.when(pl.program_id(1) == 0)
    def _reset():
        for ref in (sgf_ref, sgb_ref, shf_ref, shb_ref):
            ref[...] = jnp.zeros_like(ref)

    def gla_inputs(in_ref, cos_ref, sin_ref, direction, rows):
        blk = in_ref[0, rows, 0:W_GLA]
        v = blk[:, 0:GLA_WIDTH]
        q = blk[:, GLA_WIDTH:GLA_WIDTH + GLA_KEY]
        k = blk[:, GLA_WIDTH + GLA_KEY:GLA_WIDTH + 2 * GLA_KEY]
        gk = blk[:, GLA_WIDTH + 2 * GLA_KEY:]
        cos = cos_ref[rows, :]
        sin = sin_ref[rows, :]
        q = _rope(q, cos, sin) * (GLA_KEY // SCAN_HEADS) ** -0.5
        k = _rope(k, cos, sin)
        lo = direction * GLA_KEY
        z = _dot(gk.astype(BF16), wgk_ref[:, lo:lo + GLA_KEY]) + bgk_ref[:, lo:lo + GLA_KEY]
        log_sig = jnp.minimum(z, 0.0) - jnp.log(1.0 + jnp.exp(-jnp.abs(z)))
        return q, k, v, log_sig / GLA_GATE_NORM

    def hgrn_inputs(in_ref, direction, rows):
        blk = in_ref[0, rows, W_GLA:W_GLA + W_HG]
        q = blk[:, 0:HGRN_WIDTH] * HEAD_DIM ** -0.5
        f_raw = blk[:, (1 + direction) * HGRN_WIDTH:(2 + direction) * HGRN_WIDTH]
        v = blk[:, 3 * HGRN_WIDTH:]
        lb = lb_ref[direction:direction + 1, :]
        f = lb + (1.0 - lb) * _sigmoid(f_raw)
        return q, 1.0 - f, v, jnp.log(f)

    jobs, dests = [], []
    for s in range(SCAN_SUB):
        rows_f = pl.ds(s * SCAN_BLOCK, SCAN_BLOCK)
        rows_b = pl.ds((SCAN_SUB - 1 - s) * SCAN_BLOCK, SCAN_BLOCK)
        jobs += [
            _scan_direction(functools.partial(gla_inputs, in_f_ref, cos_f_ref, sin_f_ref, 0, rows_f),
                            cum_f_ref[...],sgf_ref, reverse=False),
            _scan_direction(functools.partial(hgrn_inputs, in_f_ref, 0, rows_f),
                            cum_f_ref[...],shf_ref, reverse=False),
            _scan_direction(functools.partial(gla_inputs, in_b_ref, cos_b_ref, sin_b_ref, 1, rows_b),
                            cum_b_ref[...],sgb_ref, reverse=True),
            _scan_direction(functools.partial(hgrn_inputs, in_b_ref, 1, rows_b),
                            cum_b_ref[...],shb_ref, reverse=True),
        ]
        dests += [(of_ref, rows_f, gla_cols), (of_ref, rows_f, hg_cols),
                  (ob_ref, rows_b, gla_cols), (ob_ref, rows_b, hg_cols)]
    for (ref, rows, cols), val in zip(dests, _run_interleaved(jobs)):
        ref[0, rows, cols] = val.astype(ref.dtype)


def _scans(scan_in, cos_t, sin_t, wgk, bgk, lb, consts):
    n_batch, t_all, _ = scan_in.shape
    step_rows = SCAN_SUB * SCAN_BLOCK
    assert CTX_LEN % step_rows == 0 and t_all % step_rows == 0
    n_blocks = t_all // step_rows
    n_ctx_blocks = CTX_LEN // step_rows

    def fwd(i):
        return i

    def bwd(i):
        return jnp.where(i < n_ctx_blocks, n_ctx_blocks - 1 - i, n_blocks - 1 + n_ctx_blocks - i)

    def tok_spec(width, order):
        return pl.BlockSpec((1, step_rows, width), lambda b, i: (b, order(i), 0))

    def tab_spec(order):
        return pl.BlockSpec((step_rows, LANES), lambda b, i: (order(i), 0))

    def const_spec(arr):
        return pl.BlockSpec(arr.shape, lambda b, i: (0,) * arr.ndim)

    const_arrays = [wgk, bgk, lb, consts["cum_f"], consts["cum_b"]]
    out_shape = jax.ShapeDtypeStruct((n_batch, t_all, GLA_WIDTH + HGRN_WIDTH), BF16)
    return pl.pallas_call(
        functools.partial(_scan_kernel, n_ctx_blocks),
        grid=(n_batch, n_blocks),
        in_specs=[tok_spec(W_GLA + W_HG, fwd), tok_spec(W_GLA + W_HG, bwd),
                  tab_spec(fwd), tab_spec(fwd), tab_spec(bwd), tab_spec(bwd)]
                 + [const_spec(a) for a in const_arrays],
        out_specs=[tok_spec(GLA_WIDTH + HGRN_WIDTH, fwd), tok_spec(GLA_WIDTH + HGRN_WIDTH, bwd)],
        out_shape=[out_shape] * 2,
        scratch_shapes=[pltpu.VMEM((HEAD_DIM, GLA_KEY), F32), pltpu.VMEM((HEAD_DIM, GLA_KEY), F32),
                        pltpu.VMEM((HEAD_DIM, HGRN_WIDTH), F32), pltpu.VMEM((HEAD_DIM, HGRN_WIDTH), F32)],
        compiler_params=pltpu.CompilerParams(
            dimension_semantics=("parallel", "arbitrary"), vmem_limit_bytes=VMEM_LIMIT),
        name="bidirectional_scans",
    )(scan_in, scan_in, cos_t, sin_t, cos_t, sin_t, *const_arrays)


def _rope_tables(t_lat):
    half = GLA_KEY // SCAN_HEADS // 2
    inv_freq = 1.0 / (ROPE_BASE ** (np.arange(0, half, 2, dtype=np.float32) / half))
    t = np.arange(t_lat)
    lane = np.arange(LANES) % (2 * half)
    pos = np.where(lane[None, :] < half, (t // GRID_W)[:, None], (t % GRID_W)[:, None]).astype(np.float32)
    ang = (pos * inv_freq[(lane % half) % (half // 2)][None, :]).astype(np.float32)
    first = ((lane % half) < half // 2)[None, :]
    cos = np.cos(ang)
    sin = np.where(first, -np.sin(ang), np.sin(ang))
    cos = np.concatenate([np.ones((CTX_LEN, LANES), np.float32), cos], axis=0)
    sin = np.concatenate([np.zeros((CTX_LEN, LANES), np.float32), sin], axis=0)
    return jnp.asarray(cos, F32), jnp.asarray(sin, F32)


N_MIX_REFS = 9


def _mix_tile(x, row, ona_ref, of_ref, ob_ref, gate_ref, mod_ref, gnw_ref, hnw_ref, bd_ref, w_ref):
    gt = mod_ref[pl.ds(row, 1), :][:, 2 * D_MODEL:]
    bd = bd_ref[...]

    def head_norm(o, w):
        sq = o * o
        hi = sq.astype(BF16)
        lo = (sq - hi.astype(F32)).astype(BF16)
        ms = _dot(hi, bd) + _dot(lo, bd)
        return o * lax.rsqrt(ms + NORM_EPS) * w

    o_scan = of_ref[0].astype(F32) + ob_ref[0].astype(F32)
    o_gla = head_norm(o_scan[:, :GLA_WIDTH], gnw_ref[...])
    o_hg = head_norm(o_scan[:, GLA_WIDTH:], hnw_ref[...])
    o_cat = jnp.concatenate([ona_ref[0].astype(F32), o_gla, o_hg], axis=1)
    act = o_cat * _silu(gate_ref[0].astype(F32))
    y = _dot(act.astype(BF16), w_ref[...])
    return x + gt * y


def _final_out_kernel(x_ref, *refs):
    xn = _mix_tile(x_ref[0], pl.program_id(0), *refs[:N_MIX_REFS])
    fw_ref, out_ref = refs[N_MIX_REFS:]
    ms = jnp.mean(xn * xn, axis=-1, keepdims=True)
    out_ref[0] = xn * lax.rsqrt(ms + NORM_EPS) * fw_ref[...]


def _out_in_kernel(n_streams, n_batch, *refs):
    x, row = _token_tile(refs[:n_streams], n_batch)
    xn = _mix_tile(x, row, *refs[n_streams:n_streams + N_MIX_REFS])
    rest = refs[n_streams + N_MIX_REFS:]
    proj_in, stream_ref, proj_out = rest[:3], rest[3], rest[4:]
    stream_ref[0] = xn
    _project_tile(xn, row, *proj_in, *proj_out)


def _token_spec(width, tile, first_token):
    if first_token % tile == 0:
        return pl.BlockSpec((1, tile, width), lambda b, t: (b, t + first_token // tile, 0))
    return pl.BlockSpec((pl.Element(1), pl.Element(tile), pl.Element(width)),
                        lambda b, t: (b, pl.multiple_of(first_token + t * tile, TOKEN_TILE), 0))


def _mix_io(o_na, scans, gates, mod_l, gla_nw, hg_nw, bd, w_out_b, layer, tile=TOKEN_TILE, first_token=0):
    def tok(width):
        return _token_spec(width, tile, first_token)

    def const_spec(arr):
        return pl.BlockSpec(arr.shape, lambda b, t: (0,) * arr.ndim)

    consts = [mod_l, gla_nw, hg_nw, bd]
    specs = ([pl.BlockSpec((1, tile, NA_WIDTH), lambda b, t: (b, t, 0))]
             + [tok(GLA_WIDTH + HGRN_WIDTH)] * 2
             + [tok(W_GATE)] + [const_spec(a) for a in consts]
             + [pl.BlockSpec((None,) + w_out_b.shape[1:], lambda b, t: (layer, 0, 0))])
    return specs, [o_na, *scans, gates, *consts, w_out_b]


def _final_out(streams, mix_args, final_w, n_batch, t_all):
    (stream,) = streams
    t_lat = t_all - CTX_LEN
    assert t_lat % FINAL_TILE == 0
    mix_specs, mix_arrays = _mix_io(*mix_args, tile=FINAL_TILE, first_token=CTX_LEN)
    return pl.pallas_call(
        _final_out_kernel,
        grid=(n_batch, t_lat // FINAL_TILE),
        in_specs=[_token_spec(D_MODEL, FINAL_TILE, CTX_LEN)] + mix_specs
                 + [pl.BlockSpec((1, D_MODEL), lambda b, t: (0, 0))],
        out_specs=pl.BlockSpec((1, FINAL_TILE, D_MODEL), lambda b, t: (b, t, 0)),
        out_shape=jax.ShapeDtypeStruct((n_batch, t_lat, D_MODEL), F32),
        compiler_params=pltpu.CompilerParams(
            dimension_semantics=("parallel", "parallel"), vmem_limit_bytes=VMEM_LIMIT),
        name="out_proj_final",
    )(stream, *mix_arrays, final_w)


def _out_in_proj(streams, mix_args, mod_next, norm_w_next, w_in_p, next_layer, n_batch, t_all):
    mix_specs, mix_arrays = _mix_io(*mix_args)
    proj_specs, out_specs, out_shape = _projection_io(mod_next, next_layer, n_batch, t_all)
    stream_spec = pl.BlockSpec((1, TOKEN_TILE, D_MODEL), lambda b, t: (b, t, 0))
    stream_shape = jax.ShapeDtypeStruct((n_batch, t_all, D_MODEL), F32)
    outs = pl.pallas_call(
        functools.partial(_out_in_kernel, len(streams), n_batch),
        grid=(n_batch, t_all // TOKEN_TILE),
        in_specs=_stream_specs(len(streams)) + mix_specs + proj_specs,
        out_specs=[stream_spec] + out_specs,
        out_shape=[stream_shape] + out_shape,
        compiler_params=pltpu.CompilerParams(
            dimension_semantics=("parallel", "parallel"), vmem_limit_bytes=VMEM_LIMIT),
        name="out_in_proj",
    )(*streams, *mix_arrays, mod_next, norm_w_next, w_in_p)
    return outs[0], outs[1:]


def _pad_w_in(w):
    assert w.shape[-1] == IN_OFFS[-1]
    return jnp.pad(w.astype(BF16), [(0, 0)] * (w.ndim - 1) + [(0, W_IN_PAD - w.shape[-1])])


def kernel(x, c, ctx, c_ctx, ada_w, ada_b, norm_w, w_in, na_rpb, gla_w_gk, gla_b_gk, gla_norm_w,
           hgrn_lower_bounds, hgrn_norm_w, w_out, final_norm_w):
    n_batch, t_lat, _ = x.shape
    depth = ada_w.shape[0]
    t_all = CTX_LEN + t_lat
    assert ctx.shape[1] == CTX_LEN and t_lat % TOKEN_TILE == 0 and t_lat // GRID_W >= NA_KEY_ROWS

    lbs = jnp.cumsum(jax.nn.softmax(hgrn_lower_bounds.astype(F32), axis=0), axis=0)
    lbs = lbs - lbs[0:1]
    mod_rows = -(-(n_batch + 1) // 8) * 8
    cc = jnp.zeros((mod_rows, D_MODEL), F32).at[:n_batch].set(c).at[n_batch].set(c_ctx)
    mod = _modulation(cc, ada_w, ada_b)

    consts = {k_: jnp.asarray(v_, BF16) for k_, v_ in _scan_constants().items()}
    cos_t, sin_t = _rope_tables(t_lat)
    head_of = np.arange(GLA_WIDTH) // HEAD_DIM
    bd = jnp.asarray((head_of[:, None] == head_of[None, :]).astype(np.float32) / HEAD_DIM, BF16)

    w_in_p = _pad_w_in(w_in)
    w_out_b = w_out.astype(BF16)
    rpb_tiles = _na_bias_tiles(na_rpb)
    wgk = jnp.zeros((depth, LANES, 2 * GLA_KEY), F32)
    wgk = wgk.at[:, 0:GLA_GK_RANK, 0:GLA_KEY].set(gla_w_gk[:, 0])
    wgk = wgk.at[:, GLA_GK_RANK:2 * GLA_GK_RANK, GLA_KEY:].set(gla_w_gk[:, 1]).astype(BF16)

    streams = [ctx, x]
    projections = _in_proj(streams, mod[0], norm_w[0].reshape(1, D_MODEL), w_in_p, 0, n_batch, t_all)
    for layer in range(depth):
        last = layer == depth - 1
        bgk = gla_b_gk[layer].reshape(1, 2 * GLA_KEY)
        gla_nw = jnp.tile(gla_norm_w[layer], SCAN_HEADS).reshape(1, GLA_WIDTH)
        hg_nw = jnp.tile(hgrn_norm_w[layer], SCAN_HEADS).reshape(1, HGRN_WIDTH)

        q, k, vt, gates, scan_in = projections
        o_na = _neighborhood_attention(q, k, vt, rpb_tiles, layer, with_ctx_queries=not last)
        scans = _scans(scan_in, cos_t, sin_t, wgk[layer], bgk, lbs[layer], consts)
        mix_args = (o_na, scans, gates, mod[layer], gla_nw, hg_nw, bd, w_out_b, layer)
        if last:
            return _final_out(streams, mix_args, final_norm_w.reshape(1, D_MODEL), n_batch, t_all)
        stream, projections = _out_in_proj(streams, mix_args, mod[layer + 1],
                                           norm_w[layer + 1].reshape(1, D_MODEL), w_in_p, layer + 1,
                                           n_batch, t_all)
        streams = [stream]
```

```python
import functools

import numpy as np
import jax
import jax.numpy as jnp
from jax import lax
from jax.experimental import pallas as pl
from jax.experimental.pallas import tpu as pltpu

F32 = jnp.float32
BF16 = jnp.bfloat16

D_MODEL = 1024
CTX_LEN = 256
GRID_W = 64
NORM_EPS = 1e-6
HEAD_DIM = 64
NA_WIDTH = 512
NA_HEADS = 8
NA_WIN_H = 8
NA_WIN_W = 16
GLA_WIDTH = 256
GLA_KEY = 128
GLA_GK_RANK = 16
GLA_GATE_NORM = 16.0
HGRN_WIDTH = 256
SCAN_HEADS = 4
CHUNK = 16
ROPE_BASE = 10000.0

LOG2_E = float(np.log2(np.e))
NA_Q_SCALE = HEAD_DIM ** -0.5 * LOG2_E

LANES = 128
TOKEN_TILE = 256
NA_PIPELINE_LAG = 3
FINAL_TILE = 512
SCAN_BLOCK = 128
SCAN_SUB = 2
CHUNKS_PER_BLOCK = SCAN_BLOCK // CHUNK
NA_Q_ROWS = TOKEN_TILE // GRID_W
NA_KEY_ROWS = NA_Q_ROWS + NA_WIN_H
NA_KEYS = NA_KEY_ROWS * GRID_W
assert NA_KEYS % LANES == 0 and (NA_Q_ROWS * GRID_W) % LANES == 0 and NA_WIN_H % 4 == 0
VMEM_LIMIT = 56 * 1024 * 1024

IN_NAMES = ("na_q", "na_k", "na_v", "na_g", "gla_q", "gla_k", "gla_v", "gla_gk", "gla_g",
            "hg_q", "hg_ff", "hg_fb", "hg_i", "hg_g")
IN_SPLITS = (512, 512, 512, 512, 128, 128, 256, 32, 256, 256, 256, 256, 256, 256)
IN_OFFS = tuple(int(v) for v in np.cumsum((0,) + IN_SPLITS))
W_GATE = NA_WIDTH + GLA_WIDTH + HGRN_WIDTH
W_GLA = GLA_WIDTH + 2 * GLA_KEY + LANES
W_HG = 4 * HGRN_WIDTH
W_IN_PAD = -(-IN_OFFS[-1] // LANES) * LANES


def _dot(a, b):
    return jnp.dot(a, b, preferred_element_type=F32)


def _dot_nt(a, b):
    return lax.dot_general(a, b, (((1,), (1,)), ((), ())), preferred_element_type=F32)


def _dot_tn(a, b):
    return lax.dot_general(a, b, (((0,), (0,)), ((), ())), preferred_element_type=F32)


def _sigmoid(x):
    return 1.0 / (1.0 + jnp.exp(-x))


def _silu(x):
    return x * _sigmoid(x)


def _mod_kernel(c_ref, w_ref, b_ref, o_ref):
    s = _silu(c_ref[...]).astype(BF16)
    o_ref[0] = _dot(s, w_ref[0].astype(BF16)) + b_ref[0]


def _modulation(cc, ada_w, ada_b):
    depth = ada_w.shape[0]
    rows = cc.shape[0]
    return pl.pallas_call(
        _mod_kernel,
        grid=(depth, 3),
        in_specs=[
            pl.BlockSpec((rows, D_MODEL), lambda l, j: (0, 0)),
            pl.BlockSpec((1, D_MODEL, D_MODEL), lambda l, j: (l, 0, j)),
            pl.BlockSpec((1, 1, D_MODEL), lambda l, j: (l, 0, j)),
        ],
        out_specs=pl.BlockSpec((1, rows, D_MODEL), lambda l, j: (l, 0, j)),
        out_shape=jax.ShapeDtypeStruct((depth, rows, 3 * D_MODEL), F32),
        compiler_params=pltpu.CompilerParams(
            dimension_semantics=("parallel", "parallel"), vmem_limit_bytes=VMEM_LIMIT),
        name="adaln_mod",
    )(cc, ada_w, ada_b.reshape(depth, 1, 3 * D_MODEL))


def _token_tile(x_refs, n_batch, tile_off=0):
    t = pl.program_id(1) + tile_off
    is_ctx = t < CTX_LEN // TOKEN_TILE
    x = jnp.where(is_ctx, x_refs[0][0], x_refs[1][0]) if len(x_refs) == 2 else x_refs[0][0]
    return x, jnp.where(is_ctx, n_batch, pl.program_id(0))


def _stream_specs(n_streams, tile_off=0):
    n_ctx_tiles = CTX_LEN // TOKEN_TILE
    if n_streams == 2:
        return [pl.BlockSpec((1, TOKEN_TILE, D_MODEL), lambda b, t: (b, jnp.minimum(t, n_ctx_tiles - 1), 0)),
                pl.BlockSpec((1, TOKEN_TILE, D_MODEL), lambda b, t: (b, jnp.maximum(t - n_ctx_tiles, 0), 0))]
    return [pl.BlockSpec((1, TOKEN_TILE, D_MODEL), lambda b, t: (b, t + tile_off, 0))]


def _row_blocks(n_rows):
    return [slice(s * LANES, (s + 1) * LANES) for s in range(n_rows // LANES)]


def _in_proj_kernel(n_streams, n_batch, *refs):
    x, row = _token_tile(refs[:n_streams], n_batch)
    _run_interleaved([_project_rows(x[rows], s, rows, row, *refs[n_streams:])
                      for s, rows in enumerate(_row_blocks(x.shape[0]))])


def _project_rows(x, s, rows, row, mod_ref, nw_ref, w_ref, q_ref, k_ref, v_ref, gate_ref, scan_ref):
    m = mod_ref[pl.ds(row, 1), :]
    sh = m[:, :D_MODEL]
    sc = m[:, D_MODEL:2 * D_MODEL]
    ms = jnp.mean(x * x, axis=-1, keepdims=True)
    y = x * lax.rsqrt(ms + NORM_EPS) * nw_ref[...]
    h = (y * (1.0 + sc) + sh).astype(BF16)
    yield
    p = _dot(h, w_ref[...])
    yield
    for pr in range(NA_HEADS // 2):
        lo = pr * LANES
        q_ref[0, pr, rows, :] = (p[:, lo:lo + LANES] * NA_Q_SCALE).astype(BF16)
        k_ref[0, pr, rows, :] = p[:, NA_WIDTH + lo:NA_WIDTH + lo + LANES].astype(BF16)
        v_ref[0, pr, s] = jnp.transpose(p[:, 2 * NA_WIDTH + lo:2 * NA_WIDTH + lo + LANES]).astype(BF16)
    off = dict(zip(IN_NAMES, IN_OFFS))
    tail = p[:, off["gla_gk"]:]
    aligned = pltpu.roll(tail, tail.shape[1] - IN_SPLITS[IN_NAMES.index("gla_gk")], 1)
    seg = lambda name: aligned[:, off[name] - off["gla_g"]:off[name] - off["gla_g"] + IN_SPLITS[IN_NAMES.index(name)]]
    gate_ref[0, rows, :] = jnp.concatenate([p[:, off["na_g"]:off["gla_q"]], seg("gla_g"), seg("hg_g")],
                                           axis=1).astype(BF16)
    scan_ref[0, rows, :] = jnp.concatenate(
        [p[:, off["gla_v"]:off["gla_gk"]], p[:, off["gla_q"]:off["gla_v"]], tail[:, :LANES],
         aligned[:, off["hg_q"] - off["gla_g"]:off["hg_g"] - off["gla_g"]]], axis=1)


def _projection_io(mod_l, layer, n_batch, t_all):
    in_specs = [
        pl.BlockSpec(mod_l.shape, lambda b, t: (0, 0)),
        pl.BlockSpec((1, D_MODEL), lambda b, t: (0, 0)),
        pl.BlockSpec((None, D_MODEL, W_IN_PAD), lambda b, t: (layer, 0, 0), pipeline_mode=pl.Buffered(1)),
    ]
    pair_spec = pl.BlockSpec((1, NA_HEADS // 2, TOKEN_TILE, LANES), lambda b, t: (b, 0, t, 0))
    pair_shape = jax.ShapeDtypeStruct((n_batch, NA_HEADS // 2, t_all, LANES), BF16)
    vt_spec = pl.BlockSpec((1, NA_HEADS // 2, TOKEN_TILE // LANES, LANES, LANES), lambda b, t: (b, 0, t, 0, 0))
    vt_shape = jax.ShapeDtypeStruct((n_batch, NA_HEADS // 2, t_all // LANES, LANES, LANES), BF16)

    def slab(width, dtype):
        return (pl.BlockSpec((1, TOKEN_TILE, width), lambda b, t: (b, t, 0)),
                jax.ShapeDtypeStruct((n_batch, t_all, width), dtype))

    gate_spec, gate_shape = slab(W_GATE, BF16)
    scan_spec, scan_shape = slab(W_GLA + W_HG, F32)
    return (in_specs, [pair_spec, pair_spec, vt_spec, gate_spec, scan_spec],
            [pair_shape, pair_shape, vt_shape, gate_shape, scan_shape])


def _in_proj(streams, mod_l, norm_w_l, w_in_p, layer, n_batch, t_all):
    in_specs, out_specs, out_shape = _projection_io(mod_l, layer, n_batch, t_all)
    return pl.pallas_call(
        functools.partial(_in_proj_kernel, len(streams), n_batch),
        grid=(n_batch, t_all // TOKEN_TILE),
        in_specs=_stream_specs(len(streams)) + in_specs,
        out_specs=out_specs,
        out_shape=out_shape,
        compiler_params=pltpu.CompilerParams(
            dimension_semantics=("parallel", "parallel"), vmem_limit_bytes=VMEM_LIMIT),
        name="in_proj",
    )(*streams, mod_l, norm_w_l, w_in_p)


def _na_bias_cases():
    cases = []
    for key_row_ok, rel in (
        (lambda lr, lk: lk < NA_WIN_H, lambda lr, lk: lk - lr + NA_WIN_H - 1),
        (lambda lr, lk: lr <= lk < lr + NA_WIN_H, lambda lr, lk: lk - lr + NA_WIN_H // 2 - 1),
        (lambda lr, lk: lk >= NA_KEY_ROWS - NA_WIN_H, lambda lr, lk: lk - lr - 1),
    ):
        cases.append([[rel(lr, lk) if key_row_ok(lr, lk) else None for lk in range(NA_KEY_ROWS)]
                      for lr in range(NA_Q_ROWS)])
    return cases


def _na_kernel(tile_off, q_ref, k_ref, vt_ref, rt_ref, o_ref, bias_ref):
    tb = pl.program_id(1) + tile_off
    n_pairs = q_ref.shape[1]
    lane = lax.broadcasted_iota(jnp.int32, (1, LANES), 1)
    q_scale = [jnp.where((lane // HEAD_DIM) == hh, 1.0, 0.0).astype(BF16) for hh in range(2)]
    heads = [(p, hh) for p in range(n_pairs) for hh in range(2)]
    ctx_tiles = CTX_LEN // LANES

    @pl.when((pl.program_id(0) == 0) & (pl.program_id(1) == 0))
    def _build_bias_tables():
        minus_inf = jnp.full((GRID_W, GRID_W), -jnp.inf, F32)

        def per_head(h, carry):
            for case, rel in enumerate(_na_bias_cases()):
                for lk in range(NA_KEY_ROWS):
                    for lr in range(0, NA_Q_ROWS, 2):
                        tiles = [minus_inf if rel[r][lk] is None else rt_ref[h, rel[r][lk]] for r in (lr, lr + 1)]
                        bias_ref[case, h, lk * GRID_W:(lk + 1) * GRID_W, lr * GRID_W:(lr + 2) * GRID_W] = (
                            jnp.concatenate(tiles, axis=1))
            return carry

        lax.fori_loop(0, NA_HEADS, per_head, 0)

    def vt_tiles(p, first, count):
        tiles = vt_ref[0, p, pl.ds(first, count)]
        return jnp.concatenate([tiles[i] for i in range(count)], axis=1)

    def finish(outs):
        for p in range(n_pairs):
            o_t = jnp.concatenate([outs[(p, 0)], outs[(p, 1)]], axis=0)
            o_ref[0, :, p * LANES:(p + 1) * LANES] = jnp.transpose(o_t).astype(o_ref.dtype)

    def pipelined(scores_of, attend):
        outs, issued = {}, {}
        for i in range(len(heads) + NA_PIPELINE_LAG):
            if i < len(heads):
                issued[i] = scores_of(*heads[i])
            j = i - NA_PIPELINE_LAG
            if j >= 0:
                outs[heads[j]] = attend(*heads[j], *issued.pop(j))
        finish(outs)

    def softmax_pv(hh, vt_parts, s_parts):
        m = functools.reduce(jnp.maximum, [jnp.max(s, axis=0, keepdims=True) for s in s_parts])
        o_t = None
        for vt, s in zip(vt_parts, s_parts):
            ones = jnp.ones((HEAD_DIM, vt.shape[1]), BF16)
            lhs = jnp.concatenate([vt[:HEAD_DIM], ones] if hh == 0 else [ones, vt[HEAD_DIM:]], axis=0)
            part = _dot(lhs, jnp.exp2(s - m).astype(BF16))
            o_t = part if o_t is None else o_t + part
        own, other = (0, HEAD_DIM) if hh == 0 else (HEAD_DIM, 0)
        return o_t[own:own + HEAD_DIM] / o_t[other:other + 1]

    @pl.when(tb == 0)
    def _ctx_queries():
        def scores_of(p, hh):
            return (_dot_nt(k_ref[0, p, 0:CTX_LEN, :], q_ref[0, p] * q_scale[hh]),)

        def attend(p, hh, s):
            return softmax_pv(hh, [vt_tiles(p, 0, ctx_tiles)], [s])

        pipelined(scores_of, attend)

    @pl.when(tb > 0)
    def _latent_queries():
        g = tb - 1
        n_groups = pl.num_programs(1) - 1 + tile_off
        ws = jnp.clip(NA_Q_ROWS * g - NA_WIN_H // 2, 0, GRID_W - NA_KEY_ROWS)
        case = (g > 0).astype(jnp.int32) + (g == n_groups - 1).astype(jnp.int32)
        start = pl.multiple_of(CTX_LEN + ws * GRID_W, LANES)
        first_tile = ctx_tiles + ws * GRID_W // LANES

        def scores_of(p, hh):
            qh = q_ref[0, p] * q_scale[hh]
            s_nb = _dot_nt(k_ref[0, p, pl.ds(start, NA_KEYS), :], qh) + bias_ref[case, 2 * p + hh]
            return s_nb, _dot_nt(k_ref[0, p, 0:CTX_LEN, :], qh)

        def attend(p, hh, s_nb, s_cx):
            return softmax_pv(hh, [vt_tiles(p, first_tile, NA_KEYS // LANES), vt_tiles(p, 0, ctx_tiles)],
                              [s_nb, s_cx])

        pipelined(scores_of, attend)


def _neighborhood_attention(q, k, vt, rpb_tiles, layer, with_ctx_queries):
    n_batch, n_pairs, t_all, _ = q.shape
    n_tiles = t_all // TOKEN_TILE
    tile_off = 0 if with_ctx_queries else CTX_LEN // TOKEN_TILE
    n_steps = n_tiles - tile_off
    t_out = n_steps * TOKEN_TILE
    return pl.pallas_call(
        functools.partial(_na_kernel, tile_off),
        grid=(n_batch, n_steps),
        in_specs=[
            pl.BlockSpec((1, n_pairs, TOKEN_TILE, LANES), lambda b, g: (b, 0, g + tile_off, 0)),
            pl.BlockSpec((1, n_pairs, t_all, LANES), lambda b, g: (b, 0, 0, 0)),
            pl.BlockSpec((1, n_pairs, t_all // LANES, LANES, LANES), lambda b, g: (b, 0, 0, 0, 0)),
            pl.BlockSpec((None,) + rpb_tiles.shape[1:], lambda b, g: (layer, 0, 0, 0, 0),
                         pipeline_mode=pl.Buffered(1)),
        ],
        out_specs=pl.BlockSpec((1, TOKEN_TILE, NA_WIDTH), lambda b, g: (b, g, 0)),
        out_shape=jax.ShapeDtypeStruct((n_batch, t_out, NA_WIDTH), BF16),
        scratch_shapes=[pltpu.VMEM((3, NA_HEADS, NA_KEYS, TOKEN_TILE), F32)],
        compiler_params=pltpu.CompilerParams(
            dimension_semantics=("arbitrary", "arbitrary"),
            vmem_limit_bytes=VMEM_LIMIT),
        name="neighborhood_attention",
    )(q, k, vt, rpb_tiles)


def _na_bias_tiles(rpb):
    col = np.arange(GRID_W)
    col_start = np.clip(col - NA_WIN_W // 2, 0, GRID_W - NA_WIN_W)
    col_in = (col[:, None] >= col_start[None, :]) & (col[:, None] < col_start[None, :] + NA_WIN_W)
    col_idx = np.clip(col[:, None] - col[None, :], 1 - NA_WIN_W, NA_WIN_W - 1) + NA_WIN_W - 1
    one_hot = (col_idx[None] == np.arange(2 * NA_WIN_W - 1)[:, None, None]).astype(np.float32)
    tiles = jnp.einsum("lhrw,wkj->lhrkj", rpb.astype(F32), jnp.asarray(one_hot),
                       precision=lax.Precision.HIGHEST)
    return jnp.where(col_in, tiles * LOG2_E, -jnp.inf)


def _scan_constants():
    idx = np.arange(SCAN_BLOCK)
    c = idx // CHUNK
    pos = idx % CHUNK
    same = c[:, None] == c[None, :]
    return {"cum_f": (same & (pos[None, :] <= pos[:, None])).astype(np.float32),
            "cum_b": (same & (pos[None, :] >= pos[:, None])).astype(np.float32)}


def _block_matrix(n_rows, n_cols, block_shape, dtype, block_at):
    zero = jnp.zeros(block_shape, dtype)
    rows = []
    for r in range(n_rows):
        blocks = [block_at(r, c) for c in range(n_cols)]
        rows.append(jnp.concatenate([zero if b is None else b for b in blocks], axis=1))
    return jnp.concatenate(rows, axis=0)


def _run_interleaved(stagewise):
    results = [None] * len(stagewise)
    live = list(range(len(stagewise)))
    while live:
        for i in list(live):
            try:
                next(stagewise[i])
            except StopIteration as done:
                results[i] = done.value
                live.remove(i)
    return results


def _scan_direction(load_inputs, cum, st_ref, reverse):
    q, k, v, lf = load_inputs()
    yield
    n = SCAN_BLOCK
    hd = q.shape[1]
    dk = hd // SCAN_HEADS
    pack = 2
    low_half = lax.broadcasted_iota(jnp.int32, (1, LANES), 1) < HEAD_DIM
    lane_k = lax.broadcasted_iota(jnp.int32, (1, hd), 1) // dk
    lane_v = lax.broadcasted_iota(jnp.int32, (1, SCAN_HEADS * HEAD_DIM), 1) // HEAD_DIM

    def lane_mask(cond):
        return jnp.where(cond, 1.0, 0.0).astype(BF16)

    half_masks = {True: lane_mask(low_half), False: lane_mask(jnp.logical_not(low_half))}
    k_masks = [lane_mask(lane_k == h) for h in range(SCAN_HEADS)]
    v_masks = [lane_mask(lane_v == h) for h in range(SCAN_HEADS)]

    def keep_half(x, low):
        return x * half_masks[low]

    hi = lf.astype(BF16)
    mid = (lf - hi.astype(F32)).astype(BF16)
    g = _dot(cum, hi) + _dot(cum, mid)
    yield
    ref_pos, last_pos = (CHUNK // 2, 0) if reverse else (CHUNK // 2 - 1, CHUNK - 1)
    to_ref, to_last, decay = [], [], []
    for c in range(CHUNKS_PER_BLOCK):
        gc = g[c * CHUNK:(c + 1) * CHUNK]
        g_last = gc[last_pos:last_pos + 1]
        to_ref.append(gc - gc[ref_pos:ref_pos + 1])
        to_last.append(g_last - gc)
        decay.append(jnp.exp(g_last))
    x_ref = jnp.concatenate(to_ref, axis=0)
    a = jnp.exp(x_ref)
    a_inv = jnp.exp(-x_ref)
    e_in = jnp.exp(g)
    e_out = jnp.exp(jnp.concatenate(to_last, axis=0))

    q_t = (q * a).astype(BF16)
    k_t_tr = jnp.transpose(k * a_inv).astype(BF16)
    q_in = (q * e_in).astype(BF16)
    k_out = (k * e_out).astype(BF16)
    v_b = v.astype(BF16)

    k_blk = _block_matrix(SCAN_HEADS, SCAN_HEADS, (dk, n), BF16,
                          lambda r, c: k_t_tr[r * dk:(r + 1) * dk] if r == c else None)
    scores = _dot(q_t, k_blk)
    yield

    v_tr = jnp.transpose(v).astype(BF16)
    token_pair = lax.broadcasted_iota(jnp.int32, (1, n), 1) // (CHUNK * pack)
    pair_masks = [lane_mask(token_pair == m) for m in range(CHUNKS_PER_BLOCK // pack)]
    heads_per_group = LANES // dk
    incs = []
    for grp in range(hd // LANES):
        heads = range(grp * heads_per_group, (grp + 1) * heads_per_group)
        v_blk = jnp.concatenate(
            [jnp.concatenate([v_tr[h * HEAD_DIM:(h + 1) * HEAD_DIM] * pm for h in heads], axis=1)
             for pm in pair_masks], axis=0)
        k_rows = []
        for h in heads:
            k_h = (k_out * k_masks[h])[:, grp * LANES:(grp + 1) * LANES]
            k_rows.append(_block_matrix(
                CHUNKS_PER_BLOCK, pack, (CHUNK, LANES), BF16,
                lambda c, par, k_h=k_h: k_h[c * CHUNK:(c + 1) * CHUNK] if c % pack == par else None))
        incs.append(_dot(v_blk, jnp.concatenate(k_rows, axis=0)))
    yield

    tri =jnp.concatenate([cum.astype(F32)] * SCAN_HEADS, axis=1) > 0.5
    p = jnp.where(tri, scores, 0.0).astype(BF16)
    v_stack = jnp.concatenate([v_b * v_masks[h] for h in range(SCAN_HEADS)], axis=0)
    o_intra = _dot(p, v_stack)
    yield

    st = st_ref[...]
    states = [None] * CHUNKS_PER_BLOCK
    order = range(CHUNKS_PER_BLOCK - 1, -1, -1) if reverse else range(CHUNKS_PER_BLOCK)
    for c in order:
        states[c] = st
        m, par = divmod(c, pack)
        inc_c = jnp.concatenate([inc[m * HEAD_DIM:(m + 1) * HEAD_DIM, par * LANES:(par + 1) * LANES]
                                 for inc in incs], axis=1)
        st = decay[c] * st + inc_c
    st_ref[...] = st

    q_blk = _block_matrix(CHUNKS_PER_BLOCK, CHUNKS_PER_BLOCK, (CHUNK, hd), BF16,
                          lambda r, c: q_in[r * CHUNK:(r + 1) * CHUNK] if r == c else None)
    s_parts = []
    for s in states:
        s_tr = jnp.transpose(jnp.concatenate([s, s], axis=0)).astype(BF16)

        def s_block(h, j, s_tr=s_tr):
            return keep_half(s_tr[h * dk:(h + 1) * dk], h % 2 == 0) if h // 2 == j else None

        s_parts.append(_block_matrix(SCAN_HEADS, SCAN_HEADS // 2, (dk, LANES), BF16, s_block))
    return o_intra + _dot(q_blk, jnp.concatenate(s_parts, axis=0))


def _rope(x, cos, sin_signed):
    lane = lax.broadcasted_iota(jnp.int32, (1, x.shape[1]), 1)
    first = (lane % 16) < 8
    partner = jnp.where(first, pltpu.roll(x, x.shape[1] - 8, 1), pltpu.roll(x, 8, 1))
    return x * cos + partner * sin_signed


def _scan_kernel(n_ctx_blocks,
                 in_f_ref, in_b_ref,
                 cos_f_ref, sin_f_ref, cos_b_ref, sin_b_ref,
                 wgk_ref, bgk_ref, lb_ref, cum_f_ref, cum_b_ref,
                 of_ref, ob_ref,
                 sgf_ref, sgb_ref, shf_ref, shb_ref):
    del n_ctx_blocks
    gla_cols, hg_cols = pl.ds(0, GLA_WIDTH), pl.ds(GLA_WIDTH, HGRN_WIDTH)

    @pl.when(pl.program_id(1) == 0)
    def _reset():
        for ref in (sgf_ref, sgb_ref, shf_ref, shb_ref):
            ref[...] = jnp.zeros_like(ref)

    def gla_inputs(in_ref, cos_ref, sin_ref, direction, rows):
        blk = in_ref[0, rows, 0:W_GLA]
        v = blk[:, 0:GLA_WIDTH]
        q = blk[:, GLA_WIDTH:GLA_WIDTH + GLA_KEY]
        k = blk[:, GLA_WIDTH + GLA_KEY:GLA_WIDTH + 2 * GLA_KEY]
        gk = blk[:, GLA_WIDTH + 2 * GLA_KEY:]
        cos = cos_ref[rows, :]
        sin = sin_ref[rows, :]
        q = _rope(q, cos, sin) * (GLA_KEY // SCAN_HEADS) ** -0.5
        k = _rope(k, cos, sin)
        lo = direction * GLA_KEY
        z = _dot(gk.astype(BF16), wgk_ref[:, lo:lo + GLA_KEY]) + bgk_ref[:, lo:lo + GLA_KEY]
        log_sig = jnp.minimum(z, 0.0) - jnp.log(1.0 + jnp.exp(-jnp.abs(z)))
        return q, k, v, log_sig / GLA_GATE_NORM

    def hgrn_inputs(in_ref, direction, rows):
        blk = in_ref[0, rows, W_GLA:W_GLA + W_HG]
        q = blk[:, 0:HGRN_WIDTH] * HEAD_DIM ** -0.5
        f_raw = blk[:, (1 + direction) * HGRN_WIDTH:(2 + direction) * HGRN_WIDTH]
        v = blk[:, 3 * HGRN_WIDTH:]
        lb = lb_ref[direction:direction + 1, :]
        f = lb + (1.0 - lb) * _sigmoid(f_raw)
        return q, 1.0 - f, v, jnp.log(f)

    jobs, dests = [], []
    for s in range(SCAN_SUB):
        rows_f = pl.ds(s * SCAN_BLOCK, SCAN_BLOCK)
        rows_b = pl.ds((SCAN_SUB - 1 - s) * SCAN_BLOCK, SCAN_BLOCK)
        jobs += [
            _scan_direction(functools.partial(gla_inputs, in_f_ref, cos_f_ref, sin_f_ref, 0, rows_f),
                            cum_f_ref[...], sgf_ref, reverse=False),
            _scan_direction(functools.partial(hgrn_inputs, in_f_ref, 0, rows_f),
                            cum_f_ref[...], shf_ref, reverse=False),
            _scan_direction(functools.partial(gla_inputs, in_b_ref, cos_b_ref, sin_b_ref, 1, rows_b),
                            cum_b_ref[...], sgb_ref, reverse=True),
            _scan_direction(functools.partial(hgrn_inputs, in_b_ref, 1, rows_b),
                            cum_b_ref[...], shb_ref, reverse=True),
        ]
        dests += [(of_ref, rows_f, gla_cols), (of_ref, rows_f, hg_cols),
                  (ob_ref, rows_b, gla_cols), (ob_ref, rows_b, hg_cols)]
    for (ref, rows, cols), val in zip(dests, _run_interleaved(jobs)):
        ref[0, rows, cols] = val.astype(ref.dtype)


def _scans(scan_in, cos_t, sin_t, wgk, bgk, lb, consts):
    n_batch, t_all, _ = scan_in.shape
    step_rows = SCAN_SUB * SCAN_BLOCK
    assert CTX_LEN % step_rows == 0 and t_all % step_rows == 0
    n_blocks = t_all // step_rows
    n_ctx_blocks = CTX_LEN // step_rows

    def fwd(i):
        return i

    def bwd(i):
        return jnp.where(i < n_ctx_blocks, n_ctx_blocks - 1 - i, n_blocks - 1 + n_ctx_blocks - i)

    def tok_spec(width, order):
        return pl.BlockSpec((1, step_rows, width), lambda b, i: (b, order(i), 0))

    def tab_spec(order):
        return pl.BlockSpec((step_rows, LANES), lambda b, i: (order(i), 0))

    def const_spec(arr):
        return pl.BlockSpec(arr.shape, lambda b, i: (0,) * arr.ndim)

    const_arrays = [wgk, bgk, lb, consts["cum_f"], consts["cum_b"]]
    out_shape = jax.ShapeDtypeStruct((n_batch, t_all, GLA_WIDTH + HGRN_WIDTH), BF16)
    return pl.pallas_call(
        functools.partial(_scan_kernel, n_ctx_blocks),
        grid=(n_batch, n_blocks),
        in_specs=[tok_spec(W_GLA + W_HG, fwd), tok_spec(W_GLA + W_HG, bwd),
                  tab_spec(fwd), tab_spec(fwd), tab_spec(bwd), tab_spec(bwd)]
                 + [const_spec(a) for a in const_arrays],
        out_specs=[tok_spec(GLA_WIDTH + HGRN_WIDTH, fwd), tok_spec(GLA_WIDTH + HGRN_WIDTH, bwd)],
        out_shape=[out_shape] * 2,
        scratch_shapes=[pltpu.VMEM((HEAD_DIM, GLA_KEY), F32), pltpu.VMEM((HEAD_DIM, GLA_KEY), F32),
                        pltpu.VMEM((HEAD_DIM, HGRN_WIDTH), F32), pltpu.VMEM((HEAD_DIM, HGRN_WIDTH), F32)],
        compiler_params=pltpu.CompilerParams(
            dimension_semantics=("parallel", "arbitrary"), vmem_limit_bytes=VMEM_LIMIT),
        name="bidirectional_scans",
    )(scan_in, scan_in, cos_t, sin_t, cos_t, sin_t, *const_arrays)


def _rope_tables(t_lat):
    half = GLA_KEY // SCAN_HEADS // 2
    inv_freq = 1.0 / (ROPE_BASE ** (np.arange(0, half, 2, dtype=np.float32) / half))
    t = np.arange(t_lat)
    lane = np.arange(LANES) % (2 * half)
    pos = np.where(lane[None, :] < half, (t // GRID_W)[:, None], (t % GRID_W)[:, None]).astype(np.float32)
    ang = (pos * inv_freq[(lane % half) % (half // 2)][None, :]).astype(np.float32)
    first = ((lane % half) < half // 2)[None, :]
    cos = np.cos(ang)
    sin = np.where(first, -np.sin(ang), np.sin(ang))
    cos = np.concatenate([np.ones((CTX_LEN, LANES), np.float32), cos], axis=0)
    sin = np.concatenate([np.zeros((CTX_LEN, LANES), np.float32), sin], axis=0)
    return jnp.asarray(cos, F32), jnp.asarray(sin, F32)


N_MIX_REFS = 9


def _mix_rows(x, rows, row, ona_ref, of_ref, ob_ref, gate_ref, mod_ref, gnw_ref, hnw_ref, bd_ref, w_ref):
    gt = mod_ref[pl.ds(row, 1), :][:, 2 * D_MODEL:]
    bd = bd_ref[...]
    o_scan = of_ref[0, rows, :].astype(F32) + ob_ref[0, rows, :].astype(F32)
    sq = o_scan * o_scan
    hi = sq.astype(BF16)
    lo = (sq - hi.astype(F32)).astype(BF16)
    yield
    ms = [_dot(hi[:, c], bd) + _dot(lo[:, c], bd)
          for c in (slice(0, GLA_WIDTH), slice(GLA_WIDTH, GLA_WIDTH + HGRN_WIDTH))]
    yield
    norm_w = jnp.concatenate([gnw_ref[...], hnw_ref[...]], axis=1)
    o_norm = o_scan * lax.rsqrt(jnp.concatenate(ms, axis=1) + NORM_EPS) * norm_w
    o_cat = jnp.concatenate([ona_ref[0, rows, :].astype(F32), o_norm], axis=1)
    act = (o_cat * _silu(gate_ref[0, rows, :].astype(F32))).astype(BF16)
    yield
    y = _dot(act, w_ref[...])
    yield
    return x + gt * y


def _final_rows(x_ref, rows, refs):
    xn = yield from _mix_rows(x_ref[0, rows, :], rows, pl.program_id(0), *refs[:N_MIX_REFS])
    fw_ref, out_ref = refs[N_MIX_REFS:]
    ms = jnp.mean(xn * xn, axis=-1, keepdims=True)
    out_ref[0, rows, :] = xn * lax.rsqrt(ms + NORM_EPS) * fw_ref[...]


def _final_out_kernel(x_ref, *refs):
    _run_interleaved([_final_rows(x_ref, rows, refs) for rows in _row_blocks(x_ref.shape[1])])


def _out_in_rows(x, s, rows, row, mix_refs, proj_in, stream_ref, proj_out):
    xn = yield from _mix_rows(x, rows, row, *mix_refs)
    stream_ref[0, rows, :] = xn
    yield from _project_rows(xn, s, rows, row, *proj_in, *proj_out)


def _out_in_kernel(n_streams, n_batch, *refs):
    x, row = _token_tile(refs[:n_streams], n_batch)
    mix_refs = refs[n_streams:n_streams + N_MIX_REFS]
    rest = refs[n_streams + N_MIX_REFS:]
    proj_in, stream_ref, proj_out = rest[:3], rest[3], rest[4:]
    _run_interleaved([_out_in_rows(x[rows], s, rows, row, mix_refs, proj_in, stream_ref, proj_out)
                      for s, rows in enumerate(_row_blocks(x.shape[0]))])


def _token_spec(width, tile, first_token):
    if first_token % tile == 0:
        return pl.BlockSpec((1, tile, width), lambda b, t: (b, t + first_token // tile, 0))
    return pl.BlockSpec((pl.Element(1), pl.Element(tile), pl.Element(width)),
                        lambda b, t: (b, pl.multiple_of(first_token + t * tile, TOKEN_TILE), 0))


def _mix_io(o_na, scans, gates, mod_l, gla_nw, hg_nw, bd, w_out_b, layer, tile=TOKEN_TILE, first_token=0):
    def tok(width):
        return _token_spec(width, tile, first_token)

    def const_spec(arr):
        return pl.BlockSpec(arr.shape, lambda b, t: (0,) * arr.ndim)

    consts = [mod_l, gla_nw, hg_nw, bd]
    specs = ([pl.BlockSpec((1, tile, NA_WIDTH), lambda b, t: (b, t, 0))]
             + [tok(GLA_WIDTH + HGRN_WIDTH)] * 2
             + [tok(W_GATE)] + [const_spec(a) for a in consts]
             + [pl.BlockSpec((None,) + w_out_b.shape[1:], lambda b, t: (layer, 0, 0))])
    return specs, [o_na, *scans, gates, *consts, w_out_b]


def _final_out(streams, mix_args, final_w, n_batch, t_all):
    (stream,) = streams
    t_lat = t_all - CTX_LEN
    assert t_lat % FINAL_TILE == 0
    mix_specs, mix_arrays = _mix_io(*mix_args, tile=FINAL_TILE, first_token=CTX_LEN)
    return pl.pallas_call(
        _final_out_kernel,
        grid=(n_batch, t_lat // FINAL_TILE),
        in_specs=[_token_spec(D_MODEL, FINAL_TILE, CTX_LEN)] + mix_specs
                 + [pl.BlockSpec((1, D_MODEL), lambda b, t: (0, 0))],
        out_specs=pl.BlockSpec((1, FINAL_TILE, D_MODEL), lambda b, t: (b, t, 0)),
        out_shape=jax.ShapeDtypeStruct((n_batch, t_lat, D_MODEL), F32),
        compiler_params=pltpu.CompilerParams(
            dimension_semantics=("parallel", "parallel"), vmem_limit_bytes=VMEM_LIMIT),
        name="out_proj_final",
    )(stream, *mix_arrays, final_w)


def _out_in_proj(streams, mix_args, mod_next, norm_w_next, w_in_p, next_layer, n_batch, t_all):
    mix_specs, mix_arrays = _mix_io(*mix_args)
    proj_specs, out_specs, out_shape = _projection_io(mod_next, next_layer, n_batch, t_all)
    stream_spec = pl.BlockSpec((1, TOKEN_TILE, D_MODEL), lambda b, t: (b, t, 0))
    stream_shape = jax.ShapeDtypeStruct((n_batch, t_all, D_MODEL), F32)
    outs = pl.pallas_call(
        functools.partial(_out_in_kernel, len(streams), n_batch),
        grid=(n_batch, t_all // TOKEN_TILE),
        in_specs=_stream_specs(len(streams)) + mix_specs + proj_specs,
        out_specs=[stream_spec] + out_specs,
        out_shape=[stream_shape] + out_shape,
        compiler_params=pltpu.CompilerParams(
            dimension_semantics=("parallel", "parallel"), vmem_limit_bytes=VMEM_LIMIT),
        name="out_in_proj",
    )(*streams, *mix_arrays, mod_next, norm_w_next, w_in_p)
    return outs[0], outs[1:]


def _pad_w_in(w):
    assert w.shape[-1] == IN_OFFS[-1]
    return jnp.pad(w.astype(BF16), [(0, 0)] * (w.ndim - 1) + [(0, W_IN_PAD - w.shape[-1])])


def kernel(x, c, ctx, c_ctx, ada_w, ada_b, norm_w, w_in, na_rpb, gla_w_gk, gla_b_gk, gla_norm_w,
           hgrn_lower_bounds, hgrn_norm_w, w_out, final_norm_w):
    n_batch, t_lat, _ = x.shape
    depth = ada_w.shape[0]
    t_all = CTX_LEN + t_lat
    assert ctx.shape[1] == CTX_LEN and t_lat % TOKEN_TILE == 0 and t_lat // GRID_W >= NA_KEY_ROWS

    lbs = jnp.cumsum(jax.nn.softmax(hgrn_lower_bounds.astype(F32), axis=0), axis=0)
    lbs = lbs - lbs[0:1]
    mod_rows = -(-(n_batch + 1) // 8) * 8
    cc = jnp.zeros((mod_rows, D_MODEL), F32).at[:n_batch].set(c).at[n_batch].set(c_ctx)
    mod = _modulation(cc, ada_w, ada_b)

    consts = {k_: jnp.asarray(v_, BF16) for k_, v_ in _scan_constants().items()}
    cos_t, sin_t = _rope_tables(t_lat)
    head_of = np.arange(GLA_WIDTH) // HEAD_DIM
    bd = jnp.asarray((head_of[:, None] == head_of[None, :]).astype(np.float32) / HEAD_DIM, BF16)

    w_in_p = _pad_w_in(w_in)
    w_out_b = w_out.astype(BF16)
    rpb_tiles = _na_bias_tiles(na_rpb)
    wgk = jnp.zeros((depth, LANES, 2 * GLA_KEY), F32)
    wgk = wgk.at[:, 0:GLA_GK_RANK, 0:GLA_KEY].set(gla_w_gk[:, 0])
    wgk = wgk.at[:, GLA_GK_RANK:2 * GLA_GK_RANK, GLA_KEY:].set(gla_w_gk[:, 1]).astype(BF16)

    streams = [ctx, x]
    projections = _in_proj(streams, mod[0], norm_w[0].reshape(1, D_MODEL), w_in_p, 0, n_batch, t_all)
    for layer in range(depth):
        last = layer == depth - 1
        bgk = gla_b_gk[layer].reshape(1, 2 * GLA_KEY)
        gla_nw = jnp.tile(gla_norm_w[layer], SCAN_HEADS).reshape(1, GLA_WIDTH)
        hg_nw = jnp.tile(hgrn_norm_w[layer], SCAN_HEADS).reshape(1, HGRN_WIDTH)

        q, k, vt, gates, scan_in = projections
        o_na = _neighborhood_attention(q, k, vt, rpb_tiles, layer, with_ctx_queries=not last)
        scans = _scans(scan_in, cos_t, sin_t, wgk[layer], bgk, lbs[layer], consts)
        mix_args = (o_na, scans, gates, mod[layer], gla_nw, hg_nw, bd, w_out_b, layer)
        if last:
            return _final_out(streams, mix_args, final_norm_w.reshape(1, D_MODEL), n_batch, t_all)
        stream, projections = _out_in_proj(streams, mix_args, mod[layer + 1],
                                           norm_w[layer + 1].reshape(1, D_MODEL), w_in_p, layer + 1,
                                           n_batch, t_all)
        streams = [stream]
```

```python
import functools

import numpy as np
import jax
import jax.numpy as jnp
from jax import lax
from jax.experimental import pallas as pl
from jax.experimental.pallas import tpu as pltpu

F32 = jnp.float32
BF16 = jnp.bfloat16

D_MODEL = 1024
CTX_LEN = 256
GRID_W = 64
NORM_EPS = 1e-6
HEAD_DIM = 64
NA_WIDTH = 512
NA_HEADS = 8
NA_WIN_H = 8
NA_WIN_W = 16
GLA_WIDTH = 256
GLA_KEY = 128
GLA_GK_RANK = 16
GLA_GATE_NORM = 16.0
HGRN_WIDTH = 256
SCAN_HEADS = 4
CHUNK = 16
ROPE_BASE = 10000.0

LOG2_E = float(np.log2(np.e))
NA_Q_SCALE = HEAD_DIM ** -0.5 * LOG2_E

LANES = 128
TOKEN_TILE = 256
NA_PIPELINE_LAG = 4
FINAL_TILE = 512
SCAN_BLOCK = 128
SCAN_SUB = 2
CHUNKS_PER_BLOCK = SCAN_BLOCK // CHUNK
NA_Q_ROWS = TOKEN_TILE // GRID_W
NA_KEY_ROWS = NA_Q_ROWS + NA_WIN_H
NA_KEYS = NA_KEY_ROWS * GRID_W
assert NA_KEYS % LANES == 0 and (NA_Q_ROWS * GRID_W) % LANES == 0 and NA_WIN_H % 4 == 0
VMEM_LIMIT = 56 * 1024 * 1024

IN_NAMES = ("na_q", "na_k", "na_v", "na_g", "gla_q", "gla_k", "gla_v", "gla_gk", "gla_g",
            "hg_q", "hg_ff", "hg_fb", "hg_i", "hg_g")
IN_SPLITS = (512, 512, 512, 512, 128, 128, 256, 32, 256, 256, 256, 256, 256, 256)
IN_OFFS = tuple(int(v) for v in np.cumsum((0,) + IN_SPLITS))
W_GATE = NA_WIDTH + GLA_WIDTH + HGRN_WIDTH
W_GLA = GLA_WIDTH + 2 * GLA_KEY + LANES
W_HG = 4 * HGRN_WIDTH
W_IN_PAD = -(-IN_OFFS[-1] // LANES) * LANES


def _dot(a, b):
    return jnp.dot(a, b, preferred_element_type=F32)


def _dot_nt(a, b):
    return lax.dot_general(a, b, (((1,), (1,)), ((), ())), preferred_element_type=F32)


def _sigmoid(x):
    return 1.0 / (1.0 + jnp.exp(-x))


def _silu(x):
    return x * _sigmoid(x)


def _mod_kernel(c_ref, w_ref, b_ref, o_ref):
    s = _silu(c_ref[...]).astype(BF16)
    o_ref[0] = _dot(s, w_ref[0].astype(BF16)) + b_ref[0]


def _modulation(cc, ada_w, ada_b):
    depth = ada_w.shape[0]
    rows = cc.shape[0]
    return pl.pallas_call(
        _mod_kernel,
        grid=(depth, 3),
        in_specs=[
            pl.BlockSpec((rows, D_MODEL), lambda l, j: (0, 0)),
            pl.BlockSpec((1, D_MODEL, D_MODEL), lambda l, j: (l, 0, j)),
            pl.BlockSpec((1, 1, D_MODEL), lambda l, j: (l, 0, j)),
        ],
        out_specs=pl.BlockSpec((1, rows, D_MODEL), lambda l, j: (l, 0, j)),
        out_shape=jax.ShapeDtypeStruct((depth, rows, 3 * D_MODEL), F32),
        compiler_params=pltpu.CompilerParams(
            dimension_semantics=("parallel", "parallel"), vmem_limit_bytes=VMEM_LIMIT),
        name="adaln_mod",
    )(cc, ada_w, ada_b.reshape(depth, 1, 3 * D_MODEL))


def _token_tile(x_refs, n_batch, tile_off=0):
    t = pl.program_id(1) + tile_off
    is_ctx = t < CTX_LEN // TOKEN_TILE
    x = jnp.where(is_ctx, x_refs[0][0], x_refs[1][0]) if len(x_refs) == 2 else x_refs[0][0]
    return x, jnp.where(is_ctx, n_batch, pl.program_id(0))


def _stream_specs(n_streams, tile_off=0):
    n_ctx_tiles = CTX_LEN // TOKEN_TILE
    if n_streams == 2:
        return [pl.BlockSpec((1, TOKEN_TILE, D_MODEL), lambda b, t: (b, jnp.minimum(t, n_ctx_tiles - 1), 0)),
                pl.BlockSpec((1, TOKEN_TILE, D_MODEL), lambda b, t: (b, jnp.maximum(t - n_ctx_tiles, 0), 0))]
    return [pl.BlockSpec((1, TOKEN_TILE, D_MODEL), lambda b, t: (b, t + tile_off, 0))]


def _in_proj_kernel(n_streams, n_batch, *refs):
    x, row = _token_tile(refs[:n_streams], n_batch)
    _project_tile(x, row, *refs[n_streams:])


def _project_tile(x, row, mod_ref, nw_ref, w_ref, q_ref, k_ref, v_ref, gate_ref, scan_ref):
    m = mod_ref[pl.ds(row, 1), :]
    sh = m[:, :D_MODEL]
    sc = m[:, D_MODEL:2 * D_MODEL]
    ms = jnp.mean(x * x, axis=-1, keepdims=True)
    y = x * lax.rsqrt(ms + NORM_EPS) * nw_ref[...]
    h = y * (1.0 + sc) + sh
    p = _dot(h.astype(BF16), w_ref[...])
    for pr in range(NA_HEADS // 2):
        lo = pr * LANES
        q_ref[0, pr] = (p[:, lo:lo + LANES] * NA_Q_SCALE).astype(BF16)
        k_ref[0, pr] = p[:, NA_WIDTH + lo:NA_WIDTH + lo + LANES].astype(BF16)
        for s in range(TOKEN_TILE // LANES):
            v_tile = p[s * LANES:(s + 1) * LANES, 2 * NA_WIDTH + lo:2 * NA_WIDTH + lo + LANES]
            v_ref[0, pr, s] = jnp.transpose(v_tile).astype(BF16)
    off = dict(zip(IN_NAMES, IN_OFFS))
    tail = p[:, off["gla_gk"]:]
    aligned = pltpu.roll(tail, tail.shape[1] - IN_SPLITS[IN_NAMES.index("gla_gk")], 1)
    seg = lambda name: aligned[:, off[name] - off["gla_g"]:off[name] - off["gla_g"] + IN_SPLITS[IN_NAMES.index(name)]]
    gate_ref[0] = jnp.concatenate([p[:, off["na_g"]:off["gla_q"]], seg("gla_g"), seg("hg_g")],
                                  axis=1).astype(BF16)
    scan_ref[0] = jnp.concatenate(
        [p[:, off["gla_v"]:off["gla_gk"]], p[:, off["gla_q"]:off["gla_v"]], tail[:, :LANES],
         aligned[:, off["hg_q"] - off["gla_g"]:off["hg_g"] - off["gla_g"]]], axis=1)


def _projection_io(mod_l, layer, n_batch, t_all):
    in_specs = [
        pl.BlockSpec(mod_l.shape, lambda b, t: (0, 0)),
        pl.BlockSpec((1, D_MODEL), lambda b, t: (0, 0)),
        pl.BlockSpec((None, D_MODEL, W_IN_PAD), lambda b, t: (layer, 0, 0), pipeline_mode=pl.Buffered(1)),
    ]
    pair_spec = pl.BlockSpec((1, NA_HEADS // 2, TOKEN_TILE, LANES), lambda b, t: (b, 0, t, 0))
    pair_shape = jax.ShapeDtypeStruct((n_batch, NA_HEADS // 2, t_all, LANES), BF16)
    vt_spec = pl.BlockSpec((1, NA_HEADS // 2, TOKEN_TILE // LANES, LANES, LANES), lambda b, t: (b, 0, t, 0, 0))
    vt_shape = jax.ShapeDtypeStruct((n_batch, NA_HEADS // 2, t_all // LANES, LANES, LANES), BF16)

    def slab(width, dtype):
        return (pl.BlockSpec((1, TOKEN_TILE, width), lambda b, t: (b, t, 0)),
                jax.ShapeDtypeStruct((n_batch, t_all, width), dtype))

    gate_spec, gate_shape = slab(W_GATE, BF16)
    scan_spec, scan_shape = slab(W_GLA + W_HG, F32)
    return (in_specs, [pair_spec, pair_spec, vt_spec, gate_spec, scan_spec],
            [pair_shape, pair_shape, vt_shape, gate_shape, scan_shape])


def _in_proj(streams, mod_l, norm_w_l, w_in_p, layer, n_batch, t_all):
    in_specs, out_specs, out_shape = _projection_io(mod_l, layer, n_batch, t_all)
    return pl.pallas_call(
        functools.partial(_in_proj_kernel, len(streams), n_batch),
        grid=(n_batch, t_all // TOKEN_TILE),
        in_specs=_stream_specs(len(streams)) + in_specs,
        out_specs=out_specs,
        out_shape=out_shape,
        compiler_params=pltpu.CompilerParams(
            dimension_semantics=("parallel", "parallel"), vmem_limit_bytes=VMEM_LIMIT),
        name="in_proj",
    )(*streams, mod_l, norm_w_l, w_in_p)


def _na_bias_cases():
    cases = []
    for key_row_ok, rel in (
        (lambda lr, lk: lk < NA_WIN_H, lambda lr, lk: lk - lr + NA_WIN_H - 1),
        (lambda lr, lk: lr <= lk < lr + NA_WIN_H, lambda lr, lk: lk - lr + NA_WIN_H // 2 - 1),
        (lambda lr, lk: lk >= NA_KEY_ROWS - NA_WIN_H, lambda lr, lk: lk - lr - 1),
    ):
        cases.append([[rel(lr, lk) if key_row_ok(lr, lk) else None for lk in range(NA_KEY_ROWS)]
                      for lr in range(NA_Q_ROWS)])
    return cases


def _na_kernel(tile_off, q_ref, k_ref, vt_ref, rt_ref, o_ref, bias_ref):
    tb = pl.program_id(1) + tile_off
    n_pairs = q_ref.shape[1]
    lane = lax.broadcasted_iota(jnp.int32, (1, LANES), 1)
    q_scale = [jnp.where((lane // HEAD_DIM) == hh, 1.0, 0.0).astype(BF16) for hh in range(2)]
    heads = [(p, hh) for p in range(n_pairs) for hh in range(2)]
    ctx_tiles = CTX_LEN // LANES

    @pl.when((pl.program_id(0) == 0) & (pl.program_id(1) == 0))
    def _build_bias_tables():
        minus_inf = jnp.full((GRID_W, GRID_W), -jnp.inf, F32)

        def per_head(h, carry):
            for case, rel in enumerate(_na_bias_cases()):
                for lk in range(NA_KEY_ROWS):
                    for lr in range(0, NA_Q_ROWS, 2):
                        tiles = [minus_inf if rel[r][lk] is None else rt_ref[h, rel[r][lk]] for r in (lr, lr + 1)]
                        bias_ref[case, h, lk * GRID_W:(lk + 1) * GRID_W, lr * GRID_W:(lr + 2) * GRID_W] = (
                            jnp.concatenate(tiles, axis=1))
            return carry

        lax.fori_loop(0, NA_HEADS, per_head, 0)

    def vt_tiles(p, first, count):
        tiles = vt_ref[0, p, pl.ds(first, count)]
        return jnp.concatenate([tiles[i] for i in range(count)], axis=1)

    def finish(outs):
        for p in range(n_pairs):
            o_t = jnp.concatenate([outs[(p, 0)], outs[(p, 1)]], axis=0)
            o_ref[0, :, p * LANES:(p + 1) * LANES] = jnp.transpose(o_t).astype(o_ref.dtype)

    def pipelined(scores_of, attend):
        outs, issued = {}, {}
        for i in range(len(heads) + NA_PIPELINE_LAG):
            if i < len(heads):
                issued[i] = scores_of(*heads[i])
            j = i - NA_PIPELINE_LAG
            if j >= 0:
                outs[heads[j]] = attend(*heads[j], *issued.pop(j))
        finish(outs)

    def softmax_pv(hh, vt_parts, s_parts):
        m = functools.reduce(jnp.maximum, [jnp.max(s, axis=0, keepdims=True) for s in s_parts])
        o_t = None
        for vt, s in zip(vt_parts, s_parts):
            ones = jnp.ones((HEAD_DIM, vt.shape[1]), BF16)
            lhs = jnp.concatenate([vt[:HEAD_DIM], ones] if hh == 0 else [ones, vt[HEAD_DIM:]], axis=0)
            part = _dot(lhs, jnp.exp2(s - m).astype(BF16))
            o_t = part if o_t is None else o_t + part
        own, other = (0, HEAD_DIM) if hh == 0 else (HEAD_DIM, 0)
        return o_t[own:own + HEAD_DIM] / o_t[other:other + 1]

    @pl.when(tb == 0)
    def _ctx_queries():
        def scores_of(p, hh):
            return (_dot_nt(k_ref[0, p, 0:CTX_LEN, :], q_ref[0, p] * q_scale[hh]),)

        def attend(p, hh, s):
            return softmax_pv(hh, [vt_tiles(p, 0, ctx_tiles)], [s])

        pipelined(scores_of, attend)

    @pl.when(tb > 0)
    def _latent_queries():
        g = tb - 1
        n_groups = pl.num_programs(1) - 1 + tile_off
        ws = jnp.clip(NA_Q_ROWS * g - NA_WIN_H // 2, 0, GRID_W - NA_KEY_ROWS)
        case = (g > 0).astype(jnp.int32) + (g == n_groups - 1).astype(jnp.int32)
        start = pl.multiple_of(CTX_LEN + ws * GRID_W, LANES)
        first_tile = ctx_tiles + ws * GRID_W // LANES

        def scores_of(p, hh):
            qh = q_ref[0, p] * q_scale[hh]
            s_nb = _dot_nt(k_ref[0, p, pl.ds(start, NA_KEYS), :], qh) + bias_ref[case, 2 * p + hh]
            return s_nb, _dot_nt(k_ref[0, p, 0:CTX_LEN, :], qh)

        def attend(p, hh, s_nb, s_cx):
            return softmax_pv(hh, [vt_tiles(p, first_tile, NA_KEYS // LANES), vt_tiles(p, 0, ctx_tiles)],
                              [s_nb, s_cx])

        pipelined(scores_of, attend)


def _neighborhood_attention(q, k, vt, rpb_tiles, layer, with_ctx_queries):
    n_batch, n_pairs, t_all, _ = q.shape
    n_tiles = t_all // TOKEN_TILE
    tile_off = 0 if with_ctx_queries else CTX_LEN // TOKEN_TILE
    n_steps = n_tiles - tile_off
    t_out = n_steps * TOKEN_TILE
    return pl.pallas_call(
        functools.partial(_na_kernel, tile_off),
        grid=(n_batch, n_steps),
        in_specs=[
            pl.BlockSpec((1, n_pairs, TOKEN_TILE, LANES), lambda b, g: (b, 0, g + tile_off, 0)),
            pl.BlockSpec((1, n_pairs, t_all, LANES), lambda b, g: (b, 0, 0, 0)),
            pl.BlockSpec((1, n_pairs, t_all // LANES, LANES, LANES), lambda b, g: (b, 0, 0, 0, 0)),
            pl.BlockSpec((None,) + rpb_tiles.shape[1:], lambda b, g: (layer, 0, 0, 0, 0),
                         pipeline_mode=pl.Buffered(1)),
        ],
        out_specs=pl.BlockSpec((1, TOKEN_TILE, NA_WIDTH), lambda b, g: (b, g, 0)),
        out_shape=jax.ShapeDtypeStruct((n_batch, t_out, NA_WIDTH), BF16),
        scratch_shapes=[pltpu.VMEM((3, NA_HEADS, NA_KEYS, TOKEN_TILE), F32)],
        compiler_params=pltpu.CompilerParams(
            dimension_semantics=("arbitrary", "arbitrary"),
            vmem_limit_bytes=VMEM_LIMIT),
        name="neighborhood_attention",
    )(q, k, vt, rpb_tiles)


def _na_bias_tiles(rpb):
    col = np.arange(GRID_W)
    col_start = np.clip(col - NA_WIN_W // 2, 0, GRID_W - NA_WIN_W)
    col_in = (col[:, None] >= col_start[None, :]) & (col[:, None] < col_start[None, :] + NA_WIN_W)
    col_idx = np.clip(col[:, None] - col[None, :], 1 - NA_WIN_W, NA_WIN_W - 1) + NA_WIN_W - 1
    one_hot = (col_idx[None] == np.arange(2 * NA_WIN_W - 1)[:, None, None]).astype(np.float32)
    tiles = jnp.einsum("lhrw,wkj->lhrkj", rpb.astype(F32), jnp.asarray(one_hot),
                       precision=lax.Precision.HIGHEST)
    return jnp.where(col_in, tiles * LOG2_E, -jnp.inf)


def _scan_constants():
    idx = np.arange(SCAN_BLOCK)
    c = idx // CHUNK
    pos = idx % CHUNK
    same = c[:, None] == c[None, :]
    return {"cum_f": (same & (pos[None, :] <= pos[:, None])).astype(np.float32),
            "cum_b": (same & (pos[None, :] >= pos[:, None])).astype(np.float32)}


def _block_matrix(n_rows, n_cols, block_shape, dtype, block_at):
    zero = jnp.zeros(block_shape, dtype)
    rows = []
    for r in range(n_rows):
        blocks = [block_at(r, c) for c in range(n_cols)]
        rows.append(jnp.concatenate([zero if b is None else b for b in blocks], axis=1))
    return jnp.concatenate(rows, axis=0)


def _run_interleaved(stagewise):
    results = [None] * len(stagewise)
    live = list(range(len(stagewise)))
    while live:
        for i in list(live):
            try:
                next(stagewise[i])
            except StopIteration as done:
                results[i] = done.value
                live.remove(i)
    return results


def _scan_direction(load_inputs, cum, st_ref, reverse):
    q, k, v, lf = load_inputs()
    yield
    n = SCAN_BLOCK
    hd = q.shape[1]
    dk = hd // SCAN_HEADS
    pack = 2
    low_half = lax.broadcasted_iota(jnp.int32, (1, LANES), 1) < HEAD_DIM
    lane_k = lax.broadcasted_iota(jnp.int32, (1, hd), 1) // dk
    lane_v = lax.broadcasted_iota(jnp.int32, (1, SCAN_HEADS * HEAD_DIM), 1) // HEAD_DIM

    def lane_mask(cond):
        return jnp.where(cond, 1.0, 0.0).astype(BF16)

    half_masks = {True: lane_mask(low_half), False: lane_mask(jnp.logical_not(low_half))}
    k_masks = [lane_mask(lane_k == h) for h in range(SCAN_HEADS)]
    v_masks = [lane_mask(lane_v == h) for h in range(SCAN_HEADS)]

    def keep_half(x, low):
        return x * half_masks[low]

    hi = lf.astype(BF16)
    mid = (lf - hi.astype(F32)).astype(BF16)
    g = _dot(cum, hi) + _dot(cum, mid)
    yield
    ref_pos, last_pos = (CHUNK // 2, 0) if reverse else (CHUNK // 2 - 1, CHUNK - 1)
    to_ref, to_last, decay = [], [], []
    for c in range(CHUNKS_PER_BLOCK):
        gc = g[c * CHUNK:(c + 1) * CHUNK]
        g_last = gc[last_pos:last_pos + 1]
        to_ref.append(gc - gc[ref_pos:ref_pos + 1])
        to_last.append(g_last - gc)
        decay.append(jnp.exp(g_last))
    x_ref = jnp.concatenate(to_ref, axis=0)
    a = jnp.exp(x_ref)
    a_inv = jnp.exp(-x_ref)
    e_in = jnp.exp(g)
    e_out = jnp.exp(jnp.concatenate(to_last, axis=0))

    q_t = (q * a).astype(BF16)
    k_t_tr = jnp.transpose(k * a_inv).astype(BF16)
    q_in = (q * e_in).astype(BF16)
    k_out = (k * e_out).astype(BF16)
    v_b = v.astype(BF16)

    k_blk = _block_matrix(SCAN_HEADS, SCAN_HEADS, (dk, n), BF16,
                          lambda r, c: k_t_tr[r * dk:(r + 1) * dk] if r == c else None)
    scores = _dot(q_t, k_blk)
    yield

    v_tr = jnp.transpose(v).astype(BF16)
    token_pair = lax.broadcasted_iota(jnp.int32, (1, n), 1) // (CHUNK * pack)
    pair_masks = [lane_mask(token_pair == m) for m in range(CHUNKS_PER_BLOCK // pack)]
    heads_per_group = LANES // dk
    incs = []
    for grp in range(hd // LANES):
        heads = range(grp * heads_per_group, (grp + 1) * heads_per_group)
        v_blk = jnp.concatenate(
            [jnp.concatenate([v_tr[h * HEAD_DIM:(h + 1) * HEAD_DIM] * pm for h in heads], axis=1)
             for pm in pair_masks], axis=0)
        k_rows = []
        for h in heads:
            k_h = (k_out * k_masks[h])[:, grp * LANES:(grp + 1) * LANES]
            k_rows.append(_block_matrix(
                CHUNKS_PER_BLOCK, pack, (CHUNK, LANES), BF16,
                lambda c, par, k_h=k_h: k_h[c * CHUNK:(c + 1) * CHUNK] if c % pack == par else None))
        incs.append(_dot(v_blk, jnp.concatenate(k_rows, axis=0)))
    yield

    tri = jnp.concatenate([cum.astype(F32)] * SCAN_HEADS, axis=1) > 0.5
    p = jnp.where(tri, scores, 0.0).astype(BF16)
    v_stack = jnp.concatenate([v_b * v_masks[h] for h in range(SCAN_HEADS)], axis=0)
    o_intra = _dot(p, v_stack)
    yield

    st = st_ref[...]
    states = [None] * CHUNKS_PER_BLOCK
    order = range(CHUNKS_PER_BLOCK - 1, -1, -1) if reverse else range(CHUNKS_PER_BLOCK)
    for c in order:
        states[c] = st
        m, par = divmod(c, pack)
        inc_c = jnp.concatenate([inc[m * HEAD_DIM:(m + 1) * HEAD_DIM, par * LANES:(par + 1) * LANES]
                                 for inc in incs], axis=1)
        st = decay[c] * st + inc_c
    st_ref[...] = st

    q_blk = _block_matrix(CHUNKS_PER_BLOCK, CHUNKS_PER_BLOCK, (CHUNK, hd), BF16,
                          lambda r, c: q_in[r * CHUNK:(r + 1) * CHUNK] if r == c else None)
    s_parts = []
    for s in states:
        s_tr = jnp.transpose(jnp.concatenate([s, s], axis=0)).astype(BF16)

        def s_block(h, j, s_tr=s_tr):
            return keep_half(s_tr[h * dk:(h + 1) * dk], h % 2 == 0) if h // 2 == j else None

        s_parts.append(_block_matrix(SCAN_HEADS, SCAN_HEADS // 2, (dk, LANES), BF16, s_block))
    return o_intra + _dot(q_blk, jnp.concatenate(s_parts, axis=0))


def _rope(x, cos, sin_signed):
    lane = lax.broadcasted_iota(jnp.int32, (1, x.shape[1]), 1)
    first = (lane % 16) < 8
    partner = jnp.where(first, pltpu.roll(x, x.shape[1] - 8, 1), pltpu.roll(x, 8, 1))
    return x * cos + partner * sin_signed


def _scan_kernel(n_ctx_blocks,
                 in_f_ref, in_b_ref,
                 cos_f_ref, sin_f_ref, cos_b_ref, sin_b_ref,
                 wgk_ref, bgk_ref, lb_ref, cum_f_ref, cum_b_ref,
                 of_ref, ob_ref,
                 sgf_ref, sgb_ref, shf_ref, shb_ref):
    del n_ctx_blocks
    gla_cols, hg_cols = pl.ds(0, GLA_WIDTH), pl.ds(GLA_WIDTH, HGRN_WIDTH)

    @pl.when(pl.program_id(1) == 0)
    def _reset():
        for ref in (sgf_ref, sgb_ref, shf_ref, shb_ref):
            ref[...] = jnp.zeros_like(ref)

    def gla_inputs(in_ref, cos_ref, sin_ref, direction, rows):
        blk = in_ref[0, rows, 0:W_GLA]
        v = blk[:, 0:GLA_WIDTH]
        q = blk[:, GLA_WIDTH:GLA_WIDTH + GLA_KEY]
        k = blk[:, GLA_WIDTH + GLA_KEY:GLA_WIDTH + 2 * GLA_KEY]
        gk = blk[:, GLA_WIDTH + 2 * GLA_KEY:]
        cos = cos_ref[rows, :]
        sin = sin_ref[rows, :]
        q = _rope(q, cos, sin) * (GLA_KEY // SCAN_HEADS) ** -0.5
        k = _rope(k, cos, sin)
        lo = direction * GLA_KEY
        z = _dot(gk.astype(BF16), wgk_ref[:, lo:lo + GLA_KEY]) + bgk_ref[:, lo:lo + GLA_KEY]
        log_sig = jnp.minimum(z, 0.0) - jnp.log(1.0 + jnp.exp(-jnp.abs(z)))
        return q, k, v, log_sig / GLA_GATE_NORM

    def hgrn_inputs(in_ref, direction, rows):
        blk = in_ref[0, rows, W_GLA:W_GLA + W_HG]
        q = blk[:, 0:HGRN_WIDTH] * HEAD_DIM ** -0.5
        f_raw = blk[:, (1 + direction) * HGRN_WIDTH:(2 + direction) * HGRN_WIDTH]
        v = blk[:, 3 * HGRN_WIDTH:]
        lb = lb_ref[direction:direction + 1, :]
        f = lb + (1.0 - lb) * _sigmoid(f_raw)
        return q, 1.0 - f, v, jnp.log(f)

    jobs, dests = [], []
    for s in range(SCAN_SUB):
        rows_f = pl.ds(s * SCAN_BLOCK, SCAN_BLOCK)
        rows_b = pl.ds((SCAN_SUB - 1 - s) * SCAN_BLOCK, SCAN_BLOCK)
        jobs += [
            _scan_direction(functools.partial(gla_inputs, in_f_ref, cos_f_ref, sin_f_ref, 0, rows_f),
                            cum_f_ref[...], sgf_ref, reverse=False),
            _scan_direction(functools.partial(hgrn_inputs, in_f_ref, 0, rows_f),
                            cum_f_ref[...], shf_ref, reverse=False),
            _scan_direction(functools.partial(gla_inputs, in_b_ref, cos_b_ref, sin_b_ref, 1, rows_b),
                            cum_b_ref[...], sgb_ref, reverse=True),
            _scan_direction(functools.partial(hgrn_inputs, in_b_ref, 1, rows_b),
                            cum_b_ref[...], shb_ref, reverse=True),
        ]
        dests += [(of_ref, rows_f, gla_cols), (of_ref, rows_f, hg_cols),
                  (ob_ref, rows_b, gla_cols), (ob_ref, rows_b, hg_cols)]
    for (ref, rows, cols), val in zip(dests, _run_interleaved(jobs)):
        ref[0, rows, cols] = val.astype(ref.dtype)


def _scans(scan_in, cos_t, sin_t, wgk, bgk, lb, consts):
    n_batch, t_all, _ = scan_in.shape
    step_rows = SCAN_SUB * SCAN_BLOCK
    assert CTX_LEN % step_rows == 0 and t_all % step_rows == 0
    n_blocks = t_all // step_rows
    n_ctx_blocks = CTX_LEN // step_rows

    def fwd(i):
        return i

    def bwd(i):
        return jnp.where(i < n_ctx_blocks, n_ctx_blocks - 1 - i, n_blocks - 1 + n_ctx_blocks - i)

    def tok_spec(width, order):
        return pl.BlockSpec((1, step_rows, width), lambda b, i: (b, order(i), 0))

    def tab_spec(order):
        return pl.BlockSpec((step_rows, LANES), lambda b, i: (order(i), 0))

    def const_spec(arr):
        return pl.BlockSpec(arr.shape, lambda b, i: (0,) * arr.ndim)

    const_arrays = [wgk, bgk, lb, consts["cum_f"], consts["cum_b"]]
    out_shape = jax.ShapeDtypeStruct((n_batch, t_all, GLA_WIDTH + HGRN_WIDTH), BF16)
    return pl.pallas_call(
        functools.partial(_scan_kernel, n_ctx_blocks),
        grid=(n_batch, n_blocks),
        in_specs=[tok_spec(W_GLA + W_HG, fwd), tok_spec(W_GLA + W_HG, bwd),
                  tab_spec(fwd), tab_spec(fwd), tab_spec(bwd), tab_spec(bwd)]
                 + [const_spec(a) for a in const_arrays],
        out_specs=[tok_spec(GLA_WIDTH + HGRN_WIDTH, fwd), tok_spec(GLA_WIDTH + HGRN_WIDTH, bwd)],
        out_shape=[out_shape] * 2,
        scratch_shapes=[pltpu.VMEM((HEAD_DIM, GLA_KEY), F32), pltpu.VMEM((HEAD_DIM, GLA_KEY), F32),
                        pltpu.VMEM((HEAD_DIM, HGRN_WIDTH), F32), pltpu.VMEM((HEAD_DIM, HGRN_WIDTH), F32)],
        compiler_params=pltpu.CompilerParams(
            dimension_semantics=("parallel", "arbitrary"), vmem_limit_bytes=VMEM_LIMIT),
        name="bidirectional_scans",
    )(scan_in, scan_in, cos_t, sin_t, cos_t, sin_t, *const_arrays)


def _rope_tables(t_lat):
    half = GLA_KEY // SCAN_HEADS // 2
    inv_freq = 1.0 / (ROPE_BASE ** (np.arange(0, half, 2, dtype=np.float32) / half))
    t = np.arange(t_lat)
    lane = np.arange(LANES) % (2 * half)
    pos = np.where(lane[None, :] < half, (t // GRID_W)[:, None], (t % GRID_W)[:, None]).astype(np.float32)
    ang = (pos * inv_freq[(lane % half) % (half // 2)][None, :]).astype(np.float32)
    first = ((lane % half) < half // 2)[None, :]
    cos = np.cos(ang)
    sin = np.where(first, -np.sin(ang), np.sin(ang))
    cos = np.concatenate([np.ones((CTX_LEN, LANES), np.float32), cos], axis=0)
    sin = np.concatenate([np.zeros((CTX_LEN, LANES), np.float32), sin], axis=0)
    return jnp.asarray(cos, F32), jnp.asarray(sin, F32)


N_MIX_REFS = 9


def _mix_tile(x, row, ona_ref, of_ref, ob_ref, gate_ref, mod_ref, gnw_ref, hnw_ref, bd_ref, w_ref):
    gt = mod_ref[pl.ds(row, 1), :][:, 2 * D_MODEL:]
    bd = bd_ref[...]

    def head_norm(o, w):
        sq = o * o
        hi = sq.astype(BF16)
        lo = (sq - hi.astype(F32)).astype(BF16)
        ms = _dot(hi, bd) + _dot(lo, bd)
        return o * lax.rsqrt(ms + NORM_EPS) * w

    o_scan = of_ref[0].astype(F32) + ob_ref[0].astype(F32)
    o_gla = head_norm(o_scan[:, :GLA_WIDTH], gnw_ref[...])
    o_hg = head_norm(o_scan[:, GLA_WIDTH:], hnw_ref[...])
    o_cat = jnp.concatenate([ona_ref[0].astype(F32), o_gla, o_hg], axis=1)
    act = o_cat * _silu(gate_ref[0].astype(F32))
    y = _dot(act.astype(BF16), w_ref[...])
    return x + gt * y


def _final_out_kernel(x_ref, *refs):
    xn = _mix_tile(x_ref[0], pl.program_id(0), *refs[:N_MIX_REFS])
    fw_ref, out_ref = refs[N_MIX_REFS:]
    ms = jnp.mean(xn * xn, axis=-1, keepdims=True)
    out_ref[0] = xn * lax.rsqrt(ms + NORM_EPS) * fw_ref[...]


def _out_in_kernel(n_streams, n_batch, *refs):
    x, row = _token_tile(refs[:n_streams], n_batch)
    xn = _mix_tile(x, row, *refs[n_streams:n_streams + N_MIX_REFS])
    rest = refs[n_streams + N_MIX_REFS:]
    proj_in, stream_ref, proj_out = rest[:3], rest[3], rest[4:]
    stream_ref[0] = xn
    _project_tile(xn, row, *proj_in, *proj_out)


def _token_spec(width, tile, first_token):
    if first_token % tile == 0:
        return pl.BlockSpec((1, tile, width), lambda b, t: (b, t + first_token // tile, 0))
    return pl.BlockSpec((pl.Element(1), pl.Element(tile), pl.Element(width)),
                        lambda b, t: (b, pl.multiple_of(first_token + t * tile, TOKEN_TILE), 0))


def _mix_io(o_na, scans, gates, mod_l, gla_nw, hg_nw, bd, w_out_b, layer, tile=TOKEN_TILE, first_token=0):
    def tok(width):
        return _token_spec(width, tile, first_token)

    def const_spec(arr):
        return pl.BlockSpec(arr.shape, lambda b, t: (0,) * arr.ndim)

    consts = [mod_l, gla_nw, hg_nw, bd]
    specs = ([pl.BlockSpec((1, tile, NA_WIDTH), lambda b, t: (b, t, 0))]
             + [tok(GLA_WIDTH + HGRN_WIDTH)] * 2
             + [tok(W_GATE)] + [const_spec(a) for a in consts]
             + [pl.BlockSpec((None,) + w_out_b.shape[1:], lambda b, t: (layer, 0, 0))])
    return specs, [o_na, *scans, gates, *consts, w_out_b]


def _final_out(streams, mix_args, final_w, n_batch, t_all):
    (stream,) = streams
    t_lat = t_all - CTX_LEN
    assert t_lat % FINAL_TILE == 0
    mix_specs, mix_arrays = _mix_io(*mix_args, tile=FINAL_TILE, first_token=CTX_LEN)
    return pl.pallas_call(
        _final_out_kernel,
        grid=(n_batch, t_lat // FINAL_TILE),
        in_specs=[_token_spec(D_MODEL, FINAL_TILE, CTX_LEN)] + mix_specs
                 + [pl.BlockSpec((1, D_MODEL), lambda b, t: (0, 0))],
        out_specs=pl.BlockSpec((1, FINAL_TILE, D_MODEL), lambda b, t: (b, t, 0)),
        out_shape=jax.ShapeDtypeStruct((n_batch, t_lat, D_MODEL), F32),
        compiler_params=pltpu.CompilerParams(
            dimension_semantics=("parallel", "parallel"), vmem_limit_bytes=VMEM_LIMIT),
        name="out_proj_final",
    )(stream, *mix_arrays, final_w)


def _out_in_proj(streams, mix_args, mod_next, norm_w_next, w_in_p, next_layer, n_batch, t_all):
    mix_specs, mix_arrays = _mix_io(*mix_args)
    proj_specs, out_specs, out_shape = _projection_io(mod_next, next_layer, n_batch, t_all)
    stream_spec = pl.BlockSpec((1, TOKEN_TILE, D_MODEL), lambda b, t: (b, t, 0))
    stream_shape = jax.ShapeDtypeStruct((n_batch, t_all, D_MODEL), F32)
    outs = pl.pallas_call(
        functools.partial(_out_in_kernel, len(streams), n_batch),
        grid=(n_batch, t_all // TOKEN_TILE),
        in_specs=_stream_specs(len(streams)) + mix_specs + proj_specs,
        out_specs=[stream_spec] + out_specs,
        out_shape=[stream_shape] + out_shape,
        compiler_params=pltpu.CompilerParams(
            dimension_semantics=("parallel", "parallel"), vmem_limit_bytes=VMEM_LIMIT),
        name="out_in_proj",
    )(*streams, *mix_arrays, mod_next, norm_w_next, w_in_p)
    return outs[0], outs[1:]


def _pad_w_in(w):
    assert w.shape[-1] == IN_OFFS[-1]
    return jnp.pad(w.astype(BF16), [(0, 0)] * (w.ndim - 1) + [(0, W_IN_PAD - w.shape[-1])])


def kernel(x, c, ctx, c_ctx, ada_w, ada_b, norm_w, w_in, na_rpb, gla_w_gk, gla_b_gk, gla_norm_w,
           hgrn_lower_bounds, hgrn_norm_w, w_out, final_norm_w):
    n_batch, t_lat, _ = x.shape
    depth = ada_w.shape[0]
    t_all = CTX_LEN + t_lat
    assert ctx.shape[1] == CTX_LEN and t_lat % TOKEN_TILE == 0 and t_lat // GRID_W >= NA_KEY_ROWS

    lbs = jnp.cumsum(jax.nn.softmax(hgrn_lower_bounds.astype(F32), axis=0), axis=0)
    lbs = lbs - lbs[0:1]
    mod_rows = -(-(n_batch + 1) // 8) * 8
    cc = jnp.zeros((mod_rows, D_MODEL), F32).at[:n_batch].set(c).at[n_batch].set(c_ctx)
    mod = _modulation(cc, ada_w, ada_b)

    consts = {k_: jnp.asarray(v_, BF16) for k_, v_ in _scan_constants().items()}
    cos_t, sin_t = _rope_tables(t_lat)
    head_of = np.arange(GLA_WIDTH) // HEAD_DIM
    bd = jnp.asarray((head_of[:, None] == head_of[None, :]).astype(np.float32) / HEAD_DIM, BF16)

    w_in_p = _pad_w_in(w_in)
    w_out_b = w_out.astype(BF16)
    rpb_tiles = _na_bias_tiles(na_rpb)
    wgk = jnp.zeros((depth, LANES, 2 * GLA_KEY), F32)
    wgk = wgk.at[:, 0:GLA_GK_RANK, 0:GLA_KEY].set(gla_w_gk[:, 0])
    wgk = wgk.at[:, GLA_GK_RANK:2 * GLA_GK_RANK, GLA_KEY:].set(gla_w_gk[:, 1]).astype(BF16)

    streams = [ctx, x]
    projections = _in_proj(streams, mod[0], norm_w[0].reshape(1, D_MODEL), w_in_p, 0, n_batch, t_all)
    for layer in range(depth):
        last = layer == depth - 1
        bgk = gla_b_gk[layer].reshape(1, 2 * GLA_KEY)
        gla_nw = jnp.tile(gla_norm_w[layer], SCAN_HEADS).reshape(1, GLA_WIDTH)
        hg_nw = jnp.tile(hgrn_norm_w[layer], SCAN_HEADS).reshape(1, HGRN_WIDTH)

        q, k, vt, gates, scan_in = projections
        o_na = _neighborhood_attention(q, k, vt, rpb_tiles, layer, with_ctx_queries=not last)
        scans = _scans(scan_in, cos_t, sin_t, wgk[layer], bgk, lbs[layer], consts)
        mix_args = (o_na, scans, gates, mod[layer], gla_nw, hg_nw, bd, w_out_b, layer)
        if last:
            return _final_out(streams, mix_args, final_norm_w.reshape(1, D_MODEL), n_batch, t_all)
        stream, projections = _out_in_proj(streams, mix_args, mod[layer + 1],
                                           norm_w[layer + 1].reshape(1, D_MODEL), w_in_p, layer + 1,
                                           n_batch, t_all)
        streams = [stream]
```

```python
import functools

import numpy as np
import jax
import jax.numpy as jnp
from jax import lax
from jax.experimental import pallas as pl
from jax.experimental.pallas import tpu as pltpu

F32 = jnp.float32
BF16 = jnp.bfloat16

D_MODEL = 1024
CTX_LEN = 256
GRID_W = 64
NORM_EPS = 1e-6
HEAD_DIM = 64
NA_WIDTH = 512
NA_HEADS = 8
NA_WIN_H = 8
NA_WIN_W = 16
GLA_WIDTH = 256
GLA_KEY = 128
GLA_GK_RANK = 16
GLA_GATE_NORM = 16.0
HGRN_WIDTH = 256
SCAN_HEADS = 4
CHUNK = 16
ROPE_BASE = 10000.0

LOG2_E = float(np.log2(np.e))
NA_Q_SCALE = HEAD_DIM ** -0.5 * LOG2_E

LANES = 128
TOKEN_TILE = 256
NA_PIPELINE_LAG = 4
FINAL_TILE = 512
SCAN_BLOCK = 128
SCAN_SUB = 2
CHUNKS_PER_BLOCK = SCAN_BLOCK // CHUNK
NA_Q_ROWS = TOKEN_TILE // GRID_W
NA_KEY_ROWS = NA_Q_ROWS + NA_WIN_H
NA_KEYS = NA_KEY_ROWS * GRID_W
assert NA_KEYS % LANES == 0 and (NA_Q_ROWS * GRID_W) % LANES == 0 and NA_WIN_H % 4 == 0
VMEM_LIMIT = 56 * 1024 * 1024

IN_NAMES = ("na_q", "na_k", "na_v", "na_g", "gla_q", "gla_k", "gla_v", "gla_gk", "gla_g",
            "hg_q", "hg_ff", "hg_fb", "hg_i", "hg_g")
IN_SPLITS = (512, 512, 512, 512, 128, 128, 256, 32, 256, 256, 256, 256, 256, 256)
IN_OFFS = tuple(int(v) for v in np.cumsum((0,) + IN_SPLITS))
W_GATE = NA_WIDTH + GLA_WIDTH + HGRN_WIDTH
W_GLA = GLA_WIDTH + 2 * GLA_KEY + LANES
W_HG = 4 * HGRN_WIDTH
W_IN_PAD = -(-IN_OFFS[-1] // LANES) * LANES


def _dot(a, b):
    return jnp.dot(a, b, preferred_element_type=F32)


def _dot_nt(a, b):
    return lax.dot_general(a, b, (((1,), (1,)), ((), ())), preferred_element_type=F32)


def _sigmoid(x):
    return 1.0 / (1.0 + jnp.exp(-x))


def _silu(x):
    return x * _sigmoid(x)


def _mod_kernel(c_ref, w_ref, b_ref, o_ref):
    s = _silu(c_ref[...]).astype(BF16)
    o_ref[0] = _dot(s, w_ref[0].astype(BF16)) + b_ref[0]


def _modulation(cc, ada_w, ada_b):
    depth = ada_w.shape[0]
    rows = cc.shape[0]
    return pl.pallas_call(
        _mod_kernel,
        grid=(depth, 3),
        in_specs=[
            pl.BlockSpec((rows, D_MODEL), lambda l, j: (0, 0)),
            pl.BlockSpec((1, D_MODEL, D_MODEL), lambda l, j: (l, 0, j)),
            pl.BlockSpec((1, 1, D_MODEL), lambda l, j: (l, 0, j)),
        ],
        out_specs=pl.BlockSpec((1, rows, D_MODEL), lambda l, j: (l, 0, j)),
        out_shape=jax.ShapeDtypeStruct((depth, rows, 3 * D_MODEL), F32),
        compiler_params=pltpu.CompilerParams(
            dimension_semantics=("parallel", "parallel"), vmem_limit_bytes=VMEM_LIMIT),
        name="adaln_mod",
    )(cc, ada_w, ada_b.reshape(depth, 1, 3 * D_MODEL))


def _token_tile(x_refs, n_batch, tile_off=0):
    t = pl.program_id(1) + tile_off
    is_ctx = t < CTX_LEN // TOKEN_TILE
    x = jnp.where(is_ctx, x_refs[0][0], x_refs[1][0]) if len(x_refs) == 2 else x_refs[0][0]
    return x, jnp.where(is_ctx, n_batch, pl.program_id(0))


def _stream_specs(n_streams, tile_off=0):
    n_ctx_tiles = CTX_LEN // TOKEN_TILE
    if n_streams == 2:
        return [pl.BlockSpec((1, TOKEN_TILE, D_MODEL), lambda b, t: (b, jnp.minimum(t, n_ctx_tiles - 1), 0)),
                pl.BlockSpec((1, TOKEN_TILE, D_MODEL), lambda b, t: (b, jnp.maximum(t - n_ctx_tiles, 0), 0))]
    return [pl.BlockSpec((1, TOKEN_TILE, D_MODEL), lambda b, t: (b, t + tile_off, 0))]


def _stage_w_in(w_ref, wb_ref):
    width = w_ref.shape[1]
    whole = width // LANES * LANES

    @pl.when((pl.program_id(0) == 0) & (pl.program_id(1) == 0))
    def _cast():
        for r in range(0, w_ref.shape[0], LANES):
            rows = slice(r, r + LANES)
            wb_ref[rows, :whole] = w_ref[rows, :whole].astype(BF16)
            tail = w_ref[rows, whole:].astype(BF16)
            wb_ref[rows, whole:] = jnp.concatenate(
                [tail, jnp.zeros((LANES, wb_ref.shape[1] - width), BF16)], axis=1)


def _in_proj_kernel(n_streams, n_batch, *refs):
    x, row = _token_tile(refs[:n_streams], n_batch)
    mod_ref, nw_ref, w_ref = refs[n_streams:n_streams + 3]
    wb_ref = refs[-1]
    _stage_w_in(w_ref, wb_ref)
    _project_tile(x, row, mod_ref, nw_ref, wb_ref, *refs[n_streams + 3:-1])


def _project_tile(x, row, mod_ref, nw_ref, w_ref, q_ref, k_ref, v_ref, gate_ref, scan_ref):
    m = mod_ref[pl.ds(row, 1), :]
    sh = m[:, :D_MODEL]
    sc = m[:, D_MODEL:2 * D_MODEL]
    ms = jnp.mean(x * x, axis=-1, keepdims=True)
    y = x * lax.rsqrt(ms + NORM_EPS) * nw_ref[...]
    h = y * (1.0 + sc) + sh
    p = _dot(h.astype(BF16), w_ref[...])
    for pr in range(NA_HEADS // 2):
        lo = pr * LANES
        q_ref[0, pr] = (p[:, lo:lo + LANES] * NA_Q_SCALE).astype(BF16)
        k_ref[0, pr] = p[:, NA_WIDTH + lo:NA_WIDTH + lo + LANES].astype(BF16)
        for s in range(TOKEN_TILE // LANES):
            v_tile = p[s * LANES:(s + 1) * LANES, 2 * NA_WIDTH + lo:2 * NA_WIDTH + lo + LANES]
            v_ref[0, pr, s] = jnp.transpose(v_tile).astype(BF16)
    off = dict(zip(IN_NAMES, IN_OFFS))
    tail = p[:, off["gla_gk"]:]
    aligned = pltpu.roll(tail, tail.shape[1] - IN_SPLITS[IN_NAMES.index("gla_gk")], 1)
    seg = lambda name: aligned[:, off[name] - off["gla_g"]:off[name] - off["gla_g"] + IN_SPLITS[IN_NAMES.index(name)]]
    gate_ref[0] = jnp.concatenate([p[:, off["na_g"]:off["gla_q"]], seg("gla_g"), seg("hg_g")],
                                  axis=1).astype(BF16)
    scan_ref[0] = jnp.concatenate(
        [p[:, off["gla_v"]:off["gla_gk"]], p[:, off["gla_q"]:off["gla_v"]], tail[:, :LANES],
         aligned[:, off["hg_q"] - off["gla_g"]:off["hg_g"] - off["gla_g"]]], axis=1)


def _projection_io(mod_l, layer, n_batch, t_all):
    in_specs = [
        pl.BlockSpec(mod_l.shape, lambda b, t: (0, 0)),
        pl.BlockSpec((1, D_MODEL), lambda b, t: (0, 0)),
        pl.BlockSpec((None, D_MODEL, IN_OFFS[-1]), lambda b, t: (layer, 0, 0), pipeline_mode=pl.Buffered(1)),
    ]
    pair_spec = pl.BlockSpec((1, NA_HEADS // 2, TOKEN_TILE, LANES), lambda b, t: (b, 0, t, 0))
    pair_shape = jax.ShapeDtypeStruct((n_batch, NA_HEADS // 2, t_all, LANES), BF16)
    vt_spec = pl.BlockSpec((1, NA_HEADS // 2, TOKEN_TILE // LANES, LANES, LANES), lambda b, t: (b, 0, t, 0, 0))
    vt_shape = jax.ShapeDtypeStruct((n_batch, NA_HEADS // 2, t_all // LANES, LANES, LANES), BF16)

    def slab(width, dtype):
        return (pl.BlockSpec((1, TOKEN_TILE, width), lambda b, t: (b, t, 0)),
                jax.ShapeDtypeStruct((n_batch, t_all, width), dtype))

    gate_spec, gate_shape = slab(W_GATE, BF16)
    scan_spec, scan_shape = slab(W_GLA + W_HG, F32)
    return (in_specs, [pair_spec, pair_spec, vt_spec, gate_spec, scan_spec],
            [pair_shape, pair_shape, vt_shape, gate_shape, scan_shape])


def _in_proj(streams, mod_l, norm_w_l, w_in_p, layer, n_batch, t_all):
    in_specs, out_specs, out_shape = _projection_io(mod_l, layer, n_batch, t_all)
    return pl.pallas_call(
        functools.partial(_in_proj_kernel, len(streams), n_batch),
        grid=(n_batch, t_all // TOKEN_TILE),
        in_specs=_stream_specs(len(streams)) + in_specs,
        out_specs=out_specs,
        out_shape=out_shape,
        scratch_shapes=[pltpu.VMEM((D_MODEL, W_IN_PAD), BF16)],
        compiler_params=pltpu.CompilerParams(
            dimension_semantics=("arbitrary", "arbitrary"), vmem_limit_bytes=VMEM_LIMIT),
        name="in_proj",
    )(*streams, mod_l, norm_w_l, w_in_p)


def _na_bias_cases():
    cases = []
    for key_row_ok, rel in (
        (lambda lr, lk: lk < NA_WIN_H, lambda lr, lk: lk - lr + NA_WIN_H - 1),
        (lambda lr, lk: lr <= lk < lr + NA_WIN_H, lambda lr, lk: lk - lr + NA_WIN_H // 2 - 1),
        (lambda lr, lk: lk >= NA_KEY_ROWS - NA_WIN_H, lambda lr, lk: lk - lr - 1),
    ):
        cases.append([[rel(lr, lk) if key_row_ok(lr, lk) else None for lk in range(NA_KEY_ROWS)]
                      for lr in range(NA_Q_ROWS)])
    return cases


def _na_kernel(tile_off, q_ref, k_ref, vt_ref, rt_ref, o_ref, bias_ref):
    tb = pl.program_id(1) + tile_off
    n_pairs = q_ref.shape[1]
    lane = lax.broadcasted_iota(jnp.int32, (1, LANES), 1)
    q_scale = [jnp.where((lane // HEAD_DIM) == hh, 1.0, 0.0).astype(BF16) for hh in range(2)]
    heads = [(p, hh) for p in range(n_pairs) for hh in range(2)]
    ctx_tiles = CTX_LEN // LANES

    @pl.when((pl.program_id(0) == 0) & (pl.program_id(1) == 0))
    def _build_bias_tables():
        minus_inf = jnp.full((GRID_W, GRID_W), -jnp.inf, F32)

        def per_head(h, carry):
            for case, rel in enumerate(_na_bias_cases()):
                for lk in range(NA_KEY_ROWS):
                    for lr in range(0, NA_Q_ROWS, 2):
                        tiles = [minus_inf if rel[r][lk] is None else rt_ref[h, rel[r][lk]] for r in (lr, lr + 1)]
                        bias_ref[case, h, lk * GRID_W:(lk + 1) * GRID_W, lr * GRID_W:(lr + 2) * GRID_W] = (
                            jnp.concatenate(tiles, axis=1))
            return carry

        lax.fori_loop(0, NA_HEADS, per_head, 0)

    def vt_tiles(p, first, count):
        tiles = vt_ref[0, p, pl.ds(first, count)]
        return jnp.concatenate([tiles[i] for i in range(count)], axis=1)

    def finish(outs):
        for p in range(n_pairs):
            o_t = jnp.concatenate([outs[(p, 0)], outs[(p, 1)]], axis=0)
            o_ref[0, :, p * LANES:(p + 1) * LANES] = jnp.transpose(o_t).astype(o_ref.dtype)

    def pipelined(scores_of, attend):
        outs, issued = {}, {}
        for i in range(len(heads) + NA_PIPELINE_LAG):
            if i < len(heads):
                issued[i] = scores_of(*heads[i])
            j = i - NA_PIPELINE_LAG
            if j >= 0:
                outs[heads[j]] = attend(*heads[j], *issued.pop(j))
        finish(outs)

    def softmax_pv(hh, vt_parts, s_parts):
        m = functools.reduce(jnp.maximum, [jnp.max(s, axis=0, keepdims=True) for s in s_parts])
        o_t = None
        for vt, s in zip(vt_parts, s_parts):
            ones = jnp.ones((HEAD_DIM, vt.shape[1]), BF16)
            lhs = jnp.concatenate([vt[:HEAD_DIM], ones] if hh == 0 else [ones, vt[HEAD_DIM:]], axis=0)
            part = _dot(lhs, jnp.exp2(s - m).astype(BF16))
            o_t = part if o_t is None else o_t + part
        own, other = (0, HEAD_DIM) if hh == 0 else (HEAD_DIM, 0)
        return o_t[own:own + HEAD_DIM] / o_t[other:other + 1]

    @pl.when(tb == 0)
    def _ctx_queries():
        def scores_of(p, hh):
            return (_dot_nt(k_ref[0, p, 0:CTX_LEN, :], q_ref[0, p] * q_scale[hh]),)

        def attend(p, hh, s):
            return softmax_pv(hh, [vt_tiles(p, 0, ctx_tiles)], [s])

        pipelined(scores_of, attend)

    @pl.when(tb > 0)
    def _latent_queries():
        g = tb - 1
        n_groups = pl.num_programs(1) - 1 + tile_off
        ws = jnp.clip(NA_Q_ROWS * g - NA_WIN_H // 2, 0, GRID_W - NA_KEY_ROWS)
        case = (g > 0).astype(jnp.int32) + (g == n_groups - 1).astype(jnp.int32)
        start = pl.multiple_of(CTX_LEN + ws * GRID_W, LANES)
        first_tile = ctx_tiles + ws * GRID_W // LANES

        def scores_of(p, hh):
            qh = q_ref[0, p] * q_scale[hh]
            s_nb = _dot_nt(k_ref[0, p, pl.ds(start, NA_KEYS), :], qh) + bias_ref[case, 2 * p + hh]
            return s_nb, _dot_nt(k_ref[0, p, 0:CTX_LEN, :], qh)

        def attend(p, hh, s_nb, s_cx):
            return softmax_pv(hh, [vt_tiles(p, first_tile, NA_KEYS // LANES), vt_tiles(p, 0, ctx_tiles)],
                              [s_nb, s_cx])

        pipelined(scores_of, attend)


def _neighborhood_attention(q, k, vt, rpb_tiles, layer, with_ctx_queries):
    n_batch, n_pairs, t_all, _ = q.shape
    n_tiles = t_all // TOKEN_TILE
    tile_off = 0 if with_ctx_queries else CTX_LEN // TOKEN_TILE
    n_steps = n_tiles - tile_off
    t_out = n_steps * TOKEN_TILE
    return pl.pallas_call(
        functools.partial(_na_kernel, tile_off),
        grid=(n_batch, n_steps),
        in_specs=[
            pl.BlockSpec((1, n_pairs, TOKEN_TILE, LANES), lambda b, g: (b, 0, g + tile_off, 0)),
            pl.BlockSpec((1, n_pairs, t_all, LANES), lambda b, g: (b, 0, 0, 0)),
            pl.BlockSpec((1, n_pairs, t_all // LANES, LANES, LANES), lambda b, g: (b, 0, 0, 0, 0)),
            pl.BlockSpec((None,) + rpb_tiles.shape[1:], lambda b, g: (layer, 0, 0, 0, 0),
                         pipeline_mode=pl.Buffered(1)),
        ],
        out_specs=pl.BlockSpec((1, TOKEN_TILE, NA_WIDTH), lambda b, g: (b, g, 0)),
        out_shape=jax.ShapeDtypeStruct((n_batch, t_out, NA_WIDTH), BF16),
        scratch_shapes=[pltpu.VMEM((3, NA_HEADS, NA_KEYS, TOKEN_TILE), F32)],
        compiler_params=pltpu.CompilerParams(
            dimension_semantics=("arbitrary", "arbitrary"),
            vmem_limit_bytes=VMEM_LIMIT),
        name="neighborhood_attention",
    )(q, k, vt, rpb_tiles)


def _na_bias_tiles(rpb):
    col = np.arange(GRID_W)
    col_start = np.clip(col - NA_WIN_W // 2, 0, GRID_W - NA_WIN_W)
    col_in = (col[:, None] >= col_start[None, :]) & (col[:, None] < col_start[None, :] + NA_WIN_W)
    col_idx = np.clip(col[:, None] - col[None, :], 1 - NA_WIN_W, NA_WIN_W - 1) + NA_WIN_W - 1
    one_hot = (col_idx[None] == np.arange(2 * NA_WIN_W - 1)[:, None, None]).astype(np.float32)
    tiles = jnp.einsum("lhrw,wkj->lhrkj", rpb.astype(F32), jnp.asarray(one_hot),
                       precision=lax.Precision.HIGHEST)
    return jnp.where(col_in, tiles * LOG2_E, -jnp.inf)


def _scan_constants():
    idx = np.arange(SCAN_BLOCK)
    c = idx // CHUNK
    pos = idx % CHUNK
    same = c[:, None] == c[None, :]
    return {"cum_f": (same & (pos[None, :] <= pos[:, None])).astype(np.float32),
            "cum_b": (same & (pos[None, :] >= pos[:, None])).astype(np.float32)}


def _block_matrix(n_rows, n_cols, block_shape, dtype, block_at):
    zero = jnp.zeros(block_shape, dtype)
    rows = []
    for r in range(n_rows):
        blocks = [block_at(r, c) for c in range(n_cols)]
        rows.append(jnp.concatenate([zero if b is None else b for b in blocks], axis=1))
    return jnp.concatenate(rows, axis=0)


def _run_interleaved(stagewise):
    results = [None] * len(stagewise)
    live = list(range(len(stagewise)))
    while live:
        for i in list(live):
            try:
                next(stagewise[i])
            except StopIteration as done:
                results[i] = done.value
                live.remove(i)
    return results


def _scan_direction(load_inputs, cum, st_ref, reverse):
    q, k, v, lf = load_inputs()
    yield
    n = SCAN_BLOCK
    hd = q.shape[1]
    dk = hd // SCAN_HEADS
    pack = 2
    low_half = lax.broadcasted_iota(jnp.int32, (1, LANES), 1) < HEAD_DIM
    lane_k = lax.broadcasted_iota(jnp.int32, (1, hd), 1) // dk
    lane_v = lax.broadcasted_iota(jnp.int32, (1, SCAN_HEADS * HEAD_DIM), 1) // HEAD_DIM

    def lane_mask(cond):
        return jnp.where(cond, 1.0, 0.0).astype(BF16)

    half_masks = {True: lane_mask(low_half), False: lane_mask(jnp.logical_not(low_half))}
    k_masks = [lane_mask(lane_k == h) for h in range(SCAN_HEADS)]
    v_masks = [lane_mask(lane_v == h) for h in range(SCAN_HEADS)]

    def keep_half(x, low):
        return x * half_masks[low]

    hi = lf.astype(BF16)
    mid = (lf - hi.astype(F32)).astype(BF16)
    g = _dot(cum, hi) + _dot(cum, mid)
    yield
    ref_pos, last_pos = (CHUNK // 2, 0) if reverse else (CHUNK // 2 - 1, CHUNK - 1)
    to_ref, to_last, decay = [], [], []
    for c in range(CHUNKS_PER_BLOCK):
        gc = g[c * CHUNK:(c + 1) * CHUNK]
        g_last = gc[last_pos:last_pos + 1]
        to_ref.append(gc - gc[ref_pos:ref_pos + 1])
        to_last.append(g_last - gc)
        decay.append(jnp.exp(g_last))
    x_ref = jnp.concatenate(to_ref, axis=0)
    a = jnp.exp(x_ref)
    a_inv = jnp.exp(-x_ref)
    e_in = jnp.exp(g)
    e_out = jnp.exp(jnp.concatenate(to_last, axis=0))

    q_t = (q * a).astype(BF16)
    k_t_tr = jnp.transpose(k * a_inv).astype(BF16)
    q_in = (q * e_in).astype(BF16)
    k_out = (k * e_out).astype(BF16)
    v_b = v.astype(BF16)

    k_blk = _block_matrix(SCAN_HEADS, SCAN_HEADS, (dk, n), BF16,
                          lambda r, c: k_t_tr[r * dk:(r + 1) * dk] if r == c else None)
    scores = _dot(q_t, k_blk)
    yield

    v_tr = jnp.transpose(v).astype(BF16)
    token_pair = lax.broadcasted_iota(jnp.int32, (1, n), 1) // (CHUNK * pack)
    pair_masks = [lane_mask(token_pair == m) for m in range(CHUNKS_PER_BLOCK // pack)]
    heads_per_group = LANES // dk
    incs = []
    for grp in range(hd // LANES):
        heads = range(grp * heads_per_group, (grp + 1) * heads_per_group)
        v_blk = jnp.concatenate(
            [jnp.concatenate([v_tr[h * HEAD_DIM:(h + 1) * HEAD_DIM] * pm for h in heads], axis=1)
             for pm in pair_masks], axis=0)
        k_rows = []
        for h in heads:
            k_h = (k_out * k_masks[h])[:, grp * LANES:(grp + 1) * LANES]
            k_rows.append(_block_matrix(
                CHUNKS_PER_BLOCK, pack, (CHUNK, LANES), BF16,
                lambda c, par, k_h=k_h: k_h[c * CHUNK:(c + 1) * CHUNK] if c % pack == par else None))
        incs.append(_dot(v_blk, jnp.concatenate(k_rows, axis=0)))
    yield

    tri = jnp.concatenate([cum.astype(F32)] * SCAN_HEADS, axis=1) > 0.5
    p = jnp.where(tri, scores, 0.0).astype(BF16)
    v_stack = jnp.concatenate([v_b * v_masks[h] for h in range(SCAN_HEADS)], axis=0)
    o_intra = _dot(p, v_stack)
    yield

    st = st_ref[...]
    states = [None] * CHUNKS_PER_BLOCK
    order = range(CHUNKS_PER_BLOCK - 1, -1, -1) if reverse else range(CHUNKS_PER_BLOCK)
    for c in order:
        states[c] = st
        m, par = divmod(c, pack)
        inc_c = jnp.concatenate([inc[m * HEAD_DIM:(m + 1) * HEAD_DIM, par * LANES:(par + 1) * LANES]
                                 for inc in incs], axis=1)
        st = decay[c] * st + inc_c
    st_ref[...] = st

    q_blk = _block_matrix(CHUNKS_PER_BLOCK, CHUNKS_PER_BLOCK, (CHUNK, hd), BF16,
                          lambda r, c: q_in[r * CHUNK:(r + 1) * CHUNK] if r == c else None)
    s_parts = []
    for s in states:
        s_tr = jnp.transpose(jnp.concatenate([s, s], axis=0)).astype(BF16)

        def s_block(h, j, s_tr=s_tr):
            return keep_half(s_tr[h * dk:(h + 1) * dk], h % 2 == 0) if h // 2 == j else None

        s_parts.append(_block_matrix(SCAN_HEADS, SCAN_HEADS // 2, (dk, LANES), BF16, s_block))
    return o_intra + _dot(q_blk, jnp.concatenate(s_parts, axis=0))


def _rope(x, cos, sin_signed):
    lane = lax.broadcasted_iota(jnp.int32, (1, x.shape[1]), 1)
    first = (lane % 16) < 8
    partner = jnp.where(first, pltpu.roll(x, x.shape[1] - 8, 1), pltpu.roll(x, 8, 1))
    return x * cos + partner * sin_signed


def _scan_kernel(n_ctx_blocks,
                 in_f_ref, in_b_ref,
                 cos_f_ref, sin_f_ref, cos_b_ref, sin_b_ref,
                 wgk_ref, bgk_ref, lb_ref, cum_f_ref, cum_b_ref,
                 of_ref, ob_ref,
                 sgf_ref, sgb_ref, shf_ref, shb_ref):
    del n_ctx_blocks
    gla_cols, hg_cols = pl.ds(0, GLA_WIDTH), pl.ds(GLA_WIDTH, HGRN_WIDTH)

    @pl.when(pl.program_id(1) == 0)
    def _reset():
        for ref in (sgf_ref, sgb_ref, shf_ref, shb_ref):
            ref[...] = jnp.zeros_like(ref)

    def gla_inputs(in_ref, cos_ref, sin_ref, direction, rows):
        blk = in_ref[0, rows, 0:W_GLA]
        v = blk[:, 0:GLA_WIDTH]
        q = blk[:, GLA_WIDTH:GLA_WIDTH + GLA_KEY]
        k = blk[:, GLA_WIDTH + GLA_KEY:GLA_WIDTH + 2 * GLA_KEY]
        gk = blk[:, GLA_WIDTH + 2 * GLA_KEY:]
        cos = cos_ref[rows, :]
        sin = sin_ref[rows, :]
        q = _rope(q, cos, sin) * (GLA_KEY // SCAN_HEADS) ** -0.5
        k = _rope(k, cos, sin)
        lo = direction * GLA_KEY
        z = _dot(gk.astype(BF16), wgk_ref[:, lo:lo + GLA_KEY]) + bgk_ref[:, lo:lo + GLA_KEY]
        log_sig = jnp.minimum(z, 0.0) - jnp.log(1.0 + jnp.exp(-jnp.abs(z)))
        return q, k, v, log_sig / GLA_GATE_NORM

    def hgrn_inputs(in_ref, direction, rows):
        blk = in_ref[0, rows, W_GLA:W_GLA + W_HG]
        q = blk[:, 0:HGRN_WIDTH] * HEAD_DIM ** -0.5
        f_raw = blk[:, (1 + direction) * HGRN_WIDTH:(2 + direction) * HGRN_WIDTH]
        v = blk[:, 3 * HGRN_WIDTH:]
        lb = lb_ref[direction:direction + 1, :]
        f = lb + (1.0 - lb) * _sigmoid(f_raw)
        return q, 1.0 - f, v, jnp.log(f)

    jobs, dests = [], []
    for s in range(SCAN_SUB):
        rows_f = pl.ds(s * SCAN_BLOCK, SCAN_BLOCK)
        rows_b = pl.ds((SCAN_SUB - 1 - s) * SCAN_BLOCK, SCAN_BLOCK)
        jobs += [
            _scan_direction(functools.partial(gla_inputs, in_f_ref, cos_f_ref, sin_f_ref, 0, rows_f),
                            cum_f_ref[...], sgf_ref, reverse=False),
            _scan_direction(functools.partial(hgrn_inputs, in_f_ref, 0, rows_f),
                            cum_f_ref[...], shf_ref, reverse=False),
            _scan_direction(functools.partial(gla_inputs, in_b_ref, cos_b_ref, sin_b_ref, 1, rows_b),
                            cum_b_ref[...], sgb_ref, reverse=True),
            _scan_direction(functools.partial(hgrn_inputs, in_b_ref, 1, rows_b),
                            cum_b_ref[...], shb_ref, reverse=True),
        ]
        dests += [(of_ref, rows_f, gla_cols), (of_ref, rows_f, hg_cols),
                  (ob_ref, rows_b, gla_cols), (ob_ref, rows_b, hg_cols)]
    for (ref, rows, cols), val in zip(dests, _run_interleaved(jobs)):
        ref[0, rows, cols] = val.astype(ref.dtype)


def _scans(scan_in, cos_t, sin_t, wgk, bgk, lb, consts):
    n_batch, t_all, _ = scan_in.shape
    step_rows = SCAN_SUB * SCAN_BLOCK
    assert CTX_LEN % step_rows == 0 and t_all % step_rows == 0
    n_blocks = t_all // step_rows
    n_ctx_blocks = CTX_LEN // step_rows

    def fwd(i):
        return i

    def bwd(i):
        return jnp.where(i < n_ctx_blocks, n_ctx_blocks - 1 - i, n_blocks - 1 + n_ctx_blocks - i)

    def tok_spec(width, order):
        return pl.BlockSpec((1, step_rows, width), lambda b, i: (b, order(i), 0))

    def tab_spec(order):
        return pl.BlockSpec((step_rows, LANES), lambda b, i: (order(i), 0))

    def const_spec(arr):
        return pl.BlockSpec(arr.shape, lambda b, i: (0,) * arr.ndim)

    const_arrays = [wgk, bgk, lb, consts["cum_f"], consts["cum_b"]]
    out_shape = jax.ShapeDtypeStruct((n_batch, t_all, GLA_WIDTH + HGRN_WIDTH), BF16)
    return pl.pallas_call(
        functools.partial(_scan_kernel, n_ctx_blocks),
        grid=(n_batch, n_blocks),
        in_specs=[tok_spec(W_GLA + W_HG, fwd), tok_spec(W_GLA + W_HG, bwd),
                  tab_spec(fwd), tab_spec(fwd), tab_spec(bwd), tab_spec(bwd)]
                 + [const_spec(a) for a in const_arrays],
        out_specs=[tok_spec(GLA_WIDTH + HGRN_WIDTH, fwd), tok_spec(GLA_WIDTH + HGRN_WIDTH, bwd)],
        out_shape=[out_shape] * 2,
        scratch_shapes=[pltpu.VMEM((HEAD_DIM, GLA_KEY), F32), pltpu.VMEM((HEAD_DIM, GLA_KEY), F32),
                        pltpu.VMEM((HEAD_DIM, HGRN_WIDTH), F32), pltpu.VMEM((HEAD_DIM, HGRN_WIDTH), F32)],
        compiler_params=pltpu.CompilerParams(
            dimension_semantics=("parallel", "arbitrary"), vmem_limit_bytes=VMEM_LIMIT),
        name="bidirectional_scans",
    )(scan_in, scan_in, cos_t, sin_t, cos_t, sin_t, *const_arrays)


def _rope_tables(t_lat):
    half = GLA_KEY // SCAN_HEADS // 2
    inv_freq = 1.0 / (ROPE_BASE ** (np.arange(0, half, 2, dtype=np.float32) / half))
    t = np.arange(t_lat)
    lane = np.arange(LANES) % (2 * half)
    pos = np.where(lane[None, :] < half, (t // GRID_W)[:, None], (t % GRID_W)[:, None]).astype(np.float32)
    ang = (pos * inv_freq[(lane % half) % (half // 2)][None, :]).astype(np.float32)
    first = ((lane % half) < half // 2)[None, :]
    cos = np.cos(ang)
    sin = np.where(first, -np.sin(ang), np.sin(ang))
    cos = np.concatenate([np.ones((CTX_LEN, LANES), np.float32), cos], axis=0)
    sin = np.concatenate([np.zeros((CTX_LEN, LANES), np.float32), sin], axis=0)
    return jnp.asarray(cos, F32), jnp.asarray(sin, F32)


N_MIX_REFS = 9


def _mix_tile(x, row, ona_ref, of_ref, ob_ref, gate_ref, mod_ref, gnw_ref, hnw_ref, bd_ref, w_ref):
    gt = mod_ref[pl.ds(row, 1), :][:, 2 * D_MODEL:]
    bd = bd_ref[...]

    def head_norm(o, w):
        sq = o * o
        hi = sq.astype(BF16)
        lo = (sq - hi.astype(F32)).astype(BF16)
        ms = _dot(hi, bd) + _dot(lo, bd)
        return o * lax.rsqrt(ms + NORM_EPS) * w

    o_scan = of_ref[0].astype(F32) + ob_ref[0].astype(F32)
    o_gla = head_norm(o_scan[:, :GLA_WIDTH], gnw_ref[...])
    o_hg = head_norm(o_scan[:, GLA_WIDTH:], hnw_ref[...])
    o_cat = jnp.concatenate([ona_ref[0].astype(F32), o_gla, o_hg], axis=1)
    act = o_cat * _silu(gate_ref[0].astype(F32))
    y = _dot(act.astype(BF16), w_ref[...])
    return x + gt * y


def _final_out_kernel(x_ref, *refs):
    xn = _mix_tile(x_ref[0], pl.program_id(0), *refs[:N_MIX_REFS])
    fw_ref, out_ref = refs[N_MIX_REFS:]
    ms = jnp.mean(xn * xn, axis=-1, keepdims=True)
    out_ref[0] = xn * lax.rsqrt(ms + NORM_EPS) * fw_ref[...]


def _out_in_kernel(n_streams, n_batch, *refs):
    x, row = _token_tile(refs[:n_streams], n_batch)
    xn = _mix_tile(x, row, *refs[n_streams:n_streams + N_MIX_REFS])
    rest = refs[n_streams + N_MIX_REFS:]
    (mod_ref, nw_ref, w_ref), stream_ref, proj_out, wb_ref = rest[:3], rest[3], rest[4:-1], rest[-1]
    _stage_w_in(w_ref, wb_ref)
    stream_ref[0] = xn
    _project_tile(xn, row, mod_ref, nw_ref, wb_ref, *proj_out)


def _token_spec(width, tile, first_token):
    if first_token % tile == 0:
        return pl.BlockSpec((1, tile, width), lambda b, t: (b, t + first_token // tile, 0))
    return pl.BlockSpec((pl.Element(1), pl.Element(tile), pl.Element(width)),
                        lambda b, t: (b, pl.multiple_of(first_token + t * tile, TOKEN_TILE), 0))


def _mix_io(o_na, scans, gates, mod_l, gla_nw, hg_nw, bd, w_out_b, layer, tile=TOKEN_TILE, first_token=0):
    def tok(width):
        return _token_spec(width, tile, first_token)

    def const_spec(arr):
        return pl.BlockSpec(arr.shape, lambda b, t: (0,) * arr.ndim)

    consts = [mod_l, gla_nw, hg_nw, bd]
    specs = ([pl.BlockSpec((1, tile, NA_WIDTH), lambda b, t: (b, t, 0))]
             + [tok(GLA_WIDTH + HGRN_WIDTH)] * 2
             + [tok(W_GATE)] + [const_spec(a) for a in consts]
             + [pl.BlockSpec((None,) + w_out_b.shape[1:], lambda b, t: (layer, 0, 0))])
    return specs, [o_na, *scans, gates, *consts, w_out_b]


def _final_out(streams, mix_args, final_w, n_batch, t_all):
    (stream,) = streams
    t_lat = t_all - CTX_LEN
    assert t_lat % FINAL_TILE == 0
    mix_specs, mix_arrays = _mix_io(*mix_args, tile=FINAL_TILE, first_token=CTX_LEN)
    return pl.pallas_call(
        _final_out_kernel,
        grid=(n_batch, t_lat // FINAL_TILE),
        in_specs=[_token_spec(D_MODEL, FINAL_TILE, CTX_LEN)] + mix_specs
                 + [pl.BlockSpec((1, D_MODEL), lambda b, t: (0, 0))],
        out_specs=pl.BlockSpec((1, FINAL_TILE, D_MODEL), lambda b, t: (b, t, 0)),
        out_shape=jax.ShapeDtypeStruct((n_batch, t_lat, D_MODEL), F32),
        compiler_params=pltpu.CompilerParams(
            dimension_semantics=("parallel", "parallel"), vmem_limit_bytes=VMEM_LIMIT),
        name="out_proj_final",
    )(stream, *mix_arrays, final_w)


def _out_in_proj(streams, mix_args, mod_next, norm_w_next, w_in_p, next_layer, n_batch, t_all):
    mix_specs, mix_arrays = _mix_io(*mix_args)
    proj_specs, out_specs, out_shape = _projection_io(mod_next, next_layer, n_batch, t_all)
    stream_spec = pl.BlockSpec((1, TOKEN_TILE, D_MODEL), lambda b, t: (b, t, 0))
    stream_shape = jax.ShapeDtypeStruct((n_batch, t_all, D_MODEL), F32)
    outs = pl.pallas_call(
        functools.partial(_out_in_kernel, len(streams), n_batch),
        grid=(n_batch, t_all // TOKEN_TILE),
        in_specs=_stream_specs(len(streams)) + mix_specs + proj_specs,
        out_specs=[stream_spec] + out_specs,
        out_shape=[stream_shape] + out_shape,
        scratch_shapes=[pltpu.VMEM((D_MODEL, W_IN_PAD), BF16)],
        compiler_params=pltpu.CompilerParams(
            dimension_semantics=("arbitrary", "arbitrary"), vmem_limit_bytes=VMEM_LIMIT),
        name="out_in_proj",
    )(*streams, *mix_arrays, mod_next, norm_w_next, w_in_p)
    return outs[0], outs[1:]


def _pad_w_in(w):
    assert w.shape[-1] == IN_OFFS[-1]
    return jnp.pad(w.astype(BF16), [(0, 0)] * (w.ndim - 1) + [(0, W_IN_PAD - w.shape[-1])])


def kernel(x, c, ctx, c_ctx, ada_w, ada_b, norm_w, w_in, na_rpb, gla_w_gk, gla_b_gk, gla_norm_w,
           hgrn_lower_bounds, hgrn_norm_w, w_out, final_norm_w):
    n_batch, t_lat, _ = x.shape
    depth = ada_w.shape[0]
    t_all = CTX_LEN + t_lat
    assert ctx.shape[1] == CTX_LEN and t_lat % TOKEN_TILE == 0 and t_lat // GRID_W >= NA_KEY_ROWS

    lbs = jnp.cumsum(jax.nn.softmax(hgrn_lower_bounds.astype(F32), axis=0), axis=0)
    lbs = lbs - lbs[0:1]
    mod_rows = -(-(n_batch + 1) // 8) * 8
    cc = jnp.zeros((mod_rows, D_MODEL), F32).at[:n_batch].set(c).at[n_batch].set(c_ctx)
    mod = _modulation(cc, ada_w, ada_b)

    consts = {k_: jnp.asarray(v_, BF16) for k_, v_ in _scan_constants().items()}
    cos_t, sin_t = _rope_tables(t_lat)
    head_of = np.arange(GLA_WIDTH) // HEAD_DIM
    bd = jnp.asarray((head_of[:, None] == head_of[None, :]).astype(np.float32) / HEAD_DIM, BF16)

    w_in_p = w_in.astype(F32)
    w_out_b = w_out.astype(BF16)
    rpb_tiles = _na_bias_tiles(na_rpb)
    wgk = jnp.zeros((depth, LANES, 2 * GLA_KEY), F32)
    wgk = wgk.at[:, 0:GLA_GK_RANK, 0:GLA_KEY].set(gla_w_gk[:, 0])
    wgk = wgk.at[:, GLA_GK_RANK:2 * GLA_GK_RANK, GLA_KEY:].set(gla_w_gk[:, 1]).astype(BF16)

    streams = [ctx, x]
    projections = _in_proj(streams, mod[0], norm_w[0].reshape(1, D_MODEL), w_in_p, 0, n_batch, t_all)
    for layer in range(depth):
        last = layer == depth - 1
        bgk = gla_b_gk[layer].reshape(1, 2 * GLA_KEY)
        gla_nw = jnp.tile(gla_norm_w[layer], SCAN_HEADS).reshape(1, GLA_WIDTH)
        hg_nw = jnp.tile(hgrn_norm_w[layer], SCAN_HEADS).reshape(1, HGRN_WIDTH)

        q, k, vt, gates, scan_in = projections
        o_na = _neighborhood_attention(q, k, vt, rpb_tiles, layer, with_ctx_queries=not last)
        scans = _scans(scan_in, cos_t, sin_t, wgk[layer], bgk, lbs[layer], consts)
        mix_args = (o_na, scans, gates, mod[layer], gla_nw, hg_nw, bd, w_out_b, layer)
        if last:
            return _final_out(streams, mix_args, final_norm_w.reshape(1, D_MODEL), n_batch, t_all)
        stream, projections = _out_in_proj(streams, mix_args, mod[layer + 1],
                                           norm_w[layer + 1].reshape(1, D_MODEL), w_in_p, layer + 1,
                                           n_batch, t_all)
        streams = [stream]
```

```python
import functools

import numpy as np
import jax
import jax.numpy as jnp
from jax import lax
from jax.experimental import pallas as pl
from jax.experimental.pallas import tpu as pltpu

F32 = jnp.float32
BF16 = jnp.bfloat16

D_MODEL = 1024
CTX_LEN = 256
GRID_W = 64
NORM_EPS = 1e-6
HEAD_DIM = 64
NA_WIDTH = 512
NA_HEADS = 8
NA_WIN_H = 8
NA_WIN_W = 16
GLA_WIDTH = 256
GLA_KEY = 128
GLA_GK_RANK = 16
GLA_GATE_NORM = 16.0
HGRN_WIDTH = 256
SCAN_HEADS = 4
CHUNK = 16
ROPE_BASE = 10000.0

LOG2_E = float(np.log2(np.e))
NA_Q_SCALE = HEAD_DIM ** -0.5 * LOG2_E

LANES = 128
TOKEN_TILE = 256
NA_PIPELINE_LAG = 4
FINAL_TILE = 1024
SCAN_BLOCK = 128
SCAN_SUB = 2
CHUNKS_PER_BLOCK = SCAN_BLOCK // CHUNK
NA_Q_ROWS = TOKEN_TILE // GRID_W
NA_KEY_ROWS = NA_Q_ROWS + NA_WIN_H
NA_KEYS = NA_KEY_ROWS * GRID_W
assert NA_KEYS % LANES == 0 and (NA_Q_ROWS * GRID_W) % LANES == 0 and NA_WIN_H % 4 == 0
VMEM_LIMIT = 56 * 1024 * 1024

IN_NAMES = ("na_q", "na_k", "na_v", "na_g", "gla_q", "gla_k", "gla_v", "gla_gk", "gla_g",
            "hg_q", "hg_ff", "hg_fb", "hg_i", "hg_g")
IN_SPLITS = (512, 512, 512, 512, 128, 128, 256, 32, 256, 256, 256, 256, 256, 256)
IN_OFFS = tuple(int(v) for v in np.cumsum((0,) + IN_SPLITS))
W_GATE = NA_WIDTH + GLA_WIDTH + HGRN_WIDTH
W_GLA = GLA_WIDTH + 2 * GLA_KEY + LANES
W_HG = 4 * HGRN_WIDTH
W_IN_PAD = -(-IN_OFFS[-1] // LANES) * LANES


def _dot(a, b):
    return jnp.dot(a, b, preferred_element_type=F32)


def _dot_nt(a, b):
    return lax.dot_general(a, b, (((1,), (1,)), ((), ())), preferred_element_type=F32)


def _sigmoid(x):
    return 1.0 / (1.0 + jnp.exp(-x))


def _silu(x):
    return x * _sigmoid(x)


def _mod_kernel(c_ref, w_ref, b_ref, o_ref):
    s = _silu(c_ref[...]).astype(BF16)
    o_ref[0] = _dot(s, w_ref[0].astype(BF16)) + b_ref[0]


def _modulation(cc, ada_w, ada_b):
    depth = ada_w.shape[0]
    rows = cc.shape[0]
    return pl.pallas_call(
        _mod_kernel,
        grid=(depth, 3),
        in_specs=[
            pl.BlockSpec((rows, D_MODEL), lambda l, j: (0, 0)),
            pl.BlockSpec((1, D_MODEL, D_MODEL), lambda l, j: (l, 0, j)),
            pl.BlockSpec((1, 1, D_MODEL), lambda l, j: (l, 0, j)),
        ],
        out_specs=pl.BlockSpec((1, rows, D_MODEL), lambda l, j: (l, 0, j)),
        out_shape=jax.ShapeDtypeStruct((depth, rows, 3 * D_MODEL), F32),
        compiler_params=pltpu.CompilerParams(
            dimension_semantics=("parallel", "parallel"), vmem_limit_bytes=VMEM_LIMIT),
        name="adaln_mod",
    )(cc, ada_w, ada_b.reshape(depth, 1, 3 * D_MODEL))


def _token_tile(x_refs, n_batch, tile_off=0):
    t = pl.program_id(1) + tile_off
    is_ctx = t < CTX_LEN // TOKEN_TILE
    x = jnp.where(is_ctx, x_refs[0][0], x_refs[1][0]) if len(x_refs) == 2 else x_refs[0][0]
    return x, jnp.where(is_ctx, n_batch, pl.program_id(0))


def _stream_specs(n_streams, tile_off=0):
    n_ctx_tiles = CTX_LEN // TOKEN_TILE
    if n_streams == 2:
        return [pl.BlockSpec((1, TOKEN_TILE, D_MODEL), lambda b, t: (b, jnp.minimum(t, n_ctx_tiles - 1), 0)),
                pl.BlockSpec((1, TOKEN_TILE, D_MODEL), lambda b, t: (b, jnp.maximum(t - n_ctx_tiles, 0), 0))]
    return [pl.BlockSpec((1, TOKEN_TILE, D_MODEL), lambda b, t: (b, t + tile_off, 0))]


def _in_proj_kernel(n_streams, n_batch, *refs):
    x, row = _token_tile(refs[:n_streams], n_batch)
    _project_tile(x, row, *refs[n_streams:])


def _project_tile(x, row, mod_ref, nw_ref, w_ref, q_ref, k_ref, v_ref, gate_ref, scan_ref):
    m = mod_ref[pl.ds(row, 1), :]
    sh = m[:, :D_MODEL]
    sc = m[:, D_MODEL:2 * D_MODEL]
    ms = jnp.mean(x * x, axis=-1, keepdims=True)
    y = x * lax.rsqrt(ms + NORM_EPS) * nw_ref[...]
    h = y * (1.0 + sc) + sh
    p = _dot(h.astype(BF16), w_ref[...])
    for pr in range(NA_HEADS // 2):
        lo = pr * LANES
        q_ref[0, pr] = (p[:, lo:lo + LANES] * NA_Q_SCALE).astype(BF16)
        k_ref[0, pr] = p[:, NA_WIDTH + lo:NA_WIDTH + lo + LANES].astype(BF16)
        for s in range(TOKEN_TILE // LANES):
            v_tile = p[s * LANES:(s + 1) * LANES, 2 * NA_WIDTH + lo:2 * NA_WIDTH + lo + LANES]
            v_ref[0, pr, s] = jnp.transpose(v_tile).astype(BF16)
    off = dict(zip(IN_NAMES, IN_OFFS))
    tail = p[:, off["gla_gk"]:]
    aligned = pltpu.roll(tail, tail.shape[1] - IN_SPLITS[IN_NAMES.index("gla_gk")], 1)
    seg = lambda name: aligned[:, off[name] - off["gla_g"]:off[name] - off["gla_g"] + IN_SPLITS[IN_NAMES.index(name)]]
    gate_ref[0] = jnp.concatenate([p[:, off["na_g"]:off["gla_q"]], seg("gla_g"), seg("hg_g")],
                                  axis=1).astype(BF16)
    scan_ref[0] = jnp.concatenate(
        [p[:, off["gla_v"]:off["gla_gk"]], p[:, off["gla_q"]:off["gla_v"]], tail[:, :LANES],
         aligned[:, off["hg_q"] - off["gla_g"]:off["hg_g"] - off["gla_g"]]], axis=1)


def _projection_io(mod_l, layer, n_batch, t_all):
    in_specs = [
        pl.BlockSpec(mod_l.shape, lambda b, t: (0, 0)),
        pl.BlockSpec((1, D_MODEL), lambda b, t: (0, 0)),
        pl.BlockSpec((None, D_MODEL, W_IN_PAD), lambda b, t: (layer, 0, 0), pipeline_mode=pl.Buffered(1)),
    ]
    pair_spec = pl.BlockSpec((1, NA_HEADS // 2, TOKEN_TILE, LANES), lambda b, t: (b, 0, t, 0))
    pair_shape = jax.ShapeDtypeStruct((n_batch, NA_HEADS // 2, t_all, LANES), BF16)
    vt_spec = pl.BlockSpec((1, NA_HEADS // 2, TOKEN_TILE // LANES, LANES, LANES), lambda b, t: (b, 0, t, 0, 0))
    vt_shape = jax.ShapeDtypeStruct((n_batch, NA_HEADS // 2, t_all // LANES, LANES, LANES), BF16)

    def slab(width, dtype):
        return (pl.BlockSpec((1, TOKEN_TILE, width), lambda b, t: (b, t, 0)),
                jax.ShapeDtypeStruct((n_batch, t_all, width), dtype))

    gate_spec, gate_shape = slab(W_GATE, BF16)
    scan_spec, scan_shape = slab(W_GLA + W_HG, F32)
    return (in_specs, [pair_spec, pair_spec, vt_spec, gate_spec, scan_spec],
            [pair_shape, pair_shape, vt_shape, gate_shape, scan_shape])


def _in_proj(streams, mod_l, norm_w_l, w_in_p, layer, n_batch, t_all):
    in_specs, out_specs, out_shape = _projection_io(mod_l, layer, n_batch, t_all)
    return pl.pallas_call(
        functools.partial(_in_proj_kernel, len(streams), n_batch),
        grid=(n_batch, t_all // TOKEN_TILE),
        in_specs=_stream_specs(len(streams)) + in_specs,
        out_specs=out_specs,
        out_shape=out_shape,
        compiler_params=pltpu.CompilerParams(
            dimension_semantics=("parallel", "parallel"), vmem_limit_bytes=VMEM_LIMIT),
        name="in_proj",
    )(*streams, mod_l, norm_w_l, w_in_p)


def _na_bias_cases():
    cases = []
    for key_row_ok, rel in (
        (lambda lr, lk: lk < NA_WIN_H, lambda lr, lk: lk - lr + NA_WIN_H - 1),
        (lambda lr, lk: lr <= lk < lr + NA_WIN_H, lambda lr, lk: lk - lr + NA_WIN_H // 2 - 1),
        (lambda lr, lk: lk >= NA_KEY_ROWS - NA_WIN_H, lambda lr, lk: lk - lr - 1),
    ):
        cases.append([[rel(lr, lk) if key_row_ok(lr, lk) else None for lk in range(NA_KEY_ROWS)]
                      for lr in range(NA_Q_ROWS)])
    return cases


def _na_kernel(tile_off, q_ref, k_ref, vt_ref, rt_ref, o_ref, bias_ref):
    tb = pl.program_id(1) + tile_off
    n_pairs = q_ref.shape[1]
    lane = lax.broadcasted_iota(jnp.int32, (1, LANES), 1)
    q_scale = [jnp.where((lane // HEAD_DIM) == hh, 1.0, 0.0).astype(BF16) for hh in range(2)]
    heads = [(p, hh) for p in range(n_pairs) for hh in range(2)]
    ctx_tiles = CTX_LEN // LANES

    @pl.when((pl.program_id(0) == 0) & (pl.program_id(1) == 0))
    def _build_bias_tables():
        minus_inf = jnp.full((GRID_W, GRID_W), -jnp.inf, F32)

        def per_head(h, carry):
            for case, rel in enumerate(_na_bias_cases()):
                for lk in range(NA_KEY_ROWS):
                    for lr in range(0, NA_Q_ROWS, 2):
                        tiles = [minus_inf if rel[r][lk] is None else rt_ref[h, rel[r][lk]] for r in (lr, lr + 1)]
                        bias_ref[case, h, lk * GRID_W:(lk + 1) * GRID_W, lr * GRID_W:(lr + 2) * GRID_W] = (
                            jnp.concatenate(tiles, axis=1))
            return carry

        lax.fori_loop(0, NA_HEADS, per_head, 0)

    def vt_tiles(p, first, count):
        tiles = vt_ref[0, p, pl.ds(first, count)]
        return jnp.concatenate([tiles[i] for i in range(count)], axis=1)

    def finish(outs):
        for p in range(n_pairs):
            o_t = jnp.concatenate([outs[(p, 0)], outs[(p, 1)]], axis=0)
            o_ref[0, :, p * LANES:(p + 1) * LANES] = jnp.transpose(o_t).astype(o_ref.dtype)

    def pipelined(scores_of, attend):
        outs, issued = {}, {}
        for i in range(len(heads) + NA_PIPELINE_LAG):
            if i < len(heads):
                issued[i] = scores_of(*heads[i])
            j = i - NA_PIPELINE_LAG
            if j >= 0:
                outs[heads[j]] = attend(*heads[j], *issued.pop(j))
        finish(outs)

    def softmax_pv(hh, vt_parts, s_parts):
        m = functools.reduce(jnp.maximum, [jnp.max(s, axis=0, keepdims=True) for s in s_parts])
        o_t = None
        for vt, s in zip(vt_parts, s_parts):
            ones = jnp.ones((HEAD_DIM, vt.shape[1]), BF16)
            lhs = jnp.concatenate([vt[:HEAD_DIM], ones] if hh == 0 else [ones, vt[HEAD_DIM:]], axis=0)
            part = _dot(lhs, jnp.exp2(s - m).astype(BF16))
            o_t = part if o_t is None else o_t + part
        own, other = (0, HEAD_DIM) if hh == 0 else (HEAD_DIM, 0)
        return o_t[own:own + HEAD_DIM] / o_t[other:other + 1]

    @pl.when(tb == 0)
    def _ctx_queries():
        def scores_of(p, hh):
            return (_dot_nt(k_ref[0, p, 0:CTX_LEN, :], q_ref[0, p] * q_scale[hh]),)

        def attend(p, hh, s):
            return softmax_pv(hh, [vt_tiles(p, 0, ctx_tiles)], [s])

        pipelined(scores_of, attend)

    @pl.when(tb > 0)
    def _latent_queries():
        g = tb - 1
        n_groups = pl.num_programs(1) - 1 + tile_off
        ws = jnp.clip(NA_Q_ROWS * g - NA_WIN_H // 2, 0, GRID_W - NA_KEY_ROWS)
        case = (g > 0).astype(jnp.int32) + (g == n_groups - 1).astype(jnp.int32)
        start = pl.multiple_of(CTX_LEN + ws * GRID_W, LANES)
        first_tile = ctx_tiles + ws * GRID_W // LANES

        def scores_of(p, hh):
            qh = q_ref[0, p] * q_scale[hh]
            s_nb = _dot_nt(k_ref[0, p, pl.ds(start, NA_KEYS), :], qh) + bias_ref[case, 2 * p + hh]
            return s_nb, _dot_nt(k_ref[0, p, 0:CTX_LEN, :], qh)

        def attend(p, hh, s_nb, s_cx):
            return softmax_pv(hh, [vt_tiles(p, first_tile, NA_KEYS // LANES), vt_tiles(p, 0, ctx_tiles)],
                              [s_nb, s_cx])

        pipelined(scores_of, attend)


def _neighborhood_attention(q, k, vt, rpb_tiles, layer, with_ctx_queries):
    n_batch, n_pairs, t_all, _ = q.shape
    n_tiles = t_all // TOKEN_TILE
    tile_off = 0 if with_ctx_queries else CTX_LEN // TOKEN_TILE
    n_steps = n_tiles - tile_off
    t_out = n_steps * TOKEN_TILE
    return pl.pallas_call(
        functools.partial(_na_kernel, tile_off),
        grid=(n_batch, n_steps),
        in_specs=[
            pl.BlockSpec((1, n_pairs, TOKEN_TILE, LANES), lambda b, g: (b, 0, g + tile_off, 0)),
            pl.BlockSpec((1, n_pairs, t_all, LANES), lambda b, g: (b, 0, 0, 0)),
            pl.BlockSpec((1, n_pairs, t_all // LANES, LANES, LANES), lambda b, g: (b, 0, 0, 0, 0)),
            pl.BlockSpec((None,) + rpb_tiles.shape[1:], lambda b, g: (layer, 0, 0, 0, 0),
                         pipeline_mode=pl.Buffered(1)),
        ],
        out_specs=pl.BlockSpec((1, TOKEN_TILE, NA_WIDTH), lambda b, g: (b, g, 0)),
        out_shape=jax.ShapeDtypeStruct((n_batch, t_out, NA_WIDTH), BF16),
        scratch_shapes=[pltpu.VMEM((3, NA_HEADS, NA_KEYS, TOKEN_TILE), F32)],
        compiler_params=pltpu.CompilerParams(
            dimension_semantics=("arbitrary", "arbitrary"),
            vmem_limit_bytes=VMEM_LIMIT),
        name="neighborhood_attention",
    )(q, k, vt, rpb_tiles)


def _na_bias_tiles(rpb):
    col = np.arange(GRID_W)
    col_start = np.clip(col - NA_WIN_W // 2, 0, GRID_W - NA_WIN_W)
    col_in = (col[:, None] >= col_start[None, :]) & (col[:, None] < col_start[None, :] + NA_WIN_W)
    col_idx = np.clip(col[:, None] - col[None, :], 1 - NA_WIN_W, NA_WIN_W - 1) + NA_WIN_W - 1
    one_hot = (col_idx[None] == np.arange(2 * NA_WIN_W - 1)[:, None, None]).astype(np.float32)
    tiles = jnp.einsum("lhrw,wkj->lhrkj", rpb.astype(F32), jnp.asarray(one_hot),
                       precision=lax.Precision.HIGHEST)
    return jnp.where(col_in, tiles * LOG2_E, -jnp.inf)


def _scan_constants():
    idx = np.arange(SCAN_BLOCK)
    c = idx // CHUNK
    pos = idx % CHUNK
    same = c[:, None] == c[None, :]
    return {"cum_f": (same & (pos[None, :] <= pos[:, None])).astype(np.float32),
            "cum_b": (same & (pos[None, :] >= pos[:, None])).astype(np.float32)}


def _block_matrix(n_rows, n_cols, block_shape, dtype, block_at):
    zero = jnp.zeros(block_shape, dtype)
    rows = []
    for r in range(n_rows):
        blocks = [block_at(r, c) for c in range(n_cols)]
        rows.append(jnp.concatenate([zero if b is None else b for b in blocks], axis=1))
    return jnp.concatenate(rows, axis=0)


def _run_interleaved(stagewise):
    results = [None] * len(stagewise)
    live = list(range(len(stagewise)))
    while live:
        for i in list(live):
            try:
                next(stagewise[i])
            except StopIteration as done:
                results[i] = done.value
                live.remove(i)
    return results


def _scan_direction(load_inputs, cum, st_ref, reverse):
    q, k, v, lf = load_inputs()
    yield
    n = SCAN_BLOCK
    hd = q.shape[1]
    dk = hd // SCAN_HEADS
    pack = 2
    low_half = lax.broadcasted_iota(jnp.int32, (1, LANES), 1) < HEAD_DIM
    lane_k = lax.broadcasted_iota(jnp.int32, (1, hd), 1) // dk
    lane_v = lax.broadcasted_iota(jnp.int32, (1, SCAN_HEADS * HEAD_DIM), 1) // HEAD_DIM

    def lane_mask(cond):
        return jnp.where(cond, 1.0, 0.0).astype(BF16)

    half_masks = {True: lane_mask(low_half), False: lane_mask(jnp.logical_not(low_half))}
    k_masks = [lane_mask(lane_k == h) for h in range(SCAN_HEADS)]
    v_masks = [lane_mask(lane_v == h) for h in range(SCAN_HEADS)]

    def keep_half(x, low):
        return x * half_masks[low]

    hi = lf.astype(BF16)
    mid = (lf - hi.astype(F32)).astype(BF16)
    g = _dot(cum, hi) + _dot(cum, mid)
    yield
    ref_pos, last_pos = (CHUNK // 2, 0) if reverse else (CHUNK // 2 - 1, CHUNK - 1)
    to_ref, to_last, decay = [], [], []
    for c in range(CHUNKS_PER_BLOCK):
        gc = g[c * CHUNK:(c + 1) * CHUNK]
        g_last = gc[last_pos:last_pos + 1]
        to_ref.append(gc - gc[ref_pos:ref_pos + 1])
        to_last.append(g_last - gc)
        decay.append(jnp.exp(g_last))
    x_ref = jnp.concatenate(to_ref, axis=0)
    a = jnp.exp(x_ref)
    a_inv = jnp.exp(-x_ref)
    e_in = jnp.exp(g)
    e_out = jnp.exp(jnp.concatenate(to_last, axis=0))

    q_t = (q * a).astype(BF16)
    k_t_tr = jnp.transpose(k * a_inv).astype(BF16)
    q_in = (q * e_in).astype(BF16)
    k_out = (k * e_out).astype(BF16)
    v_b = v.astype(BF16)

    k_blk = _block_matrix(SCAN_HEADS, SCAN_HEADS, (dk, n), BF16,
                          lambda r, c: k_t_tr[r * dk:(r + 1) * dk] if r == c else None)
    scores = _dot(q_t, k_blk)
    yield

    v_tr = jnp.transpose(v).astype(BF16)
    token_pair = lax.broadcasted_iota(jnp.int32, (1, n), 1) // (CHUNK * pack)
    pair_masks = [lane_mask(token_pair == m) for m in range(CHUNKS_PER_BLOCK // pack)]
    heads_per_group = LANES // dk
    incs = []
    for grp in range(hd // LANES):
        heads = range(grp * heads_per_group, (grp + 1) * heads_per_group)
        v_blk = jnp.concatenate(
            [jnp.concatenate([v_tr[h * HEAD_DIM:(h + 1) * HEAD_DIM] * pm for h in heads], axis=1)
             for pm in pair_masks], axis=0)
        k_rows = []
        for h in heads:
            k_h = (k_out * k_masks[h])[:, grp * LANES:(grp + 1) * LANES]
            k_rows.append(_block_matrix(
                CHUNKS_PER_BLOCK, pack, (CHUNK, LANES), BF16,
                lambda c, par, k_h=k_h: k_h[c * CHUNK:(c + 1) * CHUNK] if c % pack == par else None))
        incs.append(_dot(v_blk, jnp.concatenate(k_rows, axis=0)))
    yield

    tri = jnp.concatenate([cum.astype(F32)] * SCAN_HEADS, axis=1) > 0.5
    p = jnp.where(tri, scores, 0.0).astype(BF16)
    v_stack = jnp.concatenate([v_b * v_masks[h] for h in range(SCAN_HEADS)], axis=0)
    o_intra = _dot(p, v_stack)
    yield

    st = st_ref[...]
    states = [None] * CHUNKS_PER_BLOCK
    order = range(CHUNKS_PER_BLOCK - 1, -1, -1) if reverse else range(CHUNKS_PER_BLOCK)
    for c in order:
        states[c] = st
        m, par = divmod(c, pack)
        inc_c = jnp.concatenate([inc[m * HEAD_DIM:(m + 1) * HEAD_DIM, par * LANES:(par + 1) * LANES]
                                 for inc in incs], axis=1)
        st = decay[c] * st + inc_c
    st_ref[...] = st

    q_blk = _block_matrix(CHUNKS_PER_BLOCK, CHUNKS_PER_BLOCK, (CHUNK, hd), BF16,
                          lambda r, c: q_in[r * CHUNK:(r + 1) * CHUNK] if r == c else None)
    s_parts = []
    for s in states:
        s_tr = jnp.transpose(jnp.concatenate([s, s], axis=0)).astype(BF16)

        def s_block(h, j, s_tr=s_tr):
            return keep_half(s_tr[h * dk:(h + 1) * dk], h % 2 == 0) if h // 2 == j else None

        s_parts.append(_block_matrix(SCAN_HEADS, SCAN_HEADS // 2, (dk, LANES), BF16, s_block))
    return o_intra + _dot(q_blk, jnp.concatenate(s_parts, axis=0))


def _rope(x, cos, sin_signed):
    lane = lax.broadcasted_iota(jnp.int32, (1, x.shape[1]), 1)
    first = (lane % 16) < 8
    partner = jnp.where(first, pltpu.roll(x, x.shape[1] - 8, 1), pltpu.roll(x, 8, 1))
    return x * cos + partner * sin_signed


def _scan_kernel(n_ctx_blocks,
                 in_f_ref, in_b_ref,
                 cos_f_ref, sin_f_ref, cos_b_ref, sin_b_ref,
                 wgk_ref, bgk_ref, lb_ref, cum_f_ref, cum_b_ref,
                 of_ref, ob_ref,
                 sgf_ref, sgb_ref, shf_ref, shb_ref):
    del n_ctx_blocks
    gla_cols, hg_cols = pl.ds(0, GLA_WIDTH), pl.ds(GLA_WIDTH, HGRN_WIDTH)

    @pl.when(pl.program_id(1) == 0)
    def _reset():
        for ref in (sgf_ref, sgb_ref, shf_ref, shb_ref):
            ref[...] = jnp.zeros_like(ref)

    def gla_inputs(in_ref, cos_ref, sin_ref, direction, rows):
        blk = in_ref[0, rows, 0:W_GLA]
        v = blk[:, 0:GLA_WIDTH]
        q = blk[:, GLA_WIDTH:GLA_WIDTH + GLA_KEY]
        k = blk[:, GLA_WIDTH + GLA_KEY:GLA_WIDTH + 2 * GLA_KEY]
        gk = blk[:, GLA_WIDTH + 2 * GLA_KEY:]
        cos = cos_ref[rows, :]
        sin = sin_ref[rows, :]
        q = _rope(q, cos, sin) * (GLA_KEY // SCAN_HEADS) ** -0.5
        k = _rope(k, cos, sin)
        lo = direction * GLA_KEY
        z = _dot(gk.astype(BF16), wgk_ref[:, lo:lo + GLA_KEY]) + bgk_ref[:, lo:lo + GLA_KEY]
        log_sig = jnp.minimum(z, 0.0) - jnp.log(1.0 + jnp.exp(-jnp.abs(z)))
        return q, k, v, log_sig / GLA_GATE_NORM

    def hgrn_inputs(in_ref, direction, rows):
        blk = in_ref[0, rows, W_GLA:W_GLA + W_HG]
        q = blk[:, 0:HGRN_WIDTH] * HEAD_DIM ** -0.5
        f_raw = blk[:, (1 + direction) * HGRN_WIDTH:(2 + direction) * HGRN_WIDTH]
        v = blk[:, 3 * HGRN_WIDTH:]
        lb = lb_ref[direction:direction + 1, :]
        f = lb + (1.0 - lb) * _sigmoid(f_raw)
        return q, 1.0 - f, v, jnp.log(f)

    jobs, dests = [], []
    for s in range(SCAN_SUB):
        rows_f = pl.ds(s * SCAN_BLOCK, SCAN_BLOCK)
        rows_b = pl.ds((SCAN_SUB - 1 - s) * SCAN_BLOCK, SCAN_BLOCK)
        jobs += [
            _scan_direction(functools.partial(gla_inputs, in_f_ref, cos_f_ref, sin_f_ref, 0, rows_f),
                            cum_f_ref[...], sgf_ref, reverse=False),
            _scan_direction(functools.partial(hgrn_inputs, in_f_ref, 0, rows_f),
                            cum_f_ref[...], shf_ref, reverse=False),
            _scan_direction(functools.partial(gla_inputs, in_b_ref, cos_b_ref, sin_b_ref, 1, rows_b),
                            cum_b_ref[...], sgb_ref, reverse=True),
            _scan_direction(functools.partial(hgrn_inputs, in_b_ref, 1, rows_b),
                            cum_b_ref[...], shb_ref, reverse=True),
        ]
        dests += [(of_ref, rows_f, gla_cols), (of_ref, rows_f, hg_cols),
                  (ob_ref, rows_b, gla_cols), (ob_ref, rows_b, hg_cols)]
    for (ref, rows, cols), val in zip(dests, _run_interleaved(jobs)):
        ref[0, rows, cols] = val.astype(ref.dtype)


def _scans(scan_in, cos_t, sin_t, wgk, bgk, lb, consts):
    n_batch, t_all, _ = scan_in.shape
    step_rows = SCAN_SUB * SCAN_BLOCK
    assert CTX_LEN % step_rows == 0 and t_all % step_rows == 0
    n_blocks = t_all // step_rows
    n_ctx_blocks = CTX_LEN // step_rows

    def fwd(i):
        return i

    def bwd(i):
        return jnp.where(i < n_ctx_blocks, n_ctx_blocks - 1 - i, n_blocks - 1 + n_ctx_blocks - i)

    def tok_spec(width, order):
        return pl.BlockSpec((1, step_rows, width), lambda b, i: (b, order(i), 0))

    def tab_spec(order):
        return pl.BlockSpec((step_rows, LANES), lambda b, i: (order(i), 0))

    def const_spec(arr):
        return pl.BlockSpec(arr.shape, lambda b, i: (0,) * arr.ndim)

    const_arrays = [wgk, bgk, lb, consts["cum_f"], consts["cum_b"]]
    out_shape = jax.ShapeDtypeStruct((n_batch, t_all, GLA_WIDTH + HGRN_WIDTH), BF16)
    return pl.pallas_call(
        functools.partial(_scan_kernel, n_ctx_blocks),
        grid=(n_batch, n_blocks),
        in_specs=[tok_spec(W_GLA + W_HG, fwd), tok_spec(W_GLA + W_HG, bwd),
                  tab_spec(fwd), tab_spec(fwd), tab_spec(bwd), tab_spec(bwd)]
                 + [const_spec(a) for a in const_arrays],
        out_specs=[tok_spec(GLA_WIDTH + HGRN_WIDTH, fwd), tok_spec(GLA_WIDTH + HGRN_WIDTH, bwd)],
        out_shape=[out_shape] * 2,
        scratch_shapes=[pltpu.VMEM((HEAD_DIM, GLA_KEY), F32), pltpu.VMEM((HEAD_DIM, GLA_KEY), F32),
                        pltpu.VMEM((HEAD_DIM, HGRN_WIDTH), F32), pltpu.VMEM((HEAD_DIM, HGRN_WIDTH), F32)],
        compiler_params=pltpu.CompilerParams(
            dimension_semantics=("parallel", "arbitrary"), vmem_limit_bytes=VMEM_LIMIT),
        name="bidirectional_scans",
    )(scan_in, scan_in, cos_t, sin_t, cos_t, sin_t, *const_arrays)


def _rope_tables(t_lat):
    half = GLA_KEY // SCAN_HEADS // 2
    inv_freq = 1.0 / (ROPE_BASE ** (np.arange(0, half, 2, dtype=np.float32) / half))
    t = np.arange(t_lat)
    lane = np.arange(LANES) % (2 * half)
    pos = np.where(lane[None, :] < half, (t // GRID_W)[:, None], (t % GRID_W)[:, None]).astype(np.float32)
    ang = (pos * inv_freq[(lane % half) % (half // 2)][None, :]).astype(np.float32)
    first = ((lane % half) < half // 2)[None, :]
    cos = np.cos(ang)
    sin = np.where(first, -np.sin(ang), np.sin(ang))
    cos = np.concatenate([np.ones((CTX_LEN, LANES), np.float32), cos], axis=0)
    sin = np.concatenate([np.zeros((CTX_LEN, LANES), np.float32), sin], axis=0)
    return jnp.asarray(cos, F32), jnp.asarray(sin, F32)


N_MIX_REFS = 9


def _mix_tile(x, row, ona_ref, of_ref, ob_ref, gate_ref, mod_ref, gnw_ref, hnw_ref, bd_ref, w_ref):
    gt = mod_ref[pl.ds(row, 1), :][:, 2 * D_MODEL:]
    bd = bd_ref[...]

    def head_norm(o, w):
        sq = o * o
        hi = sq.astype(BF16)
        lo = (sq - hi.astype(F32)).astype(BF16)
        ms = _dot(hi, bd) + _dot(lo, bd)
        return o * lax.rsqrt(ms + NORM_EPS) * w

    o_scan = of_ref[0].astype(F32) + ob_ref[0].astype(F32)
    o_gla = head_norm(o_scan[:, :GLA_WIDTH], gnw_ref[...])
    o_hg = head_norm(o_scan[:, GLA_WIDTH:], hnw_ref[...])
    o_cat = jnp.concatenate([ona_ref[0].astype(F32), o_gla, o_hg], axis=1)
    act = o_cat * _silu(gate_ref[0].astype(F32))
    y = _dot(act.astype(BF16), w_ref[...])
    return x + gt * y


def _final_out_kernel(x_ref, *refs):
    xn = _mix_tile(x_ref[0], pl.program_id(0), *refs[:N_MIX_REFS])
    fw_ref, out_ref = refs[N_MIX_REFS:]
    ms = jnp.mean(xn * xn, axis=-1, keepdims=True)
    out_ref[0] = xn * lax.rsqrt(ms + NORM_EPS) * fw_ref[...]


def _out_in_kernel(n_streams, n_batch, *refs):
    x, row = _token_tile(refs[:n_streams], n_batch)
    xn = _mix_tile(x, row, *refs[n_streams:n_streams + N_MIX_REFS])
    rest = refs[n_streams + N_MIX_REFS:]
    proj_in, stream_ref, proj_out = rest[:3], rest[3], rest[4:]
    stream_ref[0] = xn
    _project_tile(xn, row, *proj_in, *proj_out)


def _token_spec(width, tile, first_token):
    if first_token % tile == 0:
        return pl.BlockSpec((1, tile, width), lambda b, t: (b, t + first_token // tile, 0))
    return pl.BlockSpec((pl.Element(1), pl.Element(tile), pl.Element(width)),
                        lambda b, t: (b, pl.multiple_of(first_token + t * tile, TOKEN_TILE), 0))


def _mix_io(o_na, scans, gates, mod_l, gla_nw, hg_nw, bd, w_out_b, layer, tile=TOKEN_TILE, first_token=0):
    def tok(width):
        return _token_spec(width, tile, first_token)

    def const_spec(arr):
        return pl.BlockSpec(arr.shape, lambda b, t: (0,) * arr.ndim)

    consts = [mod_l, gla_nw, hg_nw, bd]
    specs = ([pl.BlockSpec((1, tile, NA_WIDTH), lambda b, t: (b, t, 0))]
             + [tok(GLA_WIDTH + HGRN_WIDTH)] * 2
             + [tok(W_GATE)] + [const_spec(a) for a in consts]
             + [pl.BlockSpec((None,) + w_out_b.shape[1:], lambda b, t: (layer, 0, 0))])
    return specs, [o_na, *scans, gates, *consts, w_out_b]


def _final_out(streams, mix_args, final_w, n_batch, t_all):
    (stream,) = streams
    t_lat = t_all - CTX_LEN
    assert t_lat % FINAL_TILE == 0
    mix_specs, mix_arrays = _mix_io(*mix_args, tile=FINAL_TILE, first_token=CTX_LEN)
    return pl.pallas_call(
        _final_out_kernel,
        grid=(n_batch, t_lat // FINAL_TILE),
        in_specs=[_token_spec(D_MODEL, FINAL_TILE, CTX_LEN)] + mix_specs
                 + [pl.BlockSpec((1, D_MODEL), lambda b, t: (0, 0))],
        out_specs=pl.BlockSpec((1, FINAL_TILE, D_MODEL), lambda b, t: (b, t, 0)),
        out_shape=jax.ShapeDtypeStruct((n_batch, t_lat, D_MODEL), F32),
        compiler_params=pltpu.CompilerParams(
            dimension_semantics=("parallel", "parallel"), vmem_limit_bytes=VMEM_LIMIT),
        name="out_proj_final",
    )(stream, *mix_arrays, final_w)


def _out_in_proj(streams, mix_args, mod_next, norm_w_next, w_in_p, next_layer, n_batch, t_all):
    mix_specs, mix_arrays = _mix_io(*mix_args)
    proj_specs, out_specs, out_shape = _projection_io(mod_next, next_layer, n_batch, t_all)
    stream_spec = pl.BlockSpec((1, TOKEN_TILE, D_MODEL), lambda b, t: (b, t, 0))
    stream_shape = jax.ShapeDtypeStruct((n_batch, t_all, D_MODEL), F32)
    outs = pl.pallas_call(
        functools.partial(_out_in_kernel, len(streams), n_batch),
        grid=(n_batch, t_all // TOKEN_TILE),
        in_specs=_stream_specs(len(streams)) + mix_specs + proj_specs,
        out_specs=[stream_spec] + out_specs,
        out_shape=[stream_shape] + out_shape,
        compiler_params=pltpu.CompilerParams(
            dimension_semantics=("parallel", "parallel"), vmem_limit_bytes=VMEM_LIMIT),
        name="out_in_proj",
    )(*streams, *mix_arrays, mod_next, norm_w_next, w_in_p)
    return outs[0], outs[1:]


def _pad_w_in(w):
    assert w.shape[-1] == IN_OFFS[-1]
    return jnp.pad(w.astype(BF16), [(0, 0)] * (w.ndim - 1) + [(0, W_IN_PAD - w.shape[-1])])


def kernel(x, c, ctx, c_ctx, ada_w, ada_b, norm_w, w_in, na_rpb, gla_w_gk, gla_b_gk, gla_norm_w,
           hgrn_lower_bounds, hgrn_norm_w, w_out, final_norm_w):
    n_batch, t_lat, _ = x.shape
    depth = ada_w.shape[0]
    t_all = CTX_LEN + t_lat
    assert ctx.shape[1] == CTX_LEN and t_lat % TOKEN_TILE == 0 and t_lat // GRID_W >= NA_KEY_ROWS

    lbs = jnp.cumsum(jax.nn.softmax(hgrn_lower_bounds.astype(F32), axis=0), axis=0)
    lbs = lbs - lbs[0:1]
    mod_rows = -(-(n_batch + 1) // 8) * 8
    cc = jnp.zeros((mod_rows, D_MODEL), F32).at[:n_batch].set(c).at[n_batch].set(c_ctx)
    mod = _modulation(cc, ada_w, ada_b)

    consts = {k_: jnp.asarray(v_, BF16) for k_, v_ in _scan_constants().items()}
    cos_t, sin_t = _rope_tables(t_lat)
    head_of = np.arange(GLA_WIDTH) // HEAD_DIM
    bd = jnp.asarray((head_of[:, None] == head_of[None, :]).astype(np.float32) / HEAD_DIM, BF16)

    w_in_p = _pad_w_in(w_in)
    w_out_b = w_out.astype(BF16)
    rpb_tiles = _na_bias_tiles(na_rpb)
    wgk = jnp.zeros((depth, LANES, 2 * GLA_KEY), F32)
    wgk = wgk.at[:, 0:GLA_GK_RANK, 0:GLA_KEY].set(gla_w_gk[:, 0])
    wgk = wgk.at[:, GLA_GK_RANK:2 * GLA_GK_RANK, GLA_KEY:].set(gla_w_gk[:, 1]).astype(BF16)

    streams = [ctx, x]
    projections = _in_proj(streams, mod[0], norm_w[0].reshape(1, D_MODEL), w_in_p, 0, n_batch, t_all)
    for layer in range(depth):
        last = layer == depth - 1
        bgk = gla_b_gk[layer].reshape(1, 2 * GLA_KEY)
        gla_nw = jnp.tile(gla_norm_w[layer], SCAN_HEADS).reshape(1, GLA_WIDTH)
        hg_nw = jnp.tile(hgrn_norm_w[layer], SCAN_HEADS).reshape(1, HGRN_WIDTH)

        q, k, vt, gates, scan_in = projections
        o_na = _neighborhood_attention(q, k, vt, rpb_tiles, layer, with_ctx_queries=not last)
        scans = _scans(scan_in, cos_t, sin_t, wgk[layer], bgk, lbs[layer], consts)
        mix_args = (o_na, scans, gates, mod[layer], gla_nw, hg_nw, bd, w_out_b, layer)
        if last:
            return _final_out(streams, mix_args, final_norm_w.reshape(1, D_MODEL), n_batch, t_all)
        stream, projections = _out_in_proj(streams, mix_args, mod[layer + 1],
                                           norm_w[layer + 1].reshape(1, D_MODEL), w_in_p, layer + 1,
                                           n_batch, t_all)
        streams = [stream]
```
